```python
import math
import jax
import jax.numpy as jnp
from jax import lax
import numpy as np

D_MODEL = 1024
BATCH = 32
SEQ = 256
DEPTH = 2
DEC_BATCH = 4
DEC_SEQ = 1024
PAST_LEN = 512

GRID_W = 64
SSM_HEAD_DIM = 64
SSM_INNER = D_MODEL
SSM_HEADS = SSM_INNER // SSM_HEAD_DIM
SSM_GROUPS = 2
SSM_HEADS_PER_GROUP = SSM_HEADS // SSM_GROUPS
D_STATE = 128
SSM_CONV = 5
SSM_CHUNK = 128
SSM_CONV_DIM = SSM_INNER + 2 * SSM_GROUPS * D_STATE
ATT_HEAD_DIM = 64
ATT_HEADS = D_MODEL // ATT_HEAD_DIM
ATT_KV_HEADS = 4
ATT_GROUP = ATT_HEADS // ATT_KV_HEADS
WINDOW = 128
ATT_BLOCK = 128
ATT_SCALE = ATT_HEAD_DIM ** -0.5
ROPE_BASE = 10000.0
ROPE_HALF = ATT_HEAD_DIM // 2
ROPE_QUARTER = ATT_HEAD_DIM // 4
SHORT_CONV = 3
N_EXPERTS = 32
TOP_K = 4
D_EXPERT = D_MODEL
SWIGLU_ALPHA = 1.702
SWIGLU_LIMIT = 7.0
N_EVEN = (DEPTH + 1) // 2
N_ODD = DEPTH // 2
DEEPNORM_ALPHA = (2 * DEPTH) ** 0.25
DEEPNORM_BETA = (8 * DEPTH) ** -0.25
LN_EPS = 1e-5
RMS_EPS = 1e-5
A_SIZES = (SSM_INNER, SSM_CONV_DIM, 2 * SSM_HEADS, ATT_HEADS * ATT_HEAD_DIM, ATT_KV_HEADS * ATT_HEAD_DIM, ATT_KV_HEADS * ATT_HEAD_DIM)
A_SPLITS = tuple(sum(A_SIZES[:n + 1]) for n in range(len(A_SIZES) - 1))
D_IN_A = sum(A_SIZES)
D_MIX_A = SSM_INNER + ATT_HEADS * ATT_HEAD_DIM

kernel_name = 'hybrid_ssd_swa_shortconv_moe_diffusion_step'


def layer_norm(x, w, b):
    xf = x.astype(jnp.float32)
    mu = jnp.mean(xf, axis=-1, keepdims=True)
    var = jnp.mean(jnp.square(xf - mu), axis=-1, keepdims=True)
    return ((xf - mu) * lax.rsqrt(var + LN_EPS) * w + b).astype(x.dtype)


def dwconv_centred(x, w):
    width, ch = w.shape
    return lax.conv_general_dilated(x, w[:, None, :].astype(x.dtype), window_strides=(1,),
                                    padding=[(width // 2, width // 2)],
                                    dimension_numbers=('NWC', 'WIO', 'NWC'), feature_group_count=ch)


def axial_rope(x):
    n_tok = x.shape[1]
    rows = n_tok // GRID_W
    row = jnp.repeat(jnp.arange(rows, dtype=jnp.float32), GRID_W)
    col = jnp.tile(jnp.arange(GRID_W, dtype=jnp.float32), rows)
    inv = ROPE_BASE ** (-jnp.arange(ROPE_QUARTER, dtype=jnp.float32) / ROPE_QUARTER)

    def rotate(xh, pos):
        ang = pos[:, None] * inv[None, :]
        cos = jnp.cos(ang)[None, :, None, :]
        sin = jnp.sin(ang)[None, :, None, :]
        x1, x2 = xh[..., :ROPE_QUARTER], xh[..., ROPE_QUARTER:]
        return jnp.concatenate([x1 * cos - x2 * sin, x2 * cos + x1 * sin], axis=-1)

    xf = x.astype(jnp.float32)
    out = jnp.concatenate([rotate(xf[..., :ROPE_HALF], row), rotate(xf[..., ROPE_HALF:], col)], axis=-1)
    return out.astype(x.dtype)


def sink_softmax(s, sink):
    sk = sink.astype(jnp.float32).reshape(ATT_KV_HEADS, ATT_GROUP)[None, :, :, None, None]
    m = jnp.maximum(jnp.max(s, axis=-1, keepdims=True), sk)
    p = jnp.exp(s - m)
    return p / (jnp.sum(p, axis=-1, keepdims=True) + jnp.exp(sk - m))


def attend_context(q, k, v, sink):
    b, l = q.shape[:2]
    nq = l // ATT_BLOCK
    qb = jnp.moveaxis(q.reshape(b, nq, ATT_BLOCK, ATT_KV_HEADS, ATT_GROUP, ATT_HEAD_DIM), 1, 0)
    kf = k.astype(jnp.float32)

    def block(qi):
        s = jnp.einsum('bqkgd,bskd->bkgqs', qi.astype(jnp.float32), kf) * ATT_SCALE
        p = sink_softmax(s, sink)
        return jnp.einsum('bkgqs,bskd->bqkgd', p.astype(v.dtype), v)

    out = lax.map(block, qb)
    return jnp.moveaxis(out, 0, 1).reshape(b, l, ATT_HEADS * ATT_HEAD_DIM)


def attend_latent(q, k, v, k_ctx, v_ctx, sink):
    b, l = q.shape[:2]
    nb = l // ATT_BLOCK
    qg = q.reshape(b, l, ATT_KV_HEADS, ATT_GROUP, ATT_HEAD_DIM)
    pad = ((0, 0), (ATT_BLOCK, ATT_BLOCK), (0, 0), (0, 0))
    kp = jnp.pad(k, pad)
    vp = jnp.pad(v, pad)
    kc = k_ctx.astype(jnp.float32)

    def block(i):
        start = i * ATT_BLOCK
        qi = lax.dynamic_slice_in_dim(qg, start, ATT_BLOCK, axis=1).astype(jnp.float32)
        ki = lax.dynamic_slice_in_dim(kp, start, 3 * ATT_BLOCK, axis=1)
        vi = lax.dynamic_slice_in_dim(vp, start, 3 * ATT_BLOCK, axis=1)
        qpos = start + jnp.arange(ATT_BLOCK)
        kpos = start - ATT_BLOCK + jnp.arange(3 * ATT_BLOCK)
        valid = (jnp.abs(qpos[:, None] - kpos[None, :]) <= WINDOW) & (kpos >= 0)[None, :] & (kpos < l)[None, :]
        s_loc = jnp.einsum('bqkgd,bskd->bkgqs', qi, ki.astype(jnp.float32)) * ATT_SCALE
        s_loc = jnp.where(valid, s_loc, -jnp.inf)
        s_ctx = jnp.einsum('bqkgd,bskd->bkgqs', qi, kc) * ATT_SCALE
        p = sink_softmax(jnp.concatenate([s_loc, s_ctx], axis=-1), sink)
        p_loc = p[..., :3 * ATT_BLOCK].astype(v.dtype)
        p_ctx = p[..., 3 * ATT_BLOCK:].astype(v_ctx.dtype)
        return jnp.einsum('bkgqs,bskd->bqkgd', p_loc, vi) + jnp.einsum('bkgqs,bskd->bqkgd', p_ctx, v_ctx)

    out = lax.map(block, jnp.arange(nb))
    return jnp.moveaxis(out, 0, 1).reshape(b, l, ATT_HEADS * ATT_HEAD_DIM)


def ssd_scan(x, dt, a, bm, cm, h0):
    b, l, g, j, p = x.shape
    n = bm.shape[-1]
    nc = l // SSM_CHUNK
    x = x.astype(jnp.float32).reshape(b, nc, SSM_CHUNK, g, j, p)
    dt = dt.astype(jnp.float32).reshape(b, nc, SSM_CHUNK, g, j)
    bm = bm.astype(jnp.float32).reshape(b, nc, SSM_CHUNK, g, n)
    cm = cm.astype(jnp.float32).reshape(b, nc, SSM_CHUNK, g, n)
    a_cum = jnp.cumsum(dt * a, axis=2)
    causal = jnp.tril(jnp.ones((SSM_CHUNK, SSM_CHUNK), dtype=bool))[:, :, None, None]
    seg = a_cum[:, :, :, None] - a_cum[:, :, None, :]
    decay = jnp.exp(jnp.where(causal, seg, -jnp.inf))
    xdt = x * dt[..., None]
    cb = jnp.einsum('bclgn,bcsgn->bclsg', cm, bm)
    y_diag = jnp.einsum('bclsg,bclsgj,bcsgjp->bclgjp', cb, decay, xdt)
    to_end = jnp.exp(a_cum[:, :, -1:] - a_cum)
    chunk_states = jnp.einsum('bcsgn,bcsgj,bcsgjp->bcgjpn', bm, to_end, xdt)
    chunk_decay = jnp.exp(a_cum[:, :, -1])

    def step(h, inp):
        s_c, d_c = inp
        return h * d_c[..., None, None] + s_c, h

    h_fin, h_in = lax.scan(step, h0.astype(jnp.float32),
                           (jnp.moveaxis(chunk_states, 1, 0), jnp.moveaxis(chunk_decay, 1, 0)))
    h_in = jnp.moveaxis(h_in, 0, 1)
    y_off = jnp.einsum('bclgn,bcgjpn,bclgj->bclgjp', cm, h_in, jnp.exp(a_cum))
    return (y_diag + y_off).reshape(b, l, g, j, p), h_fin


def ssd_mixer(z, xbc, dt_raw, a_log, dt_bias, d_skip, norm_w, h0):
    b, l = z.shape[:2]
    g, j = SSM_GROUPS, SSM_HEADS_PER_GROUP
    xs, bm, cm = jnp.split(xbc, [SSM_INNER, SSM_INNER + SSM_GROUPS * D_STATE], axis=-1)
    xs = xs.reshape(b, l, g, j, SSM_HEAD_DIM)
    bm = bm.reshape(b, l, g, D_STATE)
    cm = cm.reshape(b, l, g, D_STATE)
    dt = jax.nn.softplus(dt_raw.astype(jnp.float32).reshape(b, l, 2, g, j)
                         + dt_bias.astype(jnp.float32).reshape(2, g, j))
    a = -jnp.exp(a_log.astype(jnp.float32)).reshape(2, g, j)
    h0 = h0.reshape(b, 2, g, j, SSM_HEAD_DIM, D_STATE)
    flip = lambda t: jnp.flip(t, axis=1)
    y_f, h_f = ssd_scan(xs, dt[:, :, 0], a[0], bm, cm, h0[:, 0])
    y_b, h_b = ssd_scan(flip(xs), flip(dt[:, :, 1]), a[1], flip(bm), flip(cm), h0[:, 1])
    y = y_f + flip(y_b) + d_skip.astype(jnp.float32).reshape(g, j)[:, :, None] * xs.astype(jnp.float32)
    hg = (y.reshape(b, l, SSM_INNER) * jax.nn.silu(z.astype(jnp.float32))).reshape(b, l, g, SSM_INNER // g)
    hg = hg * lax.rsqrt(jnp.mean(jnp.square(hg), axis=-1, keepdims=True) + RMS_EPS)
    out = (hg.reshape(b, l, SSM_INNER) * norm_w.astype(jnp.float32)).astype(z.dtype)
    h_fin = jnp.stack([h_f, h_b], axis=1).reshape(b, 2, SSM_HEADS, SSM_HEAD_DIM, D_STATE)
    return out, h_fin


def moe_ffn(u, w_router, b_router, w_gate, b_gate, w_lin, b_lin, w_down, b_down):
    b, l, d = u.shape
    t = u.reshape(b * l, d)
    logits = (t @ w_router + b_router).astype(jnp.float32)
    top_val, top_idx = lax.top_k(logits, TOP_K)
    probs = jax.nn.softmax(top_val, axis=-1)
    gates = jnp.einsum('tk,tke->te', probs, jax.nn.one_hot(top_idx, N_EXPERTS, dtype=jnp.float32)).astype(t.dtype)

    def expert(acc, w):
        wg, bg, wl, bl, wd, bd, g = w
        hg = jnp.minimum(t @ wg + bg, SWIGLU_LIMIT)
        hl = jnp.clip(t @ wl + bl, -SWIGLU_LIMIT, SWIGLU_LIMIT)
        h = hg * jax.nn.sigmoid(SWIGLU_ALPHA * hg) * (hl + 1)
        return acc + g[:, None] * (h @ wd + bd), None

    acc, _ = lax.scan(expert, jnp.zeros_like(t), (w_gate, b_gate, w_lin, b_lin, w_down, b_down, gates.T))
    return acc.reshape(b, l, d)


def setup_inputs(seed: int = 0) -> dict:
    key = jax.random.key(seed)
    keys = iter(jax.random.split(key, 40))
    f32 = jnp.float32

    def nrm(shape, scale):
        return jax.random.normal(next(keys), shape, f32) * scale

    dt0 = jnp.exp(jax.random.uniform(next(keys), (N_EVEN, 2, SSM_HEADS), f32, math.log(1e-3), math.log(1e-1)))
    return {
        'x_prompt': nrm((BATCH, SEQ, D_MODEL), 1.0),
        'x_sample': nrm((DEC_BATCH, DEC_SEQ, D_MODEL), 1.0),
        'cache_k': nrm((DEC_BATCH, N_EVEN, PAST_LEN, ATT_KV_HEADS, ATT_HEAD_DIM), 1.0),
        'cache_v': nrm((DEC_BATCH, N_EVEN, PAST_LEN, ATT_KV_HEADS, ATT_HEAD_DIM), 1.0),
        'state_ssm': nrm((DEC_BATCH, N_EVEN, 2, SSM_HEADS, SSM_HEAD_DIM, D_STATE), 0.5),
        'c': nrm((DEC_BATCH, D_MODEL), 1.0),
        'c_ctx': nrm((D_MODEL,), 1.0),
        'w_mod': nrm((DEPTH, D_MODEL, 6 * D_MODEL), 0.5 * D_MODEL ** -0.5),
        'b_mod': nrm((DEPTH, 6 * D_MODEL), 0.01),
        'ln_w': 1.0 + nrm((DEPTH, 2, D_MODEL), 0.02),
        'ln_b': nrm((DEPTH, 2, D_MODEL), 0.02),
        'w_in_a': nrm((N_EVEN, D_MODEL, D_IN_A), D_MODEL ** -0.5),
        'conv_w_a': nrm((N_EVEN, SSM_CONV, SSM_CONV_DIM), SSM_CONV ** -0.5),
        'conv_b_a': nrm((N_EVEN, SSM_CONV_DIM), 0.01),
        'a_log': jnp.log(jax.random.uniform(next(keys), (N_EVEN, 2, SSM_HEADS), f32, 1.0, 16.0)),
        'dt_bias': dt0 + jnp.log(-jnp.expm1(-dt0)),
        'd_skip': 1.0 + nrm((N_EVEN, SSM_HEADS), 0.02),
        'ssm_norm_w': 1.0 + nrm((N_EVEN, SSM_INNER), 0.02),
        'attn_sink': nrm((N_EVEN, ATT_HEADS), 0.5),
        'w_out_a': nrm((N_EVEN, D_MIX_A, D_MODEL), DEEPNORM_BETA * D_MIX_A ** -0.5),
        'w_in_c': nrm((N_ODD, D_MODEL, 3 * D_MODEL), D_MODEL ** -0.5),
        'conv_w_c': nrm((N_ODD, SHORT_CONV, D_MODEL), SHORT_CONV ** -0.5),
        'w_out_c': nrm((N_ODD, D_MODEL, D_MODEL), DEEPNORM_BETA * D_MODEL ** -0.5),
        'w_router': nrm((DEPTH, D_MODEL, N_EXPERTS), D_MODEL ** -0.5),
        'b_router': nrm((DEPTH, N_EXPERTS), 0.01),
        'w_gate': nrm((DEPTH, N_EXPERTS, D_MODEL, D_EXPERT), D_MODEL ** -0.5),
        'b_gate': nrm((DEPTH, N_EXPERTS, D_EXPERT), 0.01),
        'w_lin': nrm((DEPTH, N_EXPERTS, D_MODEL, D_EXPERT), D_MODEL ** -0.5),
        'b_lin': nrm((DEPTH, N_EXPERTS, D_EXPERT), 0.01),
        'w_down': nrm((DEPTH, N_EXPERTS, D_EXPERT, D_MODEL), DEEPNORM_BETA * D_EXPERT ** -0.5),
        'b_down': nrm((DEPTH, N_EXPERTS, D_MODEL), 0.01),
    }


def reference(x_prompt, x_sample, cache_k, cache_v, state_ssm, c, c_ctx,
              w_mod, b_mod, ln_w, ln_b,
              w_in_a, conv_w_a, conv_b_a, a_log, dt_bias, d_skip, ssm_norm_w, attn_sink, w_out_a,
              w_in_c, conv_w_c, w_out_c,
              w_router, b_router, w_gate, b_gate, w_lin, b_lin, w_down, b_down):

    def even_mixer(j, u, ctx):
        b, l, _ = u.shape
        z, xbc, dt_raw, q, k, v = jnp.split(u @ w_in_a[j], A_SPLITS, axis=-1)
        xbc = jax.nn.silu(dwconv_centred(xbc, conv_w_a[j]) + conv_b_a[j])
        if ctx is None:
            h0 = jnp.zeros((b, 2, SSM_HEADS, SSM_HEAD_DIM, D_STATE), jnp.float32)
        else:
            h0 = ctx[2]
        y_ssm, h_fin = ssd_mixer(z, xbc, dt_raw, a_log[j], dt_bias[j], d_skip[j], ssm_norm_w[j], h0)
        q = q.reshape(b, l, ATT_HEADS, ATT_HEAD_DIM)
        k = k.reshape(b, l, ATT_KV_HEADS, ATT_HEAD_DIM)
        v = v.reshape(b, l, ATT_KV_HEADS, ATT_HEAD_DIM)
        if ctx is None:
            y_att = attend_context(q, k, v, attn_sink[j])
        else:
            y_att = attend_latent(axial_rope(q), axial_rope(k), v, ctx[0], ctx[1], attn_sink[j])
        y = jnp.concatenate([y_ssm, y_att.astype(y_ssm.dtype)], axis=-1) @ w_out_a[j]
        return y, (k, v, h_fin)

    def odd_mixer(j, u):
        bg, cg, xt = jnp.split(u @ w_in_c[j], 3, axis=-1)
        return (bg * dwconv_centred(cg * xt, conv_w_c[j])) @ w_out_c[j]

    def layer(i, x, mod, ctx):
        sh1, sc1, g1, sh2, sc2, g2 = jnp.split(mod, 6, axis=-1)
        u = x * (1 + sc1) + sh1
        if i % 2 == 0:
            mix, side = even_mixer(i // 2, u, ctx)
        else:
            mix, side = odd_mixer(i // 2, u), None
        x = layer_norm(DEEPNORM_ALPHA * x + g1 * mix, ln_w[i, 0], ln_b[i, 0])
        u = x * (1 + sc2) + sh2
        ff = moe_ffn(u, w_router[i], b_router[i], w_gate[i], b_gate[i], w_lin[i], b_lin[i], w_down[i], b_down[i])
        x = layer_norm(DEEPNORM_ALPHA * x + g2 * ff, ln_w[i, 1], ln_b[i, 1])
        return x, side

    h = x_prompt
    ks, vs, ss = [], [], []
    for i in range(DEPTH):
        mod = jax.nn.silu(c_ctx) @ w_mod[i] + b_mod[i]
        h, side = layer(i, h, mod, None)
        if side is not None:
            ks.append(side[0])
            vs.append(side[1])
            ss.append(side[2])
    y_prompt = h
    new_cache_k = jnp.stack(ks, axis=1)
    new_cache_v = jnp.stack(vs, axis=1)
    new_state_ssm = jnp.stack(ss, axis=1)

    h = x_sample
    for i in range(DEPTH):
        mod = (jax.nn.silu(c) @ w_mod[i] + b_mod[i])[:, None, :]
        ctx = (cache_k[:, i // 2], cache_v[:, i // 2], state_ssm[:, i // 2]) if i % 2 == 0 else None
        h, _ = layer(i, h, mod, ctx)
    y_sample = h

    return (y_prompt, y_sample, new_cache_k, new_cache_v, new_state_ssm)
```

```python
import functools
import math

import jax
import jax.numpy as jnp
from jax import lax
from jax.experimental import pallas as pl
from jax.experimental.pallas import tpu as pltpu

D_MODEL = 1024
BATCH = 32
SEQ = 256
DEPTH = 2
DEC_BATCH = 4
DEC_SEQ = 1024
PAST_LEN = 512
GRID_W = 64
SSM_HEAD_DIM = 64
SSM_INNER = D_MODEL
SSM_HEADS = SSM_INNER // SSM_HEAD_DIM
SSM_GROUPS = 2
SSM_HEADS_PER_GROUP = SSM_HEADS // SSM_GROUPS
D_STATE = 128
SSM_CONV = 5
SSM_CHUNK = 128
SSM_CONV_DIM = SSM_INNER + 2 * SSM_GROUPS * D_STATE
ATT_HEAD_DIM = 64
ATT_HEADS = D_MODEL // ATT_HEAD_DIM
ATT_KV_HEADS = 4
ATT_GROUP = ATT_HEADS // ATT_KV_HEADS
WINDOW = 128
ATT_BLOCK = 128
ATT_SCALE = ATT_HEAD_DIM ** -0.5
ROPE_BASE = 10000.0
ROPE_HALF = ATT_HEAD_DIM // 2
ROPE_QUARTER = ATT_HEAD_DIM // 4
SHORT_CONV = 3
N_EXPERTS = 32
TOP_K = 4
D_EXPERT = D_MODEL
SWIGLU_ALPHA = 1.702
SWIGLU_LIMIT = 7.0
N_EVEN = (DEPTH + 1) // 2
DEEPNORM_ALPHA = (2 * DEPTH) ** 0.25
LN_EPS = 1e-5
RMS_EPS = 1e-5
KV_DIM = ATT_KV_HEADS * ATT_HEAD_DIM

N_PROMPT = BATCH * SEQ
N_SAMPLE = DEC_BATCH * DEC_SEQ
N_TOK = N_PROMPT + N_SAMPLE
N_COND = 1 + DEC_BATCH
COND_PAD = 8
MOD_ROWS = 8

TOK_TILE = 512
MOE_TILE = 256
MOE_ROWS = N_TOK * TOP_K + N_EXPERTS * MOE_TILE
MOE_TILES = MOE_ROWS // MOE_TILE
VMEM_LIMIT = 56 * 1024 * 1024

assert N_PROMPT % TOK_TILE == 0 and DEC_SEQ % TOK_TILE == 0


def _cparams(sem):
    return pltpu.CompilerParams(dimension_semantics=sem, vmem_limit_bytes=VMEM_LIMIT)


def _cond_row(i):
    first = N_PROMPT // TOK_TILE
    per = DEC_SEQ // TOK_TILE
    return jnp.where(i < first, 0, 1 + (i - first) // per)


def _bf16_dot(a, b):
    return jnp.dot(a.astype(jnp.bfloat16), b.astype(jnp.bfloat16), preferred_element_type=jnp.float32)


def _mod_kernel(c_ref, w_ref, b_ref, o_ref):
    c = c_ref[...]
    s = c * jax.nn.sigmoid(c)
    o_ref[...] = _bf16_dot(s, w_ref[...]) + b_ref[...]


def _modulation(cond, w_mod, b_mod):
    tn = 1536
    return pl.pallas_call(
        _mod_kernel,
        grid=(DEPTH, 6 * D_MODEL // tn),
        in_specs=[
            pl.BlockSpec((COND_PAD, D_MODEL), lambda l, n: (0, 0)),
            pl.BlockSpec((None, D_MODEL, tn), lambda l, n: (l, 0, n)),
            pl.BlockSpec((None, 1, tn), lambda l, n: (l, 0, n)),
        ],
        out_specs=pl.BlockSpec((None, COND_PAD, tn), lambda l, n: (l, 0, n)),
        out_shape=jax.ShapeDtypeStruct((DEPTH, COND_PAD, 6 * D_MODEL), jnp.float32),
        compiler_params=_cparams(("arbitrary", "arbitrary")),
        name="modulation",
    )(cond, w_mod, b_mod.reshape(DEPTH, 1, 6 * D_MODEL))


def _inproj_kernel(x_ref, mod_ref, *refs):
    n = len(refs) // 2
    w_refs, o_refs = refs[:n], refs[n:]
    u = x_ref[...] * (1.0 + mod_ref[1:2, :]) + mod_ref[0:1, :]
    ub = u.astype(jnp.bfloat16)
    for w_ref, o_ref in zip(w_refs, o_refs):
        o_ref[...] = jnp.dot(ub, w_ref[...], preferred_element_type=jnp.float32)


def _inproj(x, mod, weights):
    return pl.pallas_call(
        _inproj_kernel,
        grid=(N_TOK // TOK_TILE,),
        in_specs=[
            pl.BlockSpec((TOK_TILE, D_MODEL), lambda i: (i, 0)),
            pl.BlockSpec((None, MOD_ROWS, D_MODEL), lambda i: (_cond_row(i), 0, 0)),
        ] + [pl.BlockSpec(w.shape, lambda i: (0, 0)) for w in weights],
        out_specs=[pl.BlockSpec((TOK_TILE, w.shape[1]), lambda i: (i, 0)) for w in weights],
        out_shape=[jax.ShapeDtypeStruct((N_TOK, w.shape[1]), jnp.float32) for w in weights],
        compiler_params=_cparams(("arbitrary",)),
        name="inproj",
    )(x, mod, *weights)


def _layer_norm(v, w, b):
    mu = jnp.mean(v, axis=-1, keepdims=True)
    d = v - mu
    var = jnp.mean(d * d, axis=-1, keepdims=True)
    return d * lax.rsqrt(var + LN_EPS) * w + b


def _outproj_kernel(*refs, n_in):
    y_refs = refs[:n_in]
    w_refs = refs[n_in:2 * n_in]
    x_ref, mod_ref, ln_ref, wr_ref, br_ref, x1_ref, u2_ref, lg_ref = refs[2 * n_in:]
    mix = _bf16_dot(y_refs[0][...], w_refs[0][...])
    for y_ref, w_ref in zip(y_refs[1:], w_refs[1:]):
        mix = mix + _bf16_dot(y_ref[...], w_ref[...])
    x1 = _layer_norm(DEEPNORM_ALPHA * x_ref[...] + mod_ref[2:3, :] * mix, ln_ref[0:1, :], ln_ref[1:2, :])
    x1_ref[...] = x1
    u2 = x1 * (1.0 + mod_ref[4:5, :]) + mod_ref[3:4, :]
    u2_ref[...] = u2.astype(jnp.bfloat16)
    lg_ref[...] = jnp.dot(u2, wr_ref[...], preferred_element_type=jnp.float32,
                          precision=lax.Precision.HIGHEST) + br_ref[...]


def _outproj(ys, ws, x, mod, ln, w_router, b_router):
    n_in = len(ys)
    return pl.pallas_call(
        functools.partial(_outproj_kernel, n_in=n_in),
        grid=(N_TOK // TOK_TILE,),
        in_specs=[pl.BlockSpec((TOK_TILE, y.shape[1]), lambda i: (i, 0)) for y in ys]
        + [pl.BlockSpec(w.shape, lambda i: (0, 0)) for w in ws]
        + [
            pl.BlockSpec((TOK_TILE, D_MODEL), lambda i: (i, 0)),
            pl.BlockSpec((None, MOD_ROWS, D_MODEL), lambda i: (_cond_row(i), 0, 0)),
            pl.BlockSpec((2, D_MODEL), lambda i: (0, 0)),
            pl.BlockSpec((D_MODEL, N_EXPERTS), lambda i: (0, 0)),
            pl.BlockSpec((1, N_EXPERTS), lambda i: (0, 0)),
        ],
        out_specs=[
            pl.BlockSpec((TOK_TILE, D_MODEL), lambda i: (i, 0)),
            pl.BlockSpec((TOK_TILE, D_MODEL), lambda i: (i, 0)),
            pl.BlockSpec((TOK_TILE, N_EXPERTS), lambda i: (i, 0)),
        ],
        out_shape=[
            jax.ShapeDtypeStruct((N_TOK, D_MODEL), jnp.float32),
            jax.ShapeDtypeStruct((N_TOK, D_MODEL), jnp.bfloat16),
            jax.ShapeDtypeStruct((N_TOK, N_EXPERTS), jnp.float32),
        ],
        compiler_params=_cparams(("arbitrary",)),
        name="outproj_ln_router",
    )(*ys, *ws, x, mod, ln, w_router, b_router.reshape(1, N_EXPERTS))


def _moe_kernel(te_ref, nv_ref, x_ref, wg_ref, bg_ref, wl_ref, bl_ref, wd_ref, bd_ref, o_ref,
                wg_s, wl_s, wd_s):
    i = pl.program_id(0)
    valid = i < nv_ref[0]
    prev = te_ref[jnp.maximum(i - 1, 0)]
    new_expert = jnp.logical_or(i == 0, te_ref[i] != prev)

    @pl.when(jnp.logical_and(valid, new_expert))
    def _():
        wg_s[...] = wg_ref[...].astype(jnp.bfloat16)
        wl_s[...] = wl_ref[...].astype(jnp.bfloat16)
        wd_s[...] = wd_ref[...].astype(jnp.bfloat16)

    @pl.when(valid)
    def _():
        x = x_ref[...]
        hg = jnp.dot(x, wg_s[...], preferred_element_type=jnp.float32) + bg_ref[...]
        hl = jnp.dot(x, wl_s[...], preferred_element_type=jnp.float32) + bl_ref[...]
        hg = jnp.minimum(hg, SWIGLU_LIMIT)
        hl = jnp.clip(hl, -SWIGLU_LIMIT, SWIGLU_LIMIT)
        h = hg * jax.nn.sigmoid(SWIGLU_ALPHA * hg) * (hl + 1.0)
        o_ref[...] = jnp.dot(h.astype(jnp.bfloat16), wd_s[...], preferred_element_type=jnp.float32) + bd_ref[...]

    @pl.when(jnp.logical_not(valid))
    def _():
        o_ref[...] = jnp.zeros_like(o_ref)


def _moe_grouped(tile_expert, n_valid, xg, w_gate, b_gate, w_lin, b_lin, w_down, b_down):
    wspec = pl.BlockSpec((None, D_MODEL, D_EXPERT), lambda i, te, nv: (te[i], 0, 0))
    bspec = pl.BlockSpec((None, 1, D_EXPERT), lambda i, te, nv: (te[i], 0, 0))
    grid_spec = pltpu.PrefetchScalarGridSpec(
        num_scalar_prefetch=2,
        grid=(MOE_TILES,),
        in_specs=[pl.BlockSpec((MOE_TILE, D_MODEL), lambda i, te, nv: (i, 0)),
                  wspec, bspec, wspec, bspec, wspec, bspec],
        out_specs=pl.BlockSpec((MOE_TILE, D_MODEL), lambda i, te, nv: (i, 0)),
        scratch_shapes=[pltpu.VMEM((D_MODEL, D_EXPERT), jnp.bfloat16)] * 3,
    )
    return pl.pallas_call(
        _moe_kernel,
        grid_spec=grid_spec,
        out_shape=jax.ShapeDtypeStruct((MOE_ROWS, D_MODEL), jnp.float32),
        compiler_params=_cparams(("arbitrary",)),
        name="moe_grouped",
    )(tile_expert, n_valid, xg, w_gate, b_gate.reshape(N_EXPERTS, 1, D_EXPERT),
      w_lin, b_lin.reshape(N_EXPERTS, 1, D_EXPERT), w_down, b_down.reshape(N_EXPERTS, 1, D_MODEL))


def _combine_kernel(y_ref, g_ref, x_ref, mod_ref, ln_ref, o_ref):
    g = g_ref[...]
    ff = g[:, 0:1] * y_ref[0]
    for k in range(1, TOP_K):
        ff = ff + g[:, k:k + 1] * y_ref[k]
    o_ref[...] = _layer_norm(DEEPNORM_ALPHA * x_ref[...] + mod_ref[5:6, :] * ff, ln_ref[0:1, :], ln_ref[1:2, :])


def _combine(yk, gates, x, mod, ln):
    return pl.pallas_call(
        _combine_kernel,
        grid=(N_TOK // TOK_TILE,),
        in_specs=[
            pl.BlockSpec((TOP_K, TOK_TILE, D_MODEL), lambda i: (0, i, 0)),
            pl.BlockSpec((TOK_TILE, TOP_K), lambda i: (i, 0)),
            pl.BlockSpec((TOK_TILE, D_MODEL), lambda i: (i, 0)),
            pl.BlockSpec((None, MOD_ROWS, D_MODEL), lambda i: (_cond_row(i), 0, 0)),
            pl.BlockSpec((2, D_MODEL), lambda i: (0, 0)),
        ],
        out_specs=pl.BlockSpec((TOK_TILE, D_MODEL), lambda i: (i, 0)),
        out_shape=jax.ShapeDtypeStruct((N_TOK, D_MODEL), jnp.float32),
        compiler_params=_cparams(("arbitrary",)),
        name="combine_ln",
    )(yk, gates, x, mod, ln)


def _route(logits):
    top_val, top_idx = lax.top_k(logits, TOP_K)
    gates = jax.nn.softmax(top_val, axis=-1)
    onehot = jnp.sum(jax.nn.one_hot(top_idx, N_EXPERTS, dtype=jnp.int32), axis=1)
    incl = jnp.cumsum(onehot, axis=0)
    counts = incl[-1]
    rank = jnp.take_along_axis(incl - onehot, top_idx, axis=1)
    padded = (counts + MOE_TILE - 1) // MOE_TILE * MOE_TILE
    pend = jnp.cumsum(padded)
    pstart = pend - padded
    slot = pstart[top_idx] + rank
    tok = jnp.broadcast_to(jnp.arange(N_TOK, dtype=jnp.int32)[:, None], (N_TOK, TOP_K))
    slot_token = jnp.zeros((MOE_ROWS,), jnp.int32).at[slot.reshape(-1)].set(tok.reshape(-1))
    n_valid = (pend[-1] // MOE_TILE).astype(jnp.int32)
    tile_start = jnp.arange(MOE_TILES, dtype=jnp.int32) * MOE_TILE
    tile_expert = jnp.searchsorted(pend, tile_start, side='right').astype(jnp.int32)
    last_expert = tile_expert[jnp.maximum(n_valid - 1, 0)]
    tile_expert = jnp.where(tile_start < pend[-1], tile_expert, last_expert)
    return gates, slot, slot_token, tile_expert, n_valid.reshape(1)


def _moe_layer(i, x1, u2, logits, mod, ln, w_gate, b_gate, w_lin, b_lin, w_down, b_down):
    gates, slot, slot_token, tile_expert, n_valid = _route(logits)
    xg = jnp.take(u2, slot_token, axis=0)
    y = _moe_grouped(tile_expert, n_valid, xg, w_gate[i], b_gate[i], w_lin[i], b_lin[i], w_down[i], b_down[i])
    yk = jnp.take(y, slot.T, axis=0)
    return _combine(yk, gates, x1, mod, ln)


def _dwconv_centred(x, w):
    width, ch = w.shape
    return lax.conv_general_dilated(x, w[:, None, :].astype(x.dtype), window_strides=(1,),
                                    padding=[(width // 2, width // 2)],
                                    dimension_numbers=('NWC', 'WIO', 'NWC'), feature_group_count=ch)


def _axial_rope(x):
    n_tok = x.shape[1]
    rows = n_tok // GRID_W
    row = jnp.repeat(jnp.arange(rows, dtype=jnp.float32), GRID_W)
    col = jnp.tile(jnp.arange(GRID_W, dtype=jnp.float32), rows)
    inv = ROPE_BASE ** (-jnp.arange(ROPE_QUARTER, dtype=jnp.float32) / ROPE_QUARTER)

    def rotate(xh, pos):
        ang = pos[:, None] * inv[None, :]
        cos = jnp.cos(ang)[None, :, None, :]
        sin = jnp.sin(ang)[None, :, None, :]
        x1, x2 = xh[..., :ROPE_QUARTER], xh[..., ROPE_QUARTER:]
        return jnp.concatenate([x1 * cos - x2 * sin, x2 * cos + x1 * sin], axis=-1)

    return jnp.concatenate([rotate(x[..., :ROPE_HALF], row), rotate(x[..., ROPE_HALF:], col)], axis=-1)


def _sink_softmax(s, sink):
    sk = sink.astype(jnp.float32).reshape(ATT_KV_HEADS, ATT_GROUP)[None, :, :, None, None]
    m = jnp.maximum(jnp.max(s, axis=-1, keepdims=True), sk)
    p = jnp.exp(s - m)
    return p / (jnp.sum(p, axis=-1, keepdims=True) + jnp.exp(sk - m))


def _attend_context(q, k, v, sink):
    b, l = q.shape[:2]
    qg = q.reshape(b, l, ATT_KV_HEADS, ATT_GROUP, ATT_HEAD_DIM)
    s = jnp.einsum('bqkgd,bskd->bkgqs', qg, k) * ATT_SCALE
    p = _sink_softmax(s, sink)
    return jnp.einsum('bkgqs,bskd->bqkgd', p, v).reshape(b, l, ATT_HEADS * ATT_HEAD_DIM)


def _attend_latent(q, k, v, k_ctx, v_ctx, sink):
    b, l = q.shape[:2]
    nb = l // ATT_BLOCK
    qg = q.reshape(b, l, ATT_KV_HEADS, ATT_GROUP, ATT_HEAD_DIM)
    pad = ((0, 0), (ATT_BLOCK, ATT_BLOCK), (0, 0), (0, 0))
    kp = jnp.pad(k, pad)
    vp = jnp.pad(v, pad)

    def block(i):
        start = i * ATT_BLOCK
        qi = lax.dynamic_slice_in_dim(qg, start, ATT_BLOCK, axis=1)
        ki = lax.dynamic_slice_in_dim(kp, start, 3 * ATT_BLOCK, axis=1)
        vi = lax.dynamic_slice_in_dim(vp, start, 3 * ATT_BLOCK, axis=1)
        qpos = start + jnp.arange(ATT_BLOCK)
        kpos = start - ATT_BLOCK + jnp.arange(3 * ATT_BLOCK)
        valid = (jnp.abs(qpos[:, None] - kpos[None, :]) <= WINDOW) & (kpos >= 0)[None, :] & (kpos < l)[None, :]
        s_loc = jnp.einsum('bqkgd,bskd->bkgqs', qi, ki) * ATT_SCALE
        s_loc = jnp.where(valid, s_loc, -jnp.inf)
        s_ctx = jnp.einsum('bqkgd,bskd->bkgqs', qi, k_ctx) * ATT_SCALE
        p = _sink_softmax(jnp.concatenate([s_loc, s_ctx], axis=-1), sink)
        p_loc = p[..., :3 * ATT_BLOCK]
        p_ctx = p[..., 3 * ATT_BLOCK:]
        return jnp.einsum('bkgqs,bskd->bqkgd', p_loc, vi) + jnp.einsum('bkgqs,bskd->bqkgd', p_ctx, v_ctx)

    out = lax.map(block, jnp.arange(nb))
    return jnp.moveaxis(out, 0, 1).reshape(b, l, ATT_HEADS * ATT_HEAD_DIM)


def _ssd_scan(x, dt, a, bm, cm, h0):
    b, l, g, j, p = x.shape
    n = bm.shape[-1]
    nc = l // SSM_CHUNK
    x = x.reshape(b, nc, SSM_CHUNK, g, j, p)
    dt = dt.reshape(b, nc, SSM_CHUNK, g, j)
    bm = bm.reshape(b, nc, SSM_CHUNK, g, n)
    cm = cm.reshape(b, nc, SSM_CHUNK, g, n)
    a_cum = jnp.cumsum(dt * a, axis=2)
    causal = jnp.tril(jnp.ones((SSM_CHUNK, SSM_CHUNK), dtype=bool))[:, :, None, None]
    seg = a_cum[:, :, :, None] - a_cum[:, :, None, :]
    decay = jnp.exp(jnp.where(causal, seg, -jnp.inf))
    xdt = x * dt[..., None]
    cb = jnp.einsum('bclgn,bcsgn->bclsg', cm, bm)
    y_diag = jnp.einsum('bclsg,bclsgj,bcsgjp->bclgjp', cb, decay, xdt)
    to_end = jnp.exp(a_cum[:, :, -1:] - a_cum)
    chunk_states = jnp.einsum('bcsgn,bcsgj,bcsgjp->bcgjpn', bm, to_end, xdt)
    chunk_decay = jnp.exp(a_cum[:, :, -1])

    def step(h, inp):
        s_c, d_c = inp
        return h * d_c[..., None, None] + s_c, h

    h_fin, h_in = lax.scan(step, h0, (jnp.moveaxis(chunk_states, 1, 0), jnp.moveaxis(chunk_decay, 1, 0)))
    h_in = jnp.moveaxis(h_in, 0, 1)
    y_off = jnp.einsum('bclgn,bcgjpn,bclgj->bclgjp', cm, h_in, jnp.exp(a_cum))
    return (y_diag + y_off).reshape(b, l, g, j, p), h_fin


def _ssd_mixer(z, xbc, dt_raw, a_log, dt_bias, d_skip, norm_w, h0):
    b, l = z.shape[:2]
    g, j = SSM_GROUPS, SSM_HEADS_PER_GROUP
    xs, bm, cm = jnp.split(xbc, [SSM_INNER, SSM_INNER + SSM_GROUPS * D_STATE], axis=-1)
    xs = xs.reshape(b, l, g, j, SSM_HEAD_DIM)
    bm = bm.reshape(b, l, g, D_STATE)
    cm = cm.reshape(b, l, g, D_STATE)
    dt = jax.nn.softplus(dt_raw.reshape(b, l, 2, g, j) + dt_bias.reshape(2, g, j))
    a = -jnp.exp(a_log).reshape(2, g, j)
    h0 = h0.reshape(b, 2, g, j, SSM_HEAD_DIM, D_STATE)
    flip = lambda t: jnp.flip(t, axis=1)
    y_f, h_f = _ssd_scan(xs, dt[:, :, 0], a[0], bm, cm, h0[:, 0])
    y_b, h_b = _ssd_scan(flip(xs), flip(dt[:, :, 1]), a[1], flip(bm), flip(cm), h0[:, 1])
    y = y_f + flip(y_b) + d_skip.reshape(g, j)[:, :, None] * xs
    hg = (y.reshape(b, l, SSM_INNER) * jax.nn.silu(z)).reshape(b, l, g, SSM_INNER // g)
    hg = hg * lax.rsqrt(jnp.mean(jnp.square(hg), axis=-1, keepdims=True) + RMS_EPS)
    out = hg.reshape(b, l, SSM_INNER) * norm_w
    h_fin = jnp.stack([h_f, h_b], axis=1).reshape(b, 2, SSM_HEADS, SSM_HEAD_DIM, D_STATE)
    return out, h_fin


def _even_mixer(z, xbc, dt_raw, q, k, v, ctx, conv_w, conv_b, a_log, dt_bias, d_skip, norm_w, sink):
    b, l = z.shape[:2]
    xbc = jax.nn.silu(_dwconv_centred(xbc, conv_w) + conv_b)
    if ctx is None:
        h0 = jnp.zeros((b, 2, SSM_HEADS, SSM_HEAD_DIM, D_STATE), jnp.float32)
    else:
        h0 = ctx[2]
    y_ssm, h_fin = _ssd_mixer(z, xbc, dt_raw, a_log, dt_bias, d_skip, norm_w, h0)
    q = q.reshape(b, l, ATT_HEADS, ATT_HEAD_DIM)
    k = k.reshape(b, l, ATT_KV_HEADS, ATT_HEAD_DIM)
    v = v.reshape(b, l, ATT_KV_HEADS, ATT_HEAD_DIM)
    if ctx is None:
        y_att = _attend_context(q, k, v, sink)
    else:
        y_att = _attend_latent(_axial_rope(q), _axial_rope(k), v, ctx[0], ctx[1], sink)
    return y_ssm, y_att, (k, v, h_fin)


def _split_tokens(t):
    return t[:N_PROMPT].reshape(BATCH, SEQ, -1), t[N_PROMPT:].reshape(DEC_BATCH, DEC_SEQ, -1)


def _join_tokens(tp, ts):
    return jnp.concatenate([tp.reshape(N_PROMPT, -1), ts.reshape(N_SAMPLE, -1)], axis=0)


def kernel(x_prompt, x_sample, cache_k, cache_v, state_ssm, c, c_ctx,
           w_mod, b_mod, ln_w, ln_b,
           w_in_a, conv_w_a, conv_b_a, a_log, dt_bias, d_skip, ssm_norm_w, attn_sink, w_out_a,
           w_in_c, conv_w_c, w_out_c,
           w_router, b_router, w_gate, b_gate, w_lin, b_lin, w_down, b_down):
    bf16 = jnp.bfloat16
    x = _join_tokens(x_prompt, x_sample)

    cond = jnp.concatenate([c_ctx[None, :], c, jnp.zeros((COND_PAD - N_COND, D_MODEL), jnp.float32)], axis=0)
    mod_all = _modulation(cond, w_mod, b_mod).reshape(DEPTH, COND_PAD, 6, D_MODEL)
    mod_all = jnp.pad(mod_all, ((0, 0), (0, 0), (0, MOD_ROWS - 6), (0, 0)))

    mod = mod_all[0]
    wa = w_in_a[0].astype(bf16)
    edges = (0, SSM_INNER, SSM_INNER + SSM_CONV_DIM, SSM_INNER + SSM_CONV_DIM + 2 * SSM_HEADS)
    w_parts = [wa[:, edges[0]:edges[1]], wa[:, edges[1]:edges[2]], wa[:, edges[2]:edges[3]],
               wa[:, edges[3]:edges[3] + D_MODEL], wa[:, edges[3] + D_MODEL:edges[3] + D_MODEL + KV_DIM],
               wa[:, edges[3] + D_MODEL + KV_DIM:]]
    z, xbc, dt_raw, q, k, v = _inproj(x, mod, w_parts)

    parts_p, parts_s = zip(*[_split_tokens(t) for t in (z, xbc, dt_raw, q, k, v)])
    args = (conv_w_a[0], conv_b_a[0], a_log[0], dt_bias[0], d_skip[0], ssm_norm_w[0], attn_sink[0])
    ys_p, ya_p, side = _even_mixer(*parts_p, None, *args)
    ctx = (cache_k[:, 0], cache_v[:, 0], state_ssm[:, 0])
    ys_s, ya_s, _ = _even_mixer(*parts_s, ctx, *args)
    y_ssm = _join_tokens(ys_p, ys_s)
    y_att = _join_tokens(ya_p, ya_s)

    wo = w_out_a[0].astype(bf16)
    ln = jnp.stack([ln_w[0, 0], ln_b[0, 0]])
    x1, u2, logits = _outproj([y_ssm, y_att], [wo[:SSM_INNER], wo[SSM_INNER:]], x, mod, ln,
                              w_router[0], b_router[0])
    x = _moe_layer(0, x1, u2, logits, mod, jnp.stack([ln_w[0, 1], ln_b[0, 1]]),
                   w_gate, b_gate, w_lin, b_lin, w_down, b_down)

    mod = mod_all[1]
    wc = w_in_c[0].astype(bf16)
    bg, cg, xt = _inproj(x, mod, [wc[:, :D_MODEL], wc[:, D_MODEL:2 * D_MODEL], wc[:, 2 * D_MODEL:]])
    gp, gs = _split_tokens(cg * xt)
    conv = _join_tokens(_dwconv_centred(gp, conv_w_c[0]), _dwconv_centred(gs, conv_w_c[0]))
    ln = jnp.stack([ln_w[1, 0], ln_b[1, 0]])
    x1, u2, logits = _outproj([bg * conv], [w_out_c[0].astype(bf16)], x, mod, ln, w_router[1], b_router[1])
    x = _moe_layer(1, x1, u2, logits, mod, jnp.stack([ln_w[1, 1], ln_b[1, 1]]),
                   w_gate, b_gate, w_lin, b_lin, w_down, b_down)

    y_prompt = x[:N_PROMPT].reshape(BATCH, SEQ, D_MODEL)
    y_sample = x[N_PROMPT:].reshape(DEC_BATCH, DEC_SEQ, D_MODEL)
    k_new, v_new, h_new = side
    return (y_prompt, y_sample, k_new[:, None], v_new[:, None], h_new[:, None])
```

```python
import functools

import numpy as np
import jax
import jax.numpy as jnp
from jax import lax
from jax.experimental import pallas as pl
from jax.experimental.pallas import tpu as pltpu

D_MODEL = 1024
BATCH = 32
SEQ = 256
DEPTH = 2
DEC_BATCH = 4
DEC_SEQ = 1024
PAST_LEN = 512
GRID_W = 64
SSM_HEAD_DIM = 64
SSM_INNER = D_MODEL
SSM_HEADS = SSM_INNER // SSM_HEAD_DIM
SSM_GROUPS = 2
SSM_HEADS_PER_GROUP = SSM_HEADS // SSM_GROUPS
D_STATE = 128
SSM_CONV = 5
SSM_CHUNK = 128
SSM_BC = SSM_GROUPS * D_STATE
SSM_CONV_DIM = SSM_INNER + 2 * SSM_BC
ATT_HEAD_DIM = 64
ATT_HEADS = D_MODEL // ATT_HEAD_DIM
ATT_KV_HEADS = 4
ATT_GROUP = ATT_HEADS // ATT_KV_HEADS
WINDOW = 128
ATT_BLOCK = 128
ATT_SCALE = ATT_HEAD_DIM ** -0.5
ROPE_BASE = 10000.0
ROPE_HALF = ATT_HEAD_DIM // 2
ROPE_QUARTER = ATT_HEAD_DIM // 4
SHORT_CONV = 3
N_EXPERTS = 32
TOP_K = 4
D_EXPERT = D_MODEL
SWIGLU_ALPHA = 1.702
SWIGLU_LIMIT = 7.0
N_EVEN = (DEPTH + 1) // 2
DEEPNORM_ALPHA = (2 * DEPTH) ** 0.25
LN_EPS = 1e-5
RMS_EPS = 1e-5
KV_DIM = ATT_KV_HEADS * ATT_HEAD_DIM

N_PROMPT = BATCH * SEQ
N_SAMPLE = DEC_BATCH * DEC_SEQ
N_TOK = N_PROMPT + N_SAMPLE
N_COND = 1 + DEC_BATCH
SUBLANES = 8
LANES = 128
COND_PAD = SUBLANES
MOD_ROWS = SUBLANES
DT_PAD = LANES

TOK_TILE = 512
CONV_TILE = 256
MOE_TILE = 256
MOE_ROWS = N_TOK * TOP_K + N_EXPERTS * MOE_TILE
MOE_TILES = MOE_ROWS // MOE_TILE
VMEM_LIMIT = 56 * 1024 * 1024
NEG_BIG = -1e30

assert N_PROMPT % TOK_TILE == 0 and DEC_SEQ % TOK_TILE == 0
assert SEQ % CONV_TILE == 0 and DEC_SEQ % CONV_TILE == 0

_NT = (((1,), (1,)), ((), ()))
_TN = (((0,), (0,)), ((), ()))


def _cparams(sem):
    return pltpu.CompilerParams(dimension_semantics=sem, vmem_limit_bytes=VMEM_LIMIT)


def _cond_row(i):
    first = N_PROMPT // TOK_TILE
    per = DEC_SEQ // TOK_TILE
    return jnp.where(i < first, 0, 1 + (i - first) // per)


def _bf16_dot(a, b):
    return jnp.dot(a.astype(jnp.bfloat16), b.astype(jnp.bfloat16), preferred_element_type=jnp.float32)


def _mod_kernel(c_ref, w_ref, b_ref, o_ref):
    c = c_ref[...]
    s = c * jax.nn.sigmoid(c)
    o_ref[...] = _bf16_dot(s, w_ref[...]) + b_ref[...]


def _modulation(cond, w_mod, b_mod):
    tn = 1536
    return pl.pallas_call(
        _mod_kernel,
        grid=(DEPTH, 6 * D_MODEL // tn),
        in_specs=[
            pl.BlockSpec((COND_PAD, D_MODEL), lambda l, n: (0, 0)),
            pl.BlockSpec((None, D_MODEL, tn), lambda l, n: (l, 0, n)),
            pl.BlockSpec((None, 1, tn), lambda l, n: (l, 0, n)),
        ],
        out_specs=pl.BlockSpec((None, COND_PAD, tn), lambda l, n: (l, 0, n)),
        out_shape=jax.ShapeDtypeStruct((DEPTH, COND_PAD, 6 * D_MODEL), jnp.float32),
        compiler_params=_cparams(("arbitrary", "arbitrary")),
        name="modulation",
    )(cond, w_mod, b_mod.reshape(DEPTH, 1, 6 * D_MODEL))


def _inproj_kernel(x_ref, mod_ref, *refs, gate_product):
    n = len(refs) // 2 + (1 if gate_product else 0)
    w_refs, o_refs = refs[:n], refs[n:]
    u = x_ref[...] * (1.0 + mod_ref[1:2, :]) + mod_ref[0:1, :]
    ub = u.astype(jnp.bfloat16)
    outs = [jnp.dot(ub, w_ref[...], preferred_element_type=jnp.float32) for w_ref in w_refs]
    if gate_product:
        outs = outs[:-2] + [outs[-2] * outs[-1]]
    for o_ref, o in zip(o_refs, outs):
        o_ref[...] = o


def _inproj(x, mod, weights, gate_product=False):
    out_w = [w.shape[1] for w in weights]
    if gate_product:
        out_w = out_w[:-1]
    return pl.pallas_call(
        functools.partial(_inproj_kernel, gate_product=gate_product),
        grid=(N_TOK // TOK_TILE,),
        in_specs=[
            pl.BlockSpec((TOK_TILE, D_MODEL), lambda i: (i, 0)),
            pl.BlockSpec((None, MOD_ROWS, D_MODEL), lambda i: (_cond_row(i), 0, 0)),
        ] + [pl.BlockSpec(w.shape, lambda i: (0, 0)) for w in weights],
        out_specs=[pl.BlockSpec((TOK_TILE, n), lambda i: (i, 0)) for n in out_w],
        out_shape=[jax.ShapeDtypeStruct((N_TOK, n), jnp.float32) for n in out_w],
        compiler_params=_cparams(("arbitrary",)),
        name="inproj",
    )(x, mod, *weights)


def _layer_norm(v, w, b):
    mu = jnp.mean(v, axis=-1, keepdims=True)
    d = v - mu
    var = jnp.mean(d * d, axis=-1, keepdims=True)
    return d * lax.rsqrt(var + LN_EPS) * w + b


def _outproj_kernel(*refs, n_in):
    y_refs = refs[:n_in]
    w_refs = refs[n_in:2 * n_in]
    x_ref, mod_ref, ln_ref, wr_ref, br_ref, x1_ref, u2_ref, lg_ref = refs[2 * n_in:]
    mix = _bf16_dot(y_refs[0][...], w_refs[0][...])
    for y_ref, w_ref in zip(y_refs[1:], w_refs[1:]):
        mix = mix + _bf16_dot(y_ref[...], w_ref[...])
    x1 = _layer_norm(DEEPNORM_ALPHA * x_ref[...] + mod_ref[2:3, :] * mix, ln_ref[0:1, :], ln_ref[1:2, :])
    x1_ref[...] = x1
    u2 = x1 * (1.0 + mod_ref[4:5, :]) + mod_ref[3:4, :]
    u2_ref[...] = u2.astype(jnp.bfloat16)
    lg_ref[...] = jnp.dot(u2, wr_ref[...], preferred_element_type=jnp.float32,
                          precision=lax.Precision.HIGHEST) + br_ref[...]


def _outproj(ys, ws, x, mod, ln, w_router, b_router):
    n_in = len(ys)
    return pl.pallas_call(
        functools.partial(_outproj_kernel, n_in=n_in),
        grid=(N_TOK // TOK_TILE,),
        in_specs=[pl.BlockSpec((TOK_TILE, y.shape[1]), lambda i: (i, 0)) for y in ys]
        + [pl.BlockSpec(w.shape, lambda i: (0, 0)) for w in ws]
        + [
            pl.BlockSpec((TOK_TILE, D_MODEL), lambda i: (i, 0)),
            pl.BlockSpec((None, MOD_ROWS, D_MODEL), lambda i: (_cond_row(i), 0, 0)),
            pl.BlockSpec((2, D_MODEL), lambda i: (0, 0)),
            pl.BlockSpec((D_MODEL, N_EXPERTS), lambda i: (0, 0)),
            pl.BlockSpec((1, N_EXPERTS), lambda i: (0, 0)),
        ],
        out_specs=[
            pl.BlockSpec((TOK_TILE, D_MODEL), lambda i: (i, 0)),
            pl.BlockSpec((TOK_TILE, D_MODEL), lambda i: (i, 0)),
            pl.BlockSpec((TOK_TILE, N_EXPERTS), lambda i: (i, 0)),
        ],
        out_shape=[
            jax.ShapeDtypeStruct((N_TOK, D_MODEL), jnp.float32),
            jax.ShapeDtypeStruct((N_TOK, D_MODEL), jnp.bfloat16),
            jax.ShapeDtypeStruct((N_TOK, N_EXPERTS), jnp.float32),
        ],
        compiler_params=_cparams(("arbitrary",)),
        name="outproj_ln_router",
    )(*ys, *ws, x, mod, ln, w_router, b_router.reshape(1, N_EXPERTS))


def _moe_kernel(te_ref, nv_ref, x_ref, wg_ref, bg_ref, wl_ref, bl_ref, wd_ref, bd_ref, o_ref,
                wg_s, wl_s, wd_s):
    i = pl.program_id(0)
    valid = i < nv_ref[0]
    prev = te_ref[jnp.maximum(i - 1, 0)]
    new_expert = jnp.logical_or(i == 0, te_ref[i] != prev)

    @pl.when(jnp.logical_and(valid, new_expert))
    def _():
        wg_s[...] = wg_ref[...].astype(jnp.bfloat16)
        wl_s[...] = wl_ref[...].astype(jnp.bfloat16)
        wd_s[...] = wd_ref[...].astype(jnp.bfloat16)

    @pl.when(valid)
    def _():
        x = x_ref[...]
        hg = jnp.dot(x, wg_s[...], preferred_element_type=jnp.float32) + bg_ref[...]
        hl = jnp.dot(x, wl_s[...], preferred_element_type=jnp.float32) + bl_ref[...]
        hg = jnp.minimum(hg, SWIGLU_LIMIT)
        hl = jnp.clip(hl, -SWIGLU_LIMIT, SWIGLU_LIMIT)
        h = hg * jax.nn.sigmoid(SWIGLU_ALPHA * hg) * (hl + 1.0)
        o_ref[...] = jnp.dot(h.astype(jnp.bfloat16), wd_s[...], preferred_element_type=jnp.float32) + bd_ref[...]

    @pl.when(jnp.logical_not(valid))
    def _():
        o_ref[...] = jnp.zeros_like(o_ref)


def _moe_grouped(tile_expert, n_valid, xg, w_gate, b_gate, w_lin, b_lin, w_down, b_down):
    wspec = pl.BlockSpec((None, D_MODEL, D_EXPERT), lambda i, te, nv: (te[i], 0, 0))
    bspec = pl.BlockSpec((None, 1, D_EXPERT), lambda i, te, nv: (te[i], 0, 0))
    grid_spec = pltpu.PrefetchScalarGridSpec(
        num_scalar_prefetch=2,
        grid=(MOE_TILES,),
        in_specs=[pl.BlockSpec((MOE_TILE, D_MODEL), lambda i, te, nv: (i, 0)),
                  wspec, bspec, wspec, bspec, wspec, bspec],
        out_specs=pl.BlockSpec((MOE_TILE, D_MODEL), lambda i, te, nv: (i, 0)),
        scratch_shapes=[pltpu.VMEM((D_MODEL, D_EXPERT), jnp.bfloat16)] * 3,
    )
    return pl.pallas_call(
        _moe_kernel,
        grid_spec=grid_spec,
        out_shape=jax.ShapeDtypeStruct((MOE_ROWS, D_MODEL), jnp.float32),
        compiler_params=_cparams(("arbitrary",)),
        name="moe_grouped",
    )(tile_expert, n_valid, xg, w_gate, b_gate.reshape(N_EXPERTS, 1, D_EXPERT),
      w_lin, b_lin.reshape(N_EXPERTS, 1, D_EXPERT), w_down, b_down.reshape(N_EXPERTS, 1, D_MODEL))


def _combine_kernel(y_ref, g_ref, x_ref, mod_ref, ln_ref, o_ref):
    g = g_ref[...]
    ff = g[:, 0:1] * y_ref[0]
    for k in range(1, TOP_K):
        ff = ff + g[:, k:k + 1] * y_ref[k]
    o_ref[...] = _layer_norm(DEEPNORM_ALPHA * x_ref[...] + mod_ref[5:6, :] * ff, ln_ref[0:1, :], ln_ref[1:2, :])


def _combine(yk, gates, x, mod, ln):
    return pl.pallas_call(
        _combine_kernel,
        grid=(N_TOK // TOK_TILE,),
        in_specs=[
            pl.BlockSpec((TOP_K, TOK_TILE, D_MODEL), lambda i: (0, i, 0)),
            pl.BlockSpec((TOK_TILE, TOP_K), lambda i: (i, 0)),
            pl.BlockSpec((TOK_TILE, D_MODEL), lambda i: (i, 0)),
            pl.BlockSpec((None, MOD_ROWS, D_MODEL), lambda i: (_cond_row(i), 0, 0)),
            pl.BlockSpec((2, D_MODEL), lambda i: (0, 0)),
        ],
        out_specs=pl.BlockSpec((TOK_TILE, D_MODEL), lambda i: (i, 0)),
        out_shape=jax.ShapeDtypeStruct((N_TOK, D_MODEL), jnp.float32),
        compiler_params=_cparams(("arbitrary",)),
        name="combine_ln",
    )(yk, gates, x, mod, ln)


def _route(logits):
    top_val, top_idx = lax.top_k(logits, TOP_K)
    gates = jax.nn.softmax(top_val, axis=-1)
    onehot = jnp.sum(jax.nn.one_hot(top_idx, N_EXPERTS, dtype=jnp.int32), axis=1)
    incl = jnp.cumsum(onehot, axis=0)
    counts = incl[-1]
    rank = jnp.take_along_axis(incl - onehot, top_idx, axis=1)
    padded = (counts + MOE_TILE - 1) // MOE_TILE * MOE_TILE
    pend = jnp.cumsum(padded)
    pstart = pend - padded
    slot = pstart[top_idx] + rank
    tok = jnp.broadcast_to(jnp.arange(N_TOK, dtype=jnp.int32)[:, None], (N_TOK, TOP_K))
    slot_token = jnp.zeros((MOE_ROWS,), jnp.int32).at[slot.reshape(-1)].set(tok.reshape(-1))
    n_valid = (pend[-1] // MOE_TILE).astype(jnp.int32)
    tile_start = jnp.arange(MOE_TILES, dtype=jnp.int32) * MOE_TILE
    tile_expert = jnp.searchsorted(pend, tile_start, side='right').astype(jnp.int32)
    last_expert = tile_expert[jnp.maximum(n_valid - 1, 0)]
    tile_expert = jnp.where(tile_start < pend[-1], tile_expert, last_expert)
    return gates, slot, slot_token, tile_expert, n_valid.reshape(1)


def _moe_layer(i, x1, u2, logits, mod, ln, w_gate, b_gate, w_lin, b_lin, w_down, b_down):
    gates, slot, slot_token, tile_expert, n_valid = _route(logits)
    xg = jnp.take(u2, slot_token, axis=0)
    y = _moe_grouped(tile_expert, n_valid, xg, w_gate[i], b_gate[i], w_lin[i], b_lin[i], w_down[i], b_down[i])
    yk = jnp.take(y, slot.T, axis=0)
    return _combine(yk, gates, x1, mod, ln)


_F_ROW, _F_FIRST, _F_LAST, _F_PREV8, _F_HASPREV, _F_NEXT8, _F_HASNEXT, _F_H0, _F_USEH0, _F_HOUT, _F_WRITEH = range(11)
_N_FIELDS = 11
_SSD_STEPS = N_TOK // SSM_CHUNK


def _ssd_table(reverse):
    rows = []
    for prompt, nseq, slen, base in ((True, BATCH, SEQ, 0), (False, DEC_BATCH, DEC_SEQ, N_PROMPT)):
        nc = slen // SSM_CHUNK
        for s in range(nseq):
            for c in range(nc):
                row = (base + s * slen) // SSM_CHUNK + c
                first, last = (c == nc - 1, c == 0) if reverse else (c == 0, c == nc - 1)
                per = SSM_CHUNK // SUBLANES
                rows.append([
                    row, int(first), int(last),
                    max(row * per - 1, 0), int(c > 0),
                    min(row * per + per, N_TOK // SUBLANES - 1), int(c < nc - 1),
                    0 if prompt else s, int(not prompt),
                    s if prompt else BATCH - 1, int(prompt and last),
                ])
    if reverse:
        rows = rows[::-1]
    return np.asarray(rows, np.int32).T.reshape(-1)


def _fld(tbl, f, s):
    return tbl[f * _SSD_STEPS + s]


def _ssd_chunk(xs, bmat, cmat, dt_raw, dtb_row, alog_row, s_ref, reverse):
    f32, bf16 = jnp.float32, jnp.bfloat16
    base = SSM_HEADS if reverse else 0
    lane = lax.broadcasted_iota(jnp.int32, (1, DT_PAD), 1)
    head_cols = jnp.logical_and(lane >= base, lane < base + SSM_HEADS)
    xr = dt_raw + dtb_row
    dt = jnp.maximum(xr, 0.0) + jnp.log1p(jnp.exp(-jnp.abs(xr)))
    a_row = jnp.where(head_cols, -jnp.exp(alog_row), 0.0)
    d_a = dt * a_row
    r = lax.broadcasted_iota(jnp.int32, (SSM_CHUNK, SSM_CHUNK), 0)
    c = lax.broadcasted_iota(jnp.int32, (SSM_CHUNK, SSM_CHUNK), 1)
    tri = (r <= c) if reverse else (r >= c)
    cum = jnp.dot(tri.astype(f32), d_a, preferred_element_type=f32, precision=lax.Precision.HIGHEST)
    cum_t = cum.T
    total = cum[0:1, :] if reverse else cum[SSM_CHUNK - 1:SSM_CHUNK, :]
    e_in = jnp.exp(cum)
    w_end = dt * jnp.exp(total - cum)
    dec = jnp.exp(total)
    pieces = []
    for g in range(SSM_GROUPS):
        bg = bmat[:, g * D_STATE:(g + 1) * D_STATE]
        cg = cmat[:, g * D_STATE:(g + 1) * D_STATE]
        cb = lax.dot_general(cg, bg, _NT, preferred_element_type=f32)
        for j in range(SSM_HEADS_PER_GROUP):
            h = g * SSM_HEADS_PER_GROUP + j
            ch = base + h
            lo, hi = h * SSM_HEAD_DIM, (h + 1) * SSM_HEAD_DIM
            seg = cum[:, ch:ch + 1] - cum_t[ch:ch + 1, :]
            decay = jnp.exp(jnp.where(tri, seg, NEG_BIG))
            m = (cb * decay).astype(bf16)
            xh = xs[:, lo:hi]
            xdt = (xh * dt[:, ch:ch + 1]).astype(bf16)
            sh = s_ref[lo:hi, :]
            y_off = lax.dot_general(cg, sh.astype(bf16), _NT, preferred_element_type=f32)
            y = jnp.dot(m, xdt, preferred_element_type=f32) + y_off * e_in[:, ch:ch + 1]
            pieces.append(y)
            xw = (xh * w_end[:, ch:ch + 1]).astype(bf16)
            upd = lax.dot_general(xw, bg, _TN, preferred_element_type=f32)
            s_ref[lo:hi, :] = sh * dec[:, ch:ch + 1] + upd
    return jnp.concatenate(pieces, axis=1)


def _ssd_init_state(tbl, step, h0_ref, s_ref):
    @pl.when(_fld(tbl, _F_FIRST, step) == 1)
    def _():
        use = _fld(tbl, _F_USEH0, step) == 1
        s_ref[...] = jnp.where(use, h0_ref[...], 0.0)


def _ssd_fwd_kernel(tbl, xc_ref, xp_ref, xn_ref, dt_ref, cw_ref, cbias_ref, dtb_ref, alog_ref, h0_ref,
                    xs_ref, bc_ref, yf_ref, hout_ref, s_ref):
    step = pl.program_id(0)
    _ssd_init_state(tbl, step, h0_ref, s_ref)
    hp = (_fld(tbl, _F_HASPREV, step) == 1).astype(jnp.float32)
    hn = (_fld(tbl, _F_HASNEXT, step) == 1).astype(jnp.float32)
    window = jnp.concatenate([xp_ref[...] * hp, xc_ref[...], xn_ref[...] * hn], axis=0)
    acc = cbias_ref[...] + cw_ref[0:1, :] * window[SUBLANES - 2:SUBLANES - 2 + SSM_CHUNK, :]
    for k in range(1, SSM_CONV):
        off = SUBLANES - SSM_CONV // 2 + k
        acc = acc + cw_ref[k:k + 1, :] * window[off:off + SSM_CHUNK, :]
    xbc = acc * jax.nn.sigmoid(acc)
    xs = xbc[:, :SSM_INNER]
    bc = xbc[:, SSM_INNER:].astype(jnp.bfloat16)
    xs_ref[...] = xs
    bc_ref[...] = bc
    yf_ref[...] = _ssd_chunk(xs, bc[:, :SSM_BC], bc[:, SSM_BC:], dt_ref[...], dtb_ref[...], alog_ref[...],
                             s_ref, reverse=False)

    @pl.when(_fld(tbl, _F_WRITEH, step) == 1)
    def _():
        hout_ref[...] = s_ref[...]


def _ssd_bwd_kernel(tbl, xs_ref, bc_ref, dt_ref, yf_ref, z_ref, dtb_ref, alog_ref, dskip_ref, nw_ref, h0_ref,
                    y_ref, hout_ref, s_ref):
    step = pl.program_id(0)
    _ssd_init_state(tbl, step, h0_ref, s_ref)
    xs = xs_ref[...]
    bc = bc_ref[...]
    yb = _ssd_chunk(xs, bc[:, :SSM_BC], bc[:, SSM_BC:], dt_ref[...], dtb_ref[...], alog_ref[...],
                    s_ref, reverse=True)
    z = z_ref[...]
    hg = (yf_ref[...] + yb + dskip_ref[...] * xs) * (z * jax.nn.sigmoid(z))
    gw = SSM_INNER // SSM_GROUPS
    outs = []
    for g in range(SSM_GROUPS):
        hgg = hg[:, g * gw:(g + 1) * gw]
        outs.append(hgg * lax.rsqrt(jnp.mean(hgg * hgg, axis=-1, keepdims=True) + RMS_EPS))
    y_ref[...] = (jnp.concatenate(outs, axis=1) * nw_ref[...]).astype(jnp.bfloat16)

    @pl.when(_fld(tbl, _F_WRITEH, step) == 1)
    def _():
        hout_ref[...] = s_ref[...]


def _ssd_mixer(z, xbc_raw, dt_raw, conv_w, conv_b, a_log, dt_bias, d_skip, norm_w, state_in):
    f32 = jnp.float32
    row = lambda f: (lambda s, tbl: (_fld(tbl, f, s), 0))
    chunk_spec = lambda w: pl.BlockSpec((SSM_CHUNK, w), row(_F_ROW))
    const_spec = lambda shape: pl.BlockSpec(shape, lambda s, tbl: (0,) * len(shape))
    h0_spec = lambda d: pl.BlockSpec((None, None, SSM_INNER, D_STATE), lambda s, tbl: (_fld(tbl, _F_H0, s), d, 0, 0))
    hout_spec = pl.BlockSpec((None, SSM_INNER, D_STATE), lambda s, tbl: (_fld(tbl, _F_HOUT, s), 0, 0))
    hout_shape = jax.ShapeDtypeStruct((BATCH, SSM_INNER, D_STATE), f32)
    state_scratch = [pltpu.VMEM((SSM_INNER, D_STATE), f32)]

    cw = jnp.pad(conv_w, ((0, SUBLANES - SSM_CONV), (0, 0)))
    cbias = conv_b.reshape(1, SSM_CONV_DIM)
    dtb = jnp.pad(dt_bias.reshape(1, 2 * SSM_HEADS), ((0, 0), (0, DT_PAD - 2 * SSM_HEADS)))
    alog = jnp.pad(a_log.reshape(1, 2 * SSM_HEADS), ((0, 0), (0, DT_PAD - 2 * SSM_HEADS)))
    dskip = jnp.repeat(d_skip, SSM_HEAD_DIM).reshape(1, SSM_INNER)
    nw = norm_w.reshape(1, SSM_INNER)

    xs, bc, yf, h_f = pl.pallas_call(
        _ssd_fwd_kernel,
        grid_spec=pltpu.PrefetchScalarGridSpec(
            num_scalar_prefetch=1,
            grid=(_SSD_STEPS,),
            in_specs=[
                chunk_spec(SSM_CONV_DIM),
                pl.BlockSpec((SUBLANES, SSM_CONV_DIM), row(_F_PREV8)),
                pl.BlockSpec((SUBLANES, SSM_CONV_DIM), row(_F_NEXT8)),
                chunk_spec(DT_PAD),
                const_spec((SUBLANES, SSM_CONV_DIM)), const_spec((1, SSM_CONV_DIM)),
                const_spec((1, DT_PAD)), const_spec((1, DT_PAD)),
                h0_spec(0),
            ],
            out_specs=[chunk_spec(SSM_INNER), chunk_spec(2 * SSM_BC), chunk_spec(SSM_INNER), hout_spec],
            scratch_shapes=state_scratch,
        ),
        out_shape=[
            jax.ShapeDtypeStruct((N_TOK, SSM_INNER), f32),
            jax.ShapeDtypeStruct((N_TOK, 2 * SSM_BC), jnp.bfloat16),
            jax.ShapeDtypeStruct((N_TOK, SSM_INNER), f32),
            hout_shape,
        ],
        compiler_params=_cparams(("arbitrary",)),
        name="ssd_forward",
    )(jnp.asarray(_ssd_table(False)), xbc_raw, xbc_raw, xbc_raw, dt_raw, cw, cbias, dtb, alog, state_in)

    y, h_b = pl.pallas_call(
        _ssd_bwd_kernel,
        grid_spec=pltpu.PrefetchScalarGridSpec(
            num_scalar_prefetch=1,
            grid=(_SSD_STEPS,),
            in_specs=[
                chunk_spec(SSM_INNER), chunk_spec(2 * SSM_BC), chunk_spec(DT_PAD), chunk_spec(SSM_INNER),
                chunk_spec(SSM_INNER),
                const_spec((1, DT_PAD)), const_spec((1, DT_PAD)),
                const_spec((1, SSM_INNER)), const_spec((1, SSM_INNER)),
                h0_spec(1),
            ],
            out_specs=[chunk_spec(SSM_INNER), hout_spec],
            scratch_shapes=state_scratch,
        ),
        out_shape=[jax.ShapeDtypeStruct((N_TOK, SSM_INNER), jnp.bfloat16), hout_shape],
        compiler_params=_cparams(("arbitrary",)),
        name="ssd_backward",
    )(jnp.asarray(_ssd_table(True)), xs, bc, dt_raw, yf, z, dtb, alog, dskip, nw, state_in)
    return y, h_f, h_b


def _sink_attention(q_heads, keys, vals, sink_ref, kv, masks):
    outs = []
    for g, qh in enumerate(q_heads):
        sk = sink_ref[kv * ATT_GROUP + g]
        scores = []
        for kk, mask in zip(keys, masks):
            s = lax.dot_general(qh, kk, _NT, preferred_element_type=jnp.float32) * ATT_SCALE
            scores.append(s if mask is None else jnp.where(mask, s, NEG_BIG))
        m = sk
        for s in scores:
            m = jnp.maximum(m, jnp.max(s, axis=-1, keepdims=True))
        den = jnp.exp(sk - m)
        acc = None
        for s, vv in zip(scores, vals):
            p = jnp.exp(s - m)
            den = den + jnp.sum(p, axis=-1, keepdims=True)
            pv = jnp.dot(p.astype(jnp.bfloat16), vv, preferred_element_type=jnp.float32)
            acc = pv if acc is None else acc + pv
        outs.append(acc / den)
    return outs


def _head(x, h):
    return x[:, h * ATT_HEAD_DIM:(h + 1) * ATT_HEAD_DIM]


def _attn_ctx_kernel(sink_ref, q_ref, k_ref, v_ref, o_ref):
    bf16 = jnp.bfloat16
    q = q_ref[...].astype(bf16)
    k = k_ref[...].astype(bf16)
    v = v_ref[...].astype(bf16)
    outs = []
    for kv in range(ATT_KV_HEADS):
        qs = [_head(q, kv * ATT_GROUP + g) for g in range(ATT_GROUP)]
        outs += _sink_attention(qs, [_head(k, kv)], [_head(v, kv)], sink_ref, kv, [None])
    o_ref[...] = jnp.concatenate(outs, axis=1).astype(bf16)


def _attend_context(q, k, v, sink):
    return pl.pallas_call(
        _attn_ctx_kernel,
        grid=(BATCH,),
        in_specs=[
            pl.BlockSpec(memory_space=pltpu.SMEM),
            pl.BlockSpec((SEQ, D_MODEL), lambda b: (b, 0)),
            pl.BlockSpec((SEQ, KV_DIM), lambda b: (b, 0)),
            pl.BlockSpec((SEQ, KV_DIM), lambda b: (b, 0)),
        ],
        out_specs=pl.BlockSpec((SEQ, D_MODEL), lambda b: (b, 0)),
        out_shape=jax.ShapeDtypeStruct((N_PROMPT, D_MODEL), jnp.bfloat16),
        compiler_params=_cparams(("arbitrary",)),
        name="attend_context",
    )(sink, q, k, v)


def _rope_tables(width):
    t = np.arange(DEC_SEQ)
    d = np.arange(width) % ATT_HEAD_DIM
    pos = np.where(d[None, :] < ROPE_HALF, (t // GRID_W)[:, None], (t % GRID_W)[:, None]).astype(np.float32)
    inv = (ROPE_BASE ** (-np.arange(ROPE_QUARTER, dtype=np.float32) / ROPE_QUARTER)).astype(np.float32)
    ang = pos * inv[d % ROPE_QUARTER][None, :]
    sign = np.where((d % ROPE_HALF) < ROPE_QUARTER, -1.0, 1.0).astype(np.float32)
    return jnp.asarray(np.cos(ang), jnp.float32), jnp.asarray(np.sin(ang) * sign[None, :], jnp.float32)


def _rope(x, cos, sin_signed):
    width = x.shape[1]
    lane = lax.broadcasted_iota(jnp.int32, (1, width), 1)
    first = (lane % ROPE_HALF) < ROPE_QUARTER
    partner = jnp.where(first, pltpu.roll(x, width - ROPE_QUARTER, 1), pltpu.roll(x, ROPE_QUARTER, 1))
    return x * cos + partner * sin_signed


def _rope_kernel(q_ref, k_ref, v_ref, cos_ref, sin_ref, qo_ref, ko_ref, vo_ref):
    bf16 = jnp.bfloat16
    cos, sin = cos_ref[...], sin_ref[...]
    qo_ref[...] = _rope(q_ref[...], cos, sin).astype(bf16)
    ko_ref[...] = _rope(k_ref[...], cos[:, :KV_DIM], sin[:, :KV_DIM]).astype(bf16)
    vo_ref[...] = v_ref[...].astype(bf16)


def _rope_latent(q, k, v):
    nb = DEC_SEQ // ATT_BLOCK
    off = N_PROMPT // ATT_BLOCK
    cos, sin = _rope_tables(D_MODEL)
    tok = lambda b, i: (off + b * nb + i, 0)
    out = lambda b, i: (b * nb + i, 0)
    return pl.pallas_call(
        _rope_kernel,
        grid=(DEC_BATCH, nb),
        in_specs=[
            pl.BlockSpec((ATT_BLOCK, D_MODEL), tok),
            pl.BlockSpec((ATT_BLOCK, KV_DIM), tok),
            pl.BlockSpec((ATT_BLOCK, KV_DIM), tok),
            pl.BlockSpec((ATT_BLOCK, D_MODEL), lambda b, i: (i, 0)),
            pl.BlockSpec((ATT_BLOCK, D_MODEL), lambda b, i: (i, 0)),
        ],
        out_specs=[
            pl.BlockSpec((ATT_BLOCK, D_MODEL), out),
            pl.BlockSpec((ATT_BLOCK, KV_DIM), out),
            pl.BlockSpec((ATT_BLOCK, KV_DIM), out),
        ],
        out_shape=[
            jax.ShapeDtypeStruct((N_SAMPLE, D_MODEL), jnp.bfloat16),
            jax.ShapeDtypeStruct((N_SAMPLE, KV_DIM), jnp.bfloat16),
            jax.ShapeDtypeStruct((N_SAMPLE, KV_DIM), jnp.bfloat16),
        ],
        compiler_params=_cparams(("arbitrary", "arbitrary")),
        name="rope_latent",
    )(q, k, v, cos, sin)


def _attn_lat_kernel(sink_ref, q_ref, kp_ref, kc_ref, kn_ref, vp_ref, vc_ref, vn_ref, kx_ref, vx_ref, o_ref):
    i = pl.program_id(1)
    nb = pl.num_programs(1)
    q = q_ref[...]
    k_loc = jnp.concatenate([kp_ref[...], kc_ref[...], kn_ref[...]], axis=0)
    v_loc = jnp.concatenate([vp_ref[...], vc_ref[...], vn_ref[...]], axis=0)
    kx, vx = kx_ref[...], vx_ref[...]
    r = lax.broadcasted_iota(jnp.int32, (ATT_BLOCK, 3 * ATT_BLOCK), 0)
    c = lax.broadcasted_iota(jnp.int32, (ATT_BLOCK, 3 * ATT_BLOCK), 1)
    rel = c - ATT_BLOCK - r
    in_window = jnp.logical_and(rel >= -WINDOW, rel <= WINDOW)
    in_seq = jnp.logical_and(jnp.logical_or(c >= ATT_BLOCK, i > 0),
                             jnp.logical_or(c < 2 * ATT_BLOCK, i < nb - 1))
    valid = jnp.logical_and(in_window, in_seq)
    outs = []
    for kv in range(ATT_KV_HEADS):
        qs = [_head(q, kv * ATT_GROUP + g) for g in range(ATT_GROUP)]
        outs += _sink_attention(qs, [_head(k_loc, kv), _head(kx, kv)], [_head(v_loc, kv), _head(vx, kv)],
                                sink_ref, kv, [valid, None])
    o_ref[...] = jnp.concatenate(outs, axis=1).astype(jnp.bfloat16)


def _attend_latent(qr, kr, vb, k_ctx, v_ctx, sink):
    nb = DEC_SEQ // ATT_BLOCK
    cur = lambda b, i: (b * nb + i, 0)
    prv = lambda b, i: (b * nb + jnp.maximum(i - 1, 0), 0)
    nxt = lambda b, i: (b * nb + jnp.minimum(i + 1, nb - 1), 0)
    kvs = lambda f: pl.BlockSpec((ATT_BLOCK, KV_DIM), f)
    ctx = pl.BlockSpec((None, PAST_LEN, KV_DIM), lambda b, i: (b, 0, 0))
    return pl.pallas_call(
        _attn_lat_kernel,
        grid=(DEC_BATCH, nb),
        in_specs=[
            pl.BlockSpec(memory_space=pltpu.SMEM),
            pl.BlockSpec((ATT_BLOCK, D_MODEL), cur),
            kvs(prv), kvs(cur), kvs(nxt), kvs(prv), kvs(cur), kvs(nxt), ctx, ctx,
        ],
        out_specs=pl.BlockSpec((ATT_BLOCK, D_MODEL), cur),
        out_shape=jax.ShapeDtypeStruct((N_SAMPLE, D_MODEL), jnp.bfloat16),
        compiler_params=_cparams(("arbitrary", "arbitrary")),
        name="attend_latent",
    )(sink, qr, kr, kr, kr, vb, vb, vb, k_ctx, v_ctx)


def _gconv_kernel(bg_ref, g_ref, gp_ref, gn_ref, w_ref, o_ref):
    i = pl.program_id(0)
    first = N_PROMPT // CONV_TILE
    per = DEC_SEQ // CONV_TILE
    t = (i - first) % per
    latent = i >= first
    hp = jnp.logical_and(latent, t > 0).astype(jnp.float32)
    hn = jnp.logical_and(latent, t < per - 1).astype(jnp.float32)
    window = jnp.concatenate([gp_ref[...] * hp, g_ref[...], gn_ref[...] * hn], axis=0)
    acc = None
    for k in range(SHORT_CONV):
        off = SUBLANES - SHORT_CONV // 2 + k
        term = w_ref[k:k + 1, :] * window[off:off + CONV_TILE, :]
        acc = term if acc is None else acc + term
    o_ref[...] = (bg_ref[...] * acc).astype(jnp.bfloat16)


def _gated_conv(bg, g, conv_w):
    per = CONV_TILE // SUBLANES
    last = N_TOK // SUBLANES - 1
    cw = jnp.pad(conv_w, ((0, SUBLANES - SHORT_CONV), (0, 0)))
    return pl.pallas_call(
        _gconv_kernel,
        grid=(N_TOK // CONV_TILE,),
        in_specs=[
            pl.BlockSpec((CONV_TILE, D_MODEL), lambda i: (i, 0)),
            pl.BlockSpec((CONV_TILE, D_MODEL), lambda i: (i, 0)),
            pl.BlockSpec((SUBLANES, D_MODEL), lambda i: (jnp.maximum(i * per - 1, 0), 0)),
            pl.BlockSpec((SUBLANES, D_MODEL), lambda i: (jnp.minimum(i * per + per, last), 0)),
            pl.BlockSpec((SUBLANES, D_MODEL), lambda i: (0, 0)),
        ],
        out_specs=pl.BlockSpec((CONV_TILE, D_MODEL), lambda i: (i, 0)),
        out_shape=jax.ShapeDtypeStruct((N_TOK, D_MODEL), jnp.bfloat16),
        compiler_params=_cparams(("arbitrary",)),
        name="gated_conv",
    )(bg, g, g, g, cw)


def kernel(x_prompt, x_sample, cache_k, cache_v, state_ssm, c, c_ctx,
           w_mod, b_mod, ln_w, ln_b,
           w_in_a, conv_w_a, conv_b_a, a_log, dt_bias, d_skip, ssm_norm_w, attn_sink, w_out_a,
           w_in_c, conv_w_c, w_out_c,
           w_router, b_router, w_gate, b_gate, w_lin, b_lin, w_down, b_down):
    bf16 = jnp.bfloat16
    x = jnp.concatenate([x_prompt.reshape(N_PROMPT, D_MODEL), x_sample.reshape(N_SAMPLE, D_MODEL)], axis=0)

    cond = jnp.concatenate([c_ctx[None, :], c, jnp.zeros((COND_PAD - N_COND, D_MODEL), jnp.float32)], axis=0)
    mod_all = _modulation(cond, w_mod, b_mod).reshape(DEPTH, COND_PAD, 6, D_MODEL)
    mod_all = jnp.pad(mod_all, ((0, 0), (0, 0), (0, MOD_ROWS - 6), (0, 0)))

    mod = mod_all[0]
    wa = w_in_a[0].astype(bf16)
    e = np.cumsum((0, SSM_INNER, SSM_CONV_DIM, 2 * SSM_HEADS, D_MODEL, KV_DIM, KV_DIM))
    w_dt = jnp.pad(wa[:, e[2]:e[3]], ((0, 0), (0, DT_PAD - 2 * SSM_HEADS)))
    w_parts = [wa[:, e[0]:e[1]], wa[:, e[1]:e[2]], w_dt, wa[:, e[3]:e[4]], wa[:, e[4]:e[5]], wa[:, e[5]:e[6]]]
    z, xbc_raw, dt_raw, q, k, v = _inproj(x, mod, w_parts)

    state_in = state_ssm[:, 0].reshape(DEC_BATCH, 2, SSM_INNER, D_STATE)
    y_ssm, h_f, h_b = _ssd_mixer(z, xbc_raw, dt_raw, conv_w_a[0], conv_b_a[0], a_log[0], dt_bias[0],
                                 d_skip[0], ssm_norm_w[0], state_in)

    sink = attn_sink[0]
    att_p = _attend_context(q, k, v, sink)
    qr, kr, vb = _rope_latent(q, k, v)
    k_ctx = cache_k[:, 0].reshape(DEC_BATCH, PAST_LEN, KV_DIM).astype(bf16)
    v_ctx = cache_v[:, 0].reshape(DEC_BATCH, PAST_LEN, KV_DIM).astype(bf16)
    att_s = _attend_latent(qr, kr, vb, k_ctx, v_ctx, sink)
    y_att = jnp.concatenate([att_p, att_s], axis=0)

    wo = w_out_a[0].astype(bf16)
    ln = jnp.stack([ln_w[0, 0], ln_b[0, 0]])
    x1, u2, logits = _outproj([y_ssm, y_att], [wo[:SSM_INNER], wo[SSM_INNER:]], x, mod, ln,
                              w_router[0], b_router[0])
    x = _moe_layer(0, x1, u2, logits, mod, jnp.stack([ln_w[0, 1], ln_b[0, 1]]),
                   w_gate, b_gate, w_lin, b_lin, w_down, b_down)

    mod = mod_all[1]
    wc = w_in_c[0].astype(bf16)
    bg, g = _inproj(x, mod, [wc[:, :D_MODEL], wc[:, D_MODEL:2 * D_MODEL], wc[:, 2 * D_MODEL:]], gate_product=True)
    y_c = _gated_conv(bg, g, conv_w_c[0])
    ln = jnp.stack([ln_w[1, 0], ln_b[1, 0]])
    x1, u2, logits = _outproj([y_c], [w_out_c[0].astype(bf16)], x, mod, ln, w_router[1], b_router[1])
    x = _moe_layer(1, x1, u2, logits, mod, jnp.stack([ln_w[1, 1], ln_b[1, 1]]),
                   w_gate, b_gate, w_lin, b_lin, w_down, b_down)

    y_prompt = x[:N_PROMPT].reshape(BATCH, SEQ, D_MODEL)
    y_sample = x[N_PROMPT:].reshape(DEC_BATCH, DEC_SEQ, D_MODEL)
    new_k = k[:N_PROMPT].reshape(BATCH, 1, SEQ, ATT_KV_HEADS, ATT_HEAD_DIM)
    new_v = v[:N_PROMPT].reshape(BATCH, 1, SEQ, ATT_KV_HEADS, ATT_HEAD_DIM)
    new_state = jnp.stack([h_f, h_b], axis=1).reshape(BATCH, 1, 2, SSM_HEADS, SSM_HEAD_DIM, D_STATE)
    return (y_prompt, y_sample, new_k, new_v, new_state)
```

```python
import functools

import numpy as np
import jax
import jax.numpy as jnp
from jax import lax
from jax.experimental import pallas as pl
from jax.experimental.pallas import tpu as pltpu

D_MODEL = 1024
BATCH = 32
SEQ = 256
DEPTH = 2
DEC_BATCH = 4
DEC_SEQ = 1024
PAST_LEN = 512
GRID_W = 64
SSM_HEAD_DIM = 64
SSM_INNER = D_MODEL
SSM_HEADS = SSM_INNER // SSM_HEAD_DIM
SSM_GROUPS = 2
SSM_HEADS_PER_GROUP = SSM_HEADS // SSM_GROUPS
D_STATE = 128
SSM_CONV = 5
SSM_CHUNK = 128
SSM_BC = SSM_GROUPS * D_STATE
SSM_CONV_DIM = SSM_INNER + 2 * SSM_BC
ATT_HEAD_DIM = 64
ATT_HEADS = D_MODEL // ATT_HEAD_DIM
ATT_KV_HEADS = 4
ATT_GROUP = ATT_HEADS // ATT_KV_HEADS
WINDOW = 128
ATT_BLOCK = 128
ATT_SCALE = ATT_HEAD_DIM ** -0.5
ROPE_BASE = 10000.0
ROPE_HALF = ATT_HEAD_DIM // 2
ROPE_QUARTER = ATT_HEAD_DIM // 4
SHORT_CONV = 3
N_EXPERTS = 32
TOP_K = 4
D_EXPERT = D_MODEL
SWIGLU_ALPHA = 1.702
SWIGLU_LIMIT = 7.0
N_EVEN = (DEPTH + 1) // 2
DEEPNORM_ALPHA = (2 * DEPTH) ** 0.25
LN_EPS = 1e-5
RMS_EPS = 1e-5
KV_DIM = ATT_KV_HEADS * ATT_HEAD_DIM

N_PROMPT = BATCH * SEQ
N_SAMPLE = DEC_BATCH * DEC_SEQ
N_TOK = N_PROMPT + N_SAMPLE
N_COND = 1 + DEC_BATCH
SUBLANES = 8
LANES = 128
COND_PAD = SUBLANES
MOD_ROWS = SUBLANES
DT_PAD = LANES

TOK_TILE = 512
CONV_TILE = 256
MOE_TILE = 256
MOE_ROWS = N_TOK * TOP_K + N_EXPERTS * MOE_TILE
MOE_TILES = MOE_ROWS // MOE_TILE
VMEM_LIMIT = 56 * 1024 * 1024
NEG_BIG = -1e30

assert N_PROMPT % TOK_TILE == 0 and DEC_SEQ % TOK_TILE == 0
assert SEQ % CONV_TILE == 0 and DEC_SEQ % CONV_TILE == 0

_NT = (((1,), (1,)), ((), ()))
_TN = (((0,), (0,)), ((), ()))


def _cparams(sem):
    return pltpu.CompilerParams(dimension_semantics=sem, vmem_limit_bytes=VMEM_LIMIT)


def _cond_row(i):
    first = N_PROMPT // TOK_TILE
    per = DEC_SEQ // TOK_TILE
    return jnp.where(i < first, 0, 1 + (i - first) // per)


def _bf16_dot(a, b):
    return jnp.dot(a.astype(jnp.bfloat16), b.astype(jnp.bfloat16), preferred_element_type=jnp.float32)


def _mod_kernel(c_ref, w_ref, b_ref, o_ref):
    c = c_ref[...]
    s = c * jax.nn.sigmoid(c)
    o_ref[...] = _bf16_dot(s, w_ref[...]) + b_ref[...]


def _modulation(cond, w_mod, b_mod):
    tn = 1536
    return pl.pallas_call(
        _mod_kernel,
        grid=(DEPTH, 6 * D_MODEL // tn),
        in_specs=[
            pl.BlockSpec((COND_PAD, D_MODEL), lambda l, n: (0, 0)),
            pl.BlockSpec((None, D_MODEL, tn), lambda l, n: (l, 0, n)),
            pl.BlockSpec((None, 1, tn), lambda l, n: (l, 0, n)),
        ],
        out_specs=pl.BlockSpec((None, COND_PAD, tn), lambda l, n: (l, 0, n)),
        out_shape=jax.ShapeDtypeStruct((DEPTH, COND_PAD, 6 * D_MODEL), jnp.float32),
        compiler_params=_cparams(("arbitrary", "arbitrary")),
        name="modulation",
    )(cond, w_mod, b_mod.reshape(DEPTH, 1, 6 * D_MODEL))


def _inproj_kernel(x_ref, mod_ref, *refs, gate_product):
    n = len(refs) // 2 + (1 if gate_product else 0)
    w_refs, o_refs = refs[:n], refs[n:]
    u = x_ref[...] * (1.0 + mod_ref[1:2, :]) + mod_ref[0:1, :]
    ub = u.astype(jnp.bfloat16)
    outs = [jnp.dot(ub, w_ref[...], preferred_element_type=jnp.float32) for w_ref in w_refs]
    if gate_product:
        outs = outs[:-2] + [outs[-2] * outs[-1]]
    for o_ref, o in zip(o_refs, outs):
        o_ref[...] = o


def _inproj(x, mod, weights, gate_product=False):
    out_w = [w.shape[1] for w in weights]
    if gate_product:
        out_w = out_w[:-1]
    return pl.pallas_call(
        functools.partial(_inproj_kernel, gate_product=gate_product),
        grid=(N_TOK // TOK_TILE,),
        in_specs=[
            pl.BlockSpec((TOK_TILE, D_MODEL), lambda i: (i, 0)),
            pl.BlockSpec((None, MOD_ROWS, D_MODEL), lambda i: (_cond_row(i), 0, 0)),
        ] + [pl.BlockSpec(w.shape, lambda i: (0, 0)) for w in weights],
        out_specs=[pl.BlockSpec((TOK_TILE, n), lambda i: (i, 0)) for n in out_w],
        out_shape=[jax.ShapeDtypeStruct((N_TOK, n), jnp.float32) for n in out_w],
        compiler_params=_cparams(("arbitrary",)),
        name="inproj",
    )(x, mod, *weights)


def _layer_norm(v, w, b):
    mu = jnp.mean(v, axis=-1, keepdims=True)
    d = v - mu
    var = jnp.mean(d * d, axis=-1, keepdims=True)
    return d * lax.rsqrt(var + LN_EPS) * w + b


def _outproj_kernel(*refs, n_in):
    y_refs = refs[:n_in]
    w_refs = refs[n_in:2 * n_in]
    (x_ref, mod_ref, ln_ref, wr_ref, br_ref,
     x1_ref, u2_ref, idx_ref, gate_ref, rank_ref, cnt_ref, run_ref) = refs[2 * n_in:]
    f32 = jnp.float32

    @pl.when(pl.program_id(0) == 0)
    def _():
        run_ref[...] = jnp.zeros_like(run_ref)

    mix = _bf16_dot(y_refs[0][...], w_refs[0][...])
    for y_ref, w_ref in zip(y_refs[1:], w_refs[1:]):
        mix = mix + _bf16_dot(y_ref[...], w_ref[...])
    x1 = _layer_norm(DEEPNORM_ALPHA * x_ref[...] + mod_ref[2:3, :] * mix, ln_ref[0:1, :], ln_ref[1:2, :])
    x1_ref[...] = x1
    u2 = x1 * (1.0 + mod_ref[4:5, :]) + mod_ref[3:4, :]
    u2_ref[...] = u2
    logits = jnp.dot(u2, wr_ref[...], preferred_element_type=f32, precision=lax.Precision.HIGHEST) + br_ref[...]

    lane = lax.broadcasted_iota(jnp.int32, (TOK_TILE, N_EXPERTS), 1)
    work = logits
    vals, sels, idxs = [], [], []
    for _ in range(TOP_K):
        m = jnp.max(work, axis=-1, keepdims=True)
        idx = jnp.min(jnp.where(work == m, lane, N_EXPERTS), axis=-1, keepdims=True)
        sel = lane == idx
        vals.append(m)
        idxs.append(idx)
        sels.append(sel)
        work = jnp.where(sel, -jnp.inf, work)
    exps = [jnp.exp(v - vals[0]) for v in vals]
    den = exps[0]
    for e in exps[1:]:
        den = den + e
    onehot = sels[0].astype(f32)
    for sel in sels[1:]:
        onehot = onehot + sel.astype(f32)
    r = lax.broadcasted_iota(jnp.int32, (TOK_TILE, TOK_TILE), 0)
    c = lax.broadcasted_iota(jnp.int32, (TOK_TILE, TOK_TILE), 1)
    before = jnp.dot((r > c).astype(jnp.bfloat16), onehot.astype(jnp.bfloat16), preferred_element_type=f32)
    rank_all = before + run_ref[...]
    run = run_ref[...] + jnp.sum(onehot, axis=0, keepdims=True)
    run_ref[...] = run
    cnt_ref[...] = run.astype(jnp.int32)
    col = lax.broadcasted_iota(jnp.int32, (TOK_TILE, TOP_K), 1)
    idx_o = jnp.zeros((TOK_TILE, TOP_K), jnp.int32)
    gate_o = jnp.zeros((TOK_TILE, TOP_K), f32)
    rank_o = jnp.zeros((TOK_TILE, TOP_K), f32)
    for k in range(TOP_K):
        rank_k = jnp.sum(jnp.where(sels[k], rank_all, 0.0), axis=-1, keepdims=True)
        idx_o = jnp.where(col == k, idxs[k], idx_o)
        gate_o = jnp.where(col == k, exps[k] / den, gate_o)
        rank_o = jnp.where(col == k, rank_k, rank_o)
    idx_ref[...] = idx_o
    gate_ref[...] = gate_o
    rank_ref[...] = rank_o.astype(jnp.int32)


def _outproj(ys, ws, x, mod, ln, w_router, b_router):
    n_in = len(ys)
    return pl.pallas_call(
        functools.partial(_outproj_kernel, n_in=n_in),
        grid=(N_TOK // TOK_TILE,),
        in_specs=[pl.BlockSpec((TOK_TILE, y.shape[1]), lambda i: (i, 0)) for y in ys]
        + [pl.BlockSpec(w.shape, lambda i: (0, 0)) for w in ws]
        + [
            pl.BlockSpec((TOK_TILE, D_MODEL), lambda i: (i, 0)),
            pl.BlockSpec((None, MOD_ROWS, D_MODEL), lambda i: (_cond_row(i), 0, 0)),
            pl.BlockSpec((2, D_MODEL), lambda i: (0, 0)),
            pl.BlockSpec((D_MODEL, N_EXPERTS), lambda i: (0, 0)),
            pl.BlockSpec((1, N_EXPERTS), lambda i: (0, 0)),
        ],
        out_specs=[
            pl.BlockSpec((TOK_TILE, D_MODEL), lambda i: (i, 0)),
            pl.BlockSpec((TOK_TILE, D_MODEL), lambda i: (i, 0)),
            pl.BlockSpec((TOK_TILE, TOP_K), lambda i: (i, 0)),
            pl.BlockSpec((TOK_TILE, TOP_K), lambda i: (i, 0)),
            pl.BlockSpec((TOK_TILE, TOP_K), lambda i: (i, 0)),
            pl.BlockSpec((1, N_EXPERTS), lambda i: (0, 0)),
        ],
        out_shape=[
            jax.ShapeDtypeStruct((N_TOK, D_MODEL), jnp.float32),
            jax.ShapeDtypeStruct((N_TOK, D_MODEL), jnp.float32),
            jax.ShapeDtypeStruct((N_TOK, TOP_K), jnp.int32),
            jax.ShapeDtypeStruct((N_TOK, TOP_K), jnp.float32),
            jax.ShapeDtypeStruct((N_TOK, TOP_K), jnp.int32),
            jax.ShapeDtypeStruct((1, N_EXPERTS), jnp.int32),
        ],
        scratch_shapes=[pltpu.VMEM((1, N_EXPERTS), jnp.float32)],
        compiler_params=_cparams(("arbitrary",)),
        name="outproj_ln_router",
    )(*ys, *ws, x, mod, ln, w_router, b_router.reshape(1, N_EXPERTS))


def _moe_kernel(te_ref, nv_ref, x_ref, wg_ref, bg_ref, wl_ref, bl_ref, wd_ref, bd_ref, o_ref,
                wg_s, wl_s, wd_s):
    i = pl.program_id(0)
    valid = i < nv_ref[0]
    prev = te_ref[jnp.maximum(i - 1, 0)]
    new_expert = jnp.logical_or(i == 0, te_ref[i] != prev)

    @pl.when(jnp.logical_and(valid, new_expert))
    def _():
        wg_s[...] = wg_ref[...].astype(jnp.bfloat16)
        wl_s[...] = wl_ref[...].astype(jnp.bfloat16)
        wd_s[...] = wd_ref[...].astype(jnp.bfloat16)

    @pl.when(valid)
    def _():
        x = x_ref[...].astype(jnp.bfloat16)
        hg = jnp.dot(x, wg_s[...], preferred_element_type=jnp.float32) + bg_ref[...]
        hl = jnp.dot(x, wl_s[...], preferred_element_type=jnp.float32) + bl_ref[...]
        hg = jnp.minimum(hg, SWIGLU_LIMIT)
        hl = jnp.clip(hl, -SWIGLU_LIMIT, SWIGLU_LIMIT)
        h = hg * jax.nn.sigmoid(SWIGLU_ALPHA * hg) * (hl + 1.0)
        o_ref[...] = jnp.dot(h.astype(jnp.bfloat16), wd_s[...], preferred_element_type=jnp.float32) + bd_ref[...]

    @pl.when(jnp.logical_not(valid))
    def _():
        o_ref[...] = jnp.zeros_like(o_ref)


def _moe_grouped(layer, tile_expert, n_valid, xg, w_gate, b_gate, w_lin, b_lin, w_down, b_down):
    wspec = pl.BlockSpec((None, None, D_MODEL, D_EXPERT), lambda i, te, nv: (layer, te[i], 0, 0))
    bspec = pl.BlockSpec((None, None, 1, D_EXPERT), lambda i, te, nv: (layer, te[i], 0, 0))
    grid_spec = pltpu.PrefetchScalarGridSpec(
        num_scalar_prefetch=2,
        grid=(MOE_TILES,),
        in_specs=[pl.BlockSpec((MOE_TILE, D_MODEL), lambda i, te, nv: (jnp.minimum(i, nv[0] - 1), 0)),
                  wspec, bspec, wspec, bspec, wspec, bspec],
        out_specs=pl.BlockSpec((MOE_TILE, D_MODEL), lambda i, te, nv: (i, 0)),
        scratch_shapes=[pltpu.VMEM((D_MODEL, D_EXPERT), jnp.bfloat16)] * 3,
    )
    bshape = (DEPTH, N_EXPERTS, 1, D_EXPERT)
    return pl.pallas_call(
        _moe_kernel,
        grid_spec=grid_spec,
        out_shape=jax.ShapeDtypeStruct((MOE_ROWS, D_MODEL), jnp.float32),
        compiler_params=_cparams(("arbitrary",)),
        name="moe_grouped",
    )(tile_expert, n_valid, xg, w_gate, b_gate.reshape(bshape), w_lin, b_lin.reshape(bshape),
      w_down, b_down.reshape(bshape))


def _combine_kernel(y_ref, g_ref, x_ref, mod_ref, ln_ref, o_ref):
    g = g_ref[...]
    ff = g[:, 0:1] * y_ref[0]
    for k in range(1, TOP_K):
        ff = ff + g[:, k:k + 1] * y_ref[k]
    o_ref[...] = _layer_norm(DEEPNORM_ALPHA * x_ref[...] + mod_ref[5:6, :] * ff, ln_ref[0:1, :], ln_ref[1:2, :])


def _combine(yk, gates, x, mod, ln):
    return pl.pallas_call(
        _combine_kernel,
        grid=(N_TOK // TOK_TILE,),
        in_specs=[
            pl.BlockSpec((TOP_K, TOK_TILE, D_MODEL), lambda i: (0, i, 0)),
            pl.BlockSpec((TOK_TILE, TOP_K), lambda i: (i, 0)),
            pl.BlockSpec((TOK_TILE, D_MODEL), lambda i: (i, 0)),
            pl.BlockSpec((None, MOD_ROWS, D_MODEL), lambda i: (_cond_row(i), 0, 0)),
            pl.BlockSpec((2, D_MODEL), lambda i: (0, 0)),
        ],
        out_specs=pl.BlockSpec((TOK_TILE, D_MODEL), lambda i: (i, 0)),
        out_shape=jax.ShapeDtypeStruct((N_TOK, D_MODEL), jnp.float32),
        compiler_params=_cparams(("arbitrary",)),
        name="combine_ln",
    )(yk, gates, x, mod, ln)


def _route(top_idx, rank, counts):
    counts = counts.reshape(N_EXPERTS)
    padded = (counts + MOE_TILE - 1) // MOE_TILE * MOE_TILE
    pend = jnp.cumsum(padded)
    pstart = pend - padded
    slot = jnp.take(pstart, top_idx) + rank
    tok = jnp.broadcast_to(jnp.arange(N_TOK, dtype=jnp.int32)[:, None], (N_TOK, TOP_K))
    slot_token = jnp.zeros((MOE_ROWS,), jnp.int32).at[slot.reshape(-1)].set(tok.reshape(-1))
    n_valid = (pend[-1] // MOE_TILE).astype(jnp.int32)
    tile_start = jnp.arange(MOE_TILES, dtype=jnp.int32) * MOE_TILE
    tile_expert = jnp.searchsorted(pend, tile_start, side='right').astype(jnp.int32)
    last_expert = tile_expert[jnp.maximum(n_valid - 1, 0)]
    tile_expert = jnp.where(tile_start < pend[-1], tile_expert, last_expert)
    return slot, slot_token, tile_expert, n_valid.reshape(1)


def _moe_layer(i, x1, u2, routing, mod, ln, w_gate, b_gate, w_lin, b_lin, w_down, b_down):
    top_idx, gates, rank, counts = routing
    slot, slot_token, tile_expert, n_valid = _route(top_idx, rank, counts)
    xg = jnp.take(u2, slot_token, axis=0)
    y = _moe_grouped(i, tile_expert, n_valid, xg, w_gate, b_gate, w_lin, b_lin, w_down, b_down)
    yk = jnp.take(y, slot.T, axis=0)
    return _combine(yk, gates, x1, mod, ln)


_F_ROW, _F_FIRST, _F_LAST, _F_PREV8, _F_HASPREV, _F_NEXT8, _F_HASNEXT, _F_H0, _F_USEH0, _F_HOUT, _F_WRITEH = range(11)
_N_FIELDS = 11
_SSD_STEPS = N_TOK // SSM_CHUNK


def _ssd_table(reverse):
    rows = []
    for prompt, nseq, slen, base in ((True, BATCH, SEQ, 0), (False, DEC_BATCH, DEC_SEQ, N_PROMPT)):
        nc = slen // SSM_CHUNK
        for s in range(nseq):
            for c in range(nc):
                row = (base + s * slen) // SSM_CHUNK + c
                first, last = (c == nc - 1, c == 0) if reverse else (c == 0, c == nc - 1)
                per = SSM_CHUNK // SUBLANES
                rows.append([
                    row, int(first), int(last),
                    max(row * per - 1, 0), int(c > 0),
                    min(row * per + per, N_TOK // SUBLANES - 1), int(c < nc - 1),
                    0 if prompt else s, int(not prompt),
                    s if prompt else BATCH - 1, int(prompt and last),
                ])
    if reverse:
        rows = rows[::-1]
    return np.asarray(rows, np.int32).T.reshape(-1)


def _fld(tbl, f, s):
    return tbl[f * _SSD_STEPS + s]


def _ssd_chunk(xs, bmat, cmat, dt_raw, dtb_row, alog_row, s_ref, reverse):
    f32, bf16 = jnp.float32, jnp.bfloat16
    base = SSM_HEADS if reverse else 0
    lane = lax.broadcasted_iota(jnp.int32, (1, DT_PAD), 1)
    head_cols = jnp.logical_and(lane >= base, lane < base + SSM_HEADS)
    xr = dt_raw + dtb_row
    dt = jnp.maximum(xr, 0.0) + jnp.log1p(jnp.exp(-jnp.abs(xr)))
    a_row = jnp.where(head_cols, -jnp.exp(alog_row), 0.0)
    d_a = dt * a_row
    r = lax.broadcasted_iota(jnp.int32, (SSM_CHUNK, SSM_CHUNK), 0)
    c = lax.broadcasted_iota(jnp.int32, (SSM_CHUNK, SSM_CHUNK), 1)
    tri = (r <= c) if reverse else (r >= c)
    cum = jnp.dot(tri.astype(f32), d_a, preferred_element_type=f32, precision=lax.Precision.HIGHEST)
    cum_t = cum.T
    total = cum[0:1, :] if reverse else cum[SSM_CHUNK - 1:SSM_CHUNK, :]
    e_in = jnp.exp(cum)
    w_end = dt * jnp.exp(total - cum)
    dec = jnp.exp(total)
    pieces = []
    for g in range(SSM_GROUPS):
        bg = bmat[:, g * D_STATE:(g + 1) * D_STATE]
        cg = cmat[:, g * D_STATE:(g + 1) * D_STATE]
        cb = lax.dot_general(cg, bg, _NT, preferred_element_type=f32)
        for j in range(SSM_HEADS_PER_GROUP):
            h = g * SSM_HEADS_PER_GROUP + j
            ch = base + h
            lo, hi = h * SSM_HEAD_DIM, (h + 1) * SSM_HEAD_DIM
            seg = cum[:, ch:ch + 1] - cum_t[ch:ch + 1, :]
            decay = jnp.exp(jnp.where(tri, seg, NEG_BIG))
            m = (cb * decay).astype(bf16)
            xh = xs[:, lo:hi]
            xdt = (xh * dt[:, ch:ch + 1]).astype(bf16)
            sh = s_ref[lo:hi, :]
            y_off = lax.dot_general(cg, sh.astype(bf16), _NT, preferred_element_type=f32)
            y = jnp.dot(m, xdt, preferred_element_type=f32) + y_off * e_in[:, ch:ch + 1]
            pieces.append(y)
            xw = (xh * w_end[:, ch:ch + 1]).astype(bf16)
            upd = lax.dot_general(xw, bg, _TN, preferred_element_type=f32)
            s_ref[lo:hi, :] = sh * dec[:, ch:ch + 1] + upd
    return jnp.concatenate(pieces, axis=1)


def _ssd_init_state(tbl, step, h0_ref, s_ref):
    @pl.when(_fld(tbl, _F_FIRST, step) == 1)
    def _():
        use = _fld(tbl, _F_USEH0, step) == 1
        s_ref[...] = jnp.where(use, h0_ref[...], 0.0)


def _ssd_fwd_kernel(tbl, xc_ref, xp_ref, xn_ref, dt_ref, cw_ref, cbias_ref, dtb_ref, alog_ref, h0_ref,
                    xs_ref, bc_ref, yf_ref, hout_ref, s_ref):
    step = pl.program_id(0)
    _ssd_init_state(tbl, step, h0_ref, s_ref)
    hp = (_fld(tbl, _F_HASPREV, step) == 1).astype(jnp.float32)
    hn = (_fld(tbl, _F_HASNEXT, step) == 1).astype(jnp.float32)
    window = jnp.concatenate([xp_ref[...] * hp, xc_ref[...], xn_ref[...] * hn], axis=0)
    acc = cbias_ref[...] + cw_ref[0:1, :] * window[SUBLANES - 2:SUBLANES - 2 + SSM_CHUNK, :]
    for k in range(1, SSM_CONV):
        off = SUBLANES - SSM_CONV // 2 + k
        acc = acc + cw_ref[k:k + 1, :] * window[off:off + SSM_CHUNK, :]
    xbc = acc * jax.nn.sigmoid(acc)
    xs = xbc[:, :SSM_INNER]
    bc = xbc[:, SSM_INNER:].astype(jnp.bfloat16)
    xs_ref[...] = xs
    bc_ref[...] = bc
    yf_ref[...] = _ssd_chunk(xs, bc[:, :SSM_BC], bc[:, SSM_BC:], dt_ref[...], dtb_ref[...], alog_ref[...],
                             s_ref, reverse=False)

    @pl.when(_fld(tbl, _F_WRITEH, step) == 1)
    def _():
        hout_ref[...] = s_ref[...]


def _ssd_bwd_kernel(tbl, xs_ref, bc_ref, dt_ref, yf_ref, z_ref, dtb_ref, alog_ref, dskip_ref, nw_ref, h0_ref,
                    y_ref, hout_ref, s_ref):
    step = pl.program_id(0)
    _ssd_init_state(tbl, step, h0_ref, s_ref)
    xs = xs_ref[...]
    bc = bc_ref[...]
    yb = _ssd_chunk(xs, bc[:, :SSM_BC], bc[:, SSM_BC:], dt_ref[...], dtb_ref[...], alog_ref[...],
                    s_ref, reverse=True)
    z = z_ref[...]
    hg = (yf_ref[...] + yb + dskip_ref[...] * xs) * (z * jax.nn.sigmoid(z))
    gw = SSM_INNER // SSM_GROUPS
    outs = []
    for g in range(SSM_GROUPS):
        hgg = hg[:, g * gw:(g + 1) * gw]
        outs.append(hgg * lax.rsqrt(jnp.mean(hgg * hgg, axis=-1, keepdims=True) + RMS_EPS))
    y_ref[...] = (jnp.concatenate(outs, axis=1) * nw_ref[...]).astype(jnp.bfloat16)

    @pl.when(_fld(tbl, _F_WRITEH, step) == 1)
    def _():
        hout_ref[...] = s_ref[...]


def _ssd_mixer(z, xbc_raw, dt_raw, conv_w, conv_b, a_log, dt_bias, d_skip, norm_w, state_in):
    f32 = jnp.float32
    row = lambda f: (lambda s, tbl: (_fld(tbl, f, s), 0))
    chunk_spec = lambda w: pl.BlockSpec((SSM_CHUNK, w), row(_F_ROW))
    const_spec = lambda shape: pl.BlockSpec(shape, lambda s, tbl: (0,) * len(shape))
    h0_spec = lambda d: pl.BlockSpec((None, None, SSM_INNER, D_STATE), lambda s, tbl: (_fld(tbl, _F_H0, s), d, 0, 0))
    hout_spec = pl.BlockSpec((None, SSM_INNER, D_STATE), lambda s, tbl: (_fld(tbl, _F_HOUT, s), 0, 0))
    hout_shape = jax.ShapeDtypeStruct((BATCH, SSM_INNER, D_STATE), f32)
    state_scratch = [pltpu.VMEM((SSM_INNER, D_STATE), f32)]

    cw = jnp.pad(conv_w, ((0, SUBLANES - SSM_CONV), (0, 0)))
    cbias = conv_b.reshape(1, SSM_CONV_DIM)
    dtb = jnp.pad(dt_bias.reshape(1, 2 * SSM_HEADS), ((0, 0), (0, DT_PAD - 2 * SSM_HEADS)))
    alog = jnp.pad(a_log.reshape(1, 2 * SSM_HEADS), ((0, 0), (0, DT_PAD - 2 * SSM_HEADS)))
    dskip = jnp.repeat(d_skip, SSM_HEAD_DIM).reshape(1, SSM_INNER)
    nw = norm_w.reshape(1, SSM_INNER)

    xs, bc, yf, h_f = pl.pallas_call(
        _ssd_fwd_kernel,
        grid_spec=pltpu.PrefetchScalarGridSpec(
            num_scalar_prefetch=1,
            grid=(_SSD_STEPS,),
            in_specs=[
                chunk_spec(SSM_CONV_DIM),
                pl.BlockSpec((SUBLANES, SSM_CONV_DIM), row(_F_PREV8)),
                pl.BlockSpec((SUBLANES, SSM_CONV_DIM), row(_F_NEXT8)),
                chunk_spec(DT_PAD),
                const_spec((SUBLANES, SSM_CONV_DIM)), const_spec((1, SSM_CONV_DIM)),
                const_spec((1, DT_PAD)), const_spec((1, DT_PAD)),
                h0_spec(0),
            ],
            out_specs=[chunk_spec(SSM_INNER), chunk_spec(2 * SSM_BC), chunk_spec(SSM_INNER), hout_spec],
            scratch_shapes=state_scratch,
        ),
        out_shape=[
            jax.ShapeDtypeStruct((N_TOK, SSM_INNER), f32),
            jax.ShapeDtypeStruct((N_TOK, 2 * SSM_BC), jnp.bfloat16),
            jax.ShapeDtypeStruct((N_TOK, SSM_INNER), f32),
            hout_shape,
        ],
        compiler_params=_cparams(("arbitrary",)),
        name="ssd_forward",
    )(jnp.asarray(_ssd_table(False)), xbc_raw, xbc_raw, xbc_raw, dt_raw, cw, cbias, dtb, alog, state_in)

    y, h_b = pl.pallas_call(
        _ssd_bwd_kernel,
        grid_spec=pltpu.PrefetchScalarGridSpec(
            num_scalar_prefetch=1,
            grid=(_SSD_STEPS,),
            in_specs=[
                chunk_spec(SSM_INNER), chunk_spec(2 * SSM_BC), chunk_spec(DT_PAD), chunk_spec(SSM_INNER),
                chunk_spec(SSM_INNER),
                const_spec((1, DT_PAD)), const_spec((1, DT_PAD)),
                const_spec((1, SSM_INNER)), const_spec((1, SSM_INNER)),
                h0_spec(1),
            ],
            out_specs=[chunk_spec(SSM_INNER), hout_spec],
            scratch_shapes=state_scratch,
        ),
        out_shape=[jax.ShapeDtypeStruct((N_TOK, SSM_INNER), jnp.bfloat16), hout_shape],
        compiler_params=_cparams(("arbitrary",)),
        name="ssd_backward",
    )(jnp.asarray(_ssd_table(True)), xs, bc, dt_raw, yf, z, dtb, alog, dskip, nw, state_in)
    return y, h_f, h_b


def _sink_attention(q_heads, keys, vals, sink_ref, kv, masks):
    outs = []
    for g, qh in enumerate(q_heads):
        sk = sink_ref[kv * ATT_GROUP + g]
        scores = []
        for kk, mask in zip(keys, masks):
            s = lax.dot_general(qh, kk, _NT, preferred_element_type=jnp.float32) * ATT_SCALE
            scores.append(s if mask is None else jnp.where(mask, s, NEG_BIG))
        m = sk
        for s in scores:
            m = jnp.maximum(m, jnp.max(s, axis=-1, keepdims=True))
        den = jnp.exp(sk - m)
        acc = None
        for s, vv in zip(scores, vals):
            p = jnp.exp(s - m)
            den = den + jnp.sum(p, axis=-1, keepdims=True)
            pv = jnp.dot(p.astype(jnp.bfloat16), vv, preferred_element_type=jnp.float32)
            acc = pv if acc is None else acc + pv
        outs.append(acc / den)
    return outs


def _head(x, h):
    return x[:, h * ATT_HEAD_DIM:(h + 1) * ATT_HEAD_DIM]


def _attn_ctx_kernel(sink_ref, q_ref, k_ref, v_ref, o_ref):
    bf16 = jnp.bfloat16
    q = q_ref[...].astype(bf16)
    k = k_ref[...].astype(bf16)
    v = v_ref[...].astype(bf16)
    outs = []
    for kv in range(ATT_KV_HEADS):
        qs = [_head(q, kv * ATT_GROUP + g) for g in range(ATT_GROUP)]
        outs += _sink_attention(qs, [_head(k, kv)], [_head(v, kv)], sink_ref, kv, [None])
    o_ref[...] = jnp.concatenate(outs, axis=1).astype(bf16)


def _attend_context(q, k, v, sink):
    return pl.pallas_call(
        _attn_ctx_kernel,
        grid=(BATCH,),
        in_specs=[
            pl.BlockSpec(memory_space=pltpu.SMEM),
            pl.BlockSpec((SEQ, D_MODEL), lambda b: (b, 0)),
            pl.BlockSpec((SEQ, KV_DIM), lambda b: (b, 0)),
            pl.BlockSpec((SEQ, KV_DIM), lambda b: (b, 0)),
        ],
        out_specs=pl.BlockSpec((SEQ, D_MODEL), lambda b: (b, 0)),
        out_shape=jax.ShapeDtypeStruct((N_PROMPT, D_MODEL), jnp.bfloat16),
        compiler_params=_cparams(("arbitrary",)),
        name="attend_context",
    )(sink, q, k, v)


def _rope_tables(width):
    t = np.arange(DEC_SEQ)
    d = np.arange(width) % ATT_HEAD_DIM
    pos = np.where(d[None, :] < ROPE_HALF, (t // GRID_W)[:, None], (t % GRID_W)[:, None]).astype(np.float32)
    inv = (ROPE_BASE ** (-np.arange(ROPE_QUARTER, dtype=np.float32) / ROPE_QUARTER)).astype(np.float32)
    ang = pos * inv[d % ROPE_QUARTER][None, :]
    sign = np.where((d % ROPE_HALF) < ROPE_QUARTER, -1.0, 1.0).astype(np.float32)
    return jnp.asarray(np.cos(ang), jnp.float32), jnp.asarray(np.sin(ang) * sign[None, :], jnp.float32)


def _rope(x, cos, sin_signed):
    width = x.shape[1]
    lane = lax.broadcasted_iota(jnp.int32, (1, width), 1)
    first = (lane % ROPE_HALF) < ROPE_QUARTER
    partner = jnp.where(first, pltpu.roll(x, width - ROPE_QUARTER, 1), pltpu.roll(x, ROPE_QUARTER, 1))
    return x * cos + partner * sin_signed


def _rope_kernel(q_ref, k_ref, v_ref, cos_ref, sin_ref, qo_ref, ko_ref, vo_ref):
    bf16 = jnp.bfloat16
    cos, sin = cos_ref[...], sin_ref[...]
    qo_ref[...] = _rope(q_ref[...], cos, sin).astype(bf16)
    ko_ref[...] = _rope(k_ref[...], cos[:, :KV_DIM], sin[:, :KV_DIM]).astype(bf16)
    vo_ref[...] = v_ref[...].astype(bf16)


def _rope_latent(q, k, v):
    nb = DEC_SEQ // ATT_BLOCK
    off = N_PROMPT // ATT_BLOCK
    cos, sin = _rope_tables(D_MODEL)
    tok = lambda b, i: (off + b * nb + i, 0)
    out = lambda b, i: (b * nb + i, 0)
    return pl.pallas_call(
        _rope_kernel,
        grid=(DEC_BATCH, nb),
        in_specs=[
            pl.BlockSpec((ATT_BLOCK, D_MODEL), tok),
            pl.BlockSpec((ATT_BLOCK, KV_DIM), tok),
            pl.BlockSpec((ATT_BLOCK, KV_DIM), tok),
            pl.BlockSpec((ATT_BLOCK, D_MODEL), lambda b, i: (i, 0)),
            pl.BlockSpec((ATT_BLOCK, D_MODEL), lambda b, i: (i, 0)),
        ],
        out_specs=[
            pl.BlockSpec((ATT_BLOCK, D_MODEL), out),
            pl.BlockSpec((ATT_BLOCK, KV_DIM), out),
            pl.BlockSpec((ATT_BLOCK, KV_DIM), out),
        ],
        out_shape=[
            jax.ShapeDtypeStruct((N_SAMPLE, D_MODEL), jnp.bfloat16),
            jax.ShapeDtypeStruct((N_SAMPLE, KV_DIM), jnp.bfloat16),
            jax.ShapeDtypeStruct((N_SAMPLE, KV_DIM), jnp.bfloat16),
        ],
        compiler_params=_cparams(("arbitrary", "arbitrary")),
        name="rope_latent",
    )(q, k, v, cos, sin)


def _attn_lat_kernel(sink_ref, q_ref, kp_ref, kc_ref, kn_ref, vp_ref, vc_ref, vn_ref, kx_ref, vx_ref, o_ref):
    i = pl.program_id(1)
    nb = pl.num_programs(1)
    q = q_ref[...]
    k_loc = jnp.concatenate([kp_ref[...], kc_ref[...], kn_ref[...]], axis=0)
    v_loc = jnp.concatenate([vp_ref[...], vc_ref[...], vn_ref[...]], axis=0)
    kx, vx = kx_ref[...], vx_ref[...]
    r = lax.broadcasted_iota(jnp.int32, (ATT_BLOCK, 3 * ATT_BLOCK), 0)
    c = lax.broadcasted_iota(jnp.int32, (ATT_BLOCK, 3 * ATT_BLOCK), 1)
    rel = c - ATT_BLOCK - r
    in_window = jnp.logical_and(rel >= -WINDOW, rel <= WINDOW)
    in_seq = jnp.logical_and(jnp.logical_or(c >= ATT_BLOCK, i > 0),
                             jnp.logical_or(c < 2 * ATT_BLOCK, i < nb - 1))
    valid = jnp.logical_and(in_window, in_seq)
    outs = []
    for kv in range(ATT_KV_HEADS):
        qs = [_head(q, kv * ATT_GROUP + g) for g in range(ATT_GROUP)]
        outs += _sink_attention(qs, [_head(k_loc, kv), _head(kx, kv)], [_head(v_loc, kv), _head(vx, kv)],
                                sink_ref, kv, [valid, None])
    o_ref[...] = jnp.concatenate(outs, axis=1).astype(jnp.bfloat16)


def _attend_latent(qr, kr, vb, k_ctx, v_ctx, sink):
    nb = DEC_SEQ // ATT_BLOCK
    cur = lambda b, i: (b * nb + i, 0)
    prv = lambda b, i: (b * nb + jnp.maximum(i - 1, 0), 0)
    nxt = lambda b, i: (b * nb + jnp.minimum(i + 1, nb - 1), 0)
    kvs = lambda f: pl.BlockSpec((ATT_BLOCK, KV_DIM), f)
    ctx = pl.BlockSpec((None, PAST_LEN, KV_DIM), lambda b, i: (b, 0, 0))
    return pl.pallas_call(
        _attn_lat_kernel,
        grid=(DEC_BATCH, nb),
        in_specs=[
            pl.BlockSpec(memory_space=pltpu.SMEM),
            pl.BlockSpec((ATT_BLOCK, D_MODEL), cur),
            kvs(prv), kvs(cur), kvs(nxt), kvs(prv), kvs(cur), kvs(nxt), ctx, ctx,
        ],
        out_specs=pl.BlockSpec((ATT_BLOCK, D_MODEL), cur),
        out_shape=jax.ShapeDtypeStruct((N_SAMPLE, D_MODEL), jnp.bfloat16),
        compiler_params=_cparams(("arbitrary", "arbitrary")),
        name="attend_latent",
    )(sink, qr, kr, kr, kr, vb, vb, vb, k_ctx, v_ctx)


def _gconv_kernel(bg_ref, g_ref, gp_ref, gn_ref, w_ref, o_ref):
    i = pl.program_id(0)
    first = N_PROMPT // CONV_TILE
    per = DEC_SEQ // CONV_TILE
    t = (i - first) % per
    latent = i >= first
    hp = jnp.logical_and(latent, t > 0).astype(jnp.float32)
    hn = jnp.logical_and(latent, t < per - 1).astype(jnp.float32)
    window = jnp.concatenate([gp_ref[...] * hp, g_ref[...], gn_ref[...] * hn], axis=0)
    acc = None
    for k in range(SHORT_CONV):
        off = SUBLANES - SHORT_CONV // 2 + k
        term = w_ref[k:k + 1, :] * window[off:off + CONV_TILE, :]
        acc = term if acc is None else acc + term
    o_ref[...] = (bg_ref[...] * acc).astype(jnp.bfloat16)


def _gated_conv(bg, g, conv_w):
    per = CONV_TILE // SUBLANES
    last = N_TOK // SUBLANES - 1
    cw = jnp.pad(conv_w, ((0, SUBLANES - SHORT_CONV), (0, 0)))
    return pl.pallas_call(
        _gconv_kernel,
        grid=(N_TOK // CONV_TILE,),
        in_specs=[
            pl.BlockSpec((CONV_TILE, D_MODEL), lambda i: (i, 0)),
            pl.BlockSpec((CONV_TILE, D_MODEL), lambda i: (i, 0)),
            pl.BlockSpec((SUBLANES, D_MODEL), lambda i: (jnp.maximum(i * per - 1, 0), 0)),
            pl.BlockSpec((SUBLANES, D_MODEL), lambda i: (jnp.minimum(i * per + per, last), 0)),
            pl.BlockSpec((SUBLANES, D_MODEL), lambda i: (0, 0)),
        ],
        out_specs=pl.BlockSpec((CONV_TILE, D_MODEL), lambda i: (i, 0)),
        out_shape=jax.ShapeDtypeStruct((N_TOK, D_MODEL), jnp.bfloat16),
        compiler_params=_cparams(("arbitrary",)),
        name="gated_conv",
    )(bg, g, g, g, cw)


def kernel(x_prompt, x_sample, cache_k, cache_v, state_ssm, c, c_ctx,
           w_mod, b_mod, ln_w, ln_b,
           w_in_a, conv_w_a, conv_b_a, a_log, dt_bias, d_skip, ssm_norm_w, attn_sink, w_out_a,
           w_in_c, conv_w_c, w_out_c,
           w_router, b_router, w_gate, b_gate, w_lin, b_lin, w_down, b_down):
    bf16 = jnp.bfloat16
    x = jnp.concatenate([x_prompt.reshape(N_PROMPT, D_MODEL), x_sample.reshape(N_SAMPLE, D_MODEL)], axis=0)

    cond = jnp.concatenate([c_ctx[None, :], c, jnp.zeros((COND_PAD - N_COND, D_MODEL), jnp.float32)], axis=0)
    mod_all = _modulation(cond, w_mod, b_mod).reshape(DEPTH, COND_PAD, 6, D_MODEL)
    mod_all = jnp.pad(mod_all, ((0, 0), (0, 0), (0, MOD_ROWS - 6), (0, 0)))

    mod = mod_all[0]
    wa = w_in_a[0].astype(bf16)
    e = np.cumsum((0, SSM_INNER, SSM_CONV_DIM, 2 * SSM_HEADS, D_MODEL, KV_DIM, KV_DIM))
    w_dt = jnp.pad(wa[:, e[2]:e[3]], ((0, 0), (0, DT_PAD - 2 * SSM_HEADS)))
    w_parts = [wa[:, e[0]:e[1]], wa[:, e[1]:e[2]], w_dt, wa[:, e[3]:e[4]], wa[:, e[4]:e[5]], wa[:, e[5]:e[6]]]
    z, xbc_raw, dt_raw, q, k, v = _inproj(x, mod, w_parts)

    state_in = state_ssm[:, 0].reshape(DEC_BATCH, 2, SSM_INNER, D_STATE)
    y_ssm, h_f, h_b = _ssd_mixer(z, xbc_raw, dt_raw, conv_w_a[0], conv_b_a[0], a_log[0], dt_bias[0],
                                 d_skip[0], ssm_norm_w[0], state_in)

    sink = attn_sink[0]
    att_p = _attend_context(q, k, v, sink)
    qr, kr, vb = _rope_latent(q, k, v)
    k_ctx = cache_k[:, 0].reshape(DEC_BATCH, PAST_LEN, KV_DIM).astype(bf16)
    v_ctx = cache_v[:, 0].reshape(DEC_BATCH, PAST_LEN, KV_DIM).astype(bf16)
    att_s = _attend_latent(qr, kr, vb, k_ctx, v_ctx, sink)
    y_att = jnp.concatenate([att_p, att_s], axis=0)

    wo = w_out_a[0].astype(bf16)
    ln = jnp.stack([ln_w[0, 0], ln_b[0, 0]])
    x1, u2, *routing = _outproj([y_ssm, y_att], [wo[:SSM_INNER], wo[SSM_INNER:]], x, mod, ln,
                                w_router[0], b_router[0])
    x = _moe_layer(0, x1, u2, routing, mod, jnp.stack([ln_w[0, 1], ln_b[0, 1]]),
                   w_gate, b_gate, w_lin, b_lin, w_down, b_down)

    mod = mod_all[1]
    wc = w_in_c[0].astype(bf16)
    bg, g = _inproj(x, mod, [wc[:, :D_MODEL], wc[:, D_MODEL:2 * D_MODEL], wc[:, 2 * D_MODEL:]], gate_product=True)
    y_c = _gated_conv(bg, g, conv_w_c[0])
    ln = jnp.stack([ln_w[1, 0], ln_b[1, 0]])
    x1, u2, *routing = _outproj([y_c], [w_out_c[0].astype(bf16)], x, mod, ln, w_router[1], b_router[1])
    x = _moe_layer(1, x1, u2, routing, mod, jnp.stack([ln_w[1, 1], ln_b[1, 1]]),
                   w_gate, b_gate, w_lin, b_lin, w_down, b_down)

    y_prompt = x[:N_PROMPT].reshape(BATCH, SEQ, D_MODEL)
    y_sample = x[N_PROMPT:].reshape(DEC_BATCH, DEC_SEQ, D_MODEL)
    new_k = k[:N_PROMPT].reshape(BATCH, 1, SEQ, ATT_KV_HEADS, ATT_HEAD_DIM)
    new_v = v[:N_PROMPT].reshape(BATCH, 1, SEQ, ATT_KV_HEADS, ATT_HEAD_DIM)
    new_state = jnp.stack([h_f, h_b], axis=1).reshape(BATCH, 1, 2, SSM_HEADS, SSM_HEAD_DIM, D_STATE)
    return (y_prompt, y_sample, new_k, new_v, new_state)
```

```python
import functools

import numpy as np
import jax
import jax.numpy as jnp
from jax import lax
from jax.experimental import pallas as pl
from jax.experimental.pallas import tpu as pltpu

D_MODEL = 1024
BATCH = 32
SEQ = 256
DEPTH = 2
DEC_BATCH = 4
DEC_SEQ = 1024
PAST_LEN = 512
GRID_W = 64
SSM_HEAD_DIM = 64
SSM_INNER = D_MODEL
SSM_HEADS = SSM_INNER // SSM_HEAD_DIM
SSM_GROUPS = 2
SSM_HEADS_PER_GROUP = SSM_HEADS // SSM_GROUPS
D_STATE = 128
SSM_CONV = 5
SSM_CHUNK = 128
SSM_BC = SSM_GROUPS * D_STATE
SSM_CONV_DIM = SSM_INNER + 2 * SSM_BC
ATT_HEAD_DIM = 64
ATT_HEADS = D_MODEL // ATT_HEAD_DIM
ATT_KV_HEADS = 4
ATT_GROUP = ATT_HEADS // ATT_KV_HEADS
WINDOW = 128
ATT_BLOCK = 128
ATT_SCALE = ATT_HEAD_DIM ** -0.5
ROPE_BASE = 10000.0
ROPE_HALF = ATT_HEAD_DIM // 2
ROPE_QUARTER = ATT_HEAD_DIM // 4
SHORT_CONV = 3
N_EXPERTS = 32
TOP_K = 4
D_EXPERT = D_MODEL
SWIGLU_ALPHA = 1.702
SWIGLU_LIMIT = 7.0
N_EVEN = (DEPTH + 1) // 2
DEEPNORM_ALPHA = (2 * DEPTH) ** 0.25
LN_EPS = 1e-5
RMS_EPS = 1e-5
KV_DIM = ATT_KV_HEADS * ATT_HEAD_DIM

N_PROMPT = BATCH * SEQ
N_SAMPLE = DEC_BATCH * DEC_SEQ
N_TOK = N_PROMPT + N_SAMPLE
N_COND = 1 + DEC_BATCH
SUBLANES = 8
LANES = 128
COND_PAD = SUBLANES
MOD_ROWS = SUBLANES
DT_PAD = LANES

TOK_TILE = 512
CONV_TILE = 256
MOE_TILE = 256
MOE_ROWS = N_TOK * TOP_K + N_EXPERTS * MOE_TILE
MOE_TILES = MOE_ROWS // MOE_TILE
VMEM_LIMIT = 56 * 1024 * 1024
NEG_BIG = -1e30

assert N_PROMPT % TOK_TILE == 0 and DEC_SEQ % TOK_TILE == 0
assert SEQ % CONV_TILE == 0 and DEC_SEQ % CONV_TILE == 0

_NT = (((1,), (1,)), ((), ()))
_TN = (((0,), (0,)), ((), ()))


def _cparams(sem):
    return pltpu.CompilerParams(dimension_semantics=sem, vmem_limit_bytes=VMEM_LIMIT)


def _cond_row(i):
    first = N_PROMPT // TOK_TILE
    per = DEC_SEQ // TOK_TILE
    return jnp.where(i < first, 0, 1 + (i - first) // per)


def _bf16_dot(a, b):
    return jnp.dot(a.astype(jnp.bfloat16), b.astype(jnp.bfloat16), preferred_element_type=jnp.float32)


def _mod_kernel(c_ref, w_ref, b_ref, o_ref):
    c = c_ref[...]
    s = c * jax.nn.sigmoid(c)
    o_ref[...] = _bf16_dot(s, w_ref[...]) + b_ref[...]


def _modulation(cond, w_mod, b_mod):
    tn = 1536
    return pl.pallas_call(
        _mod_kernel,
        grid=(DEPTH, 6 * D_MODEL // tn),
        in_specs=[
            pl.BlockSpec((COND_PAD, D_MODEL), lambda l, n: (0, 0)),
            pl.BlockSpec((None, D_MODEL, tn), lambda l, n: (l, 0, n)),
            pl.BlockSpec((None, 1, tn), lambda l, n: (l, 0, n)),
        ],
        out_specs=pl.BlockSpec((None, COND_PAD, tn), lambda l, n: (l, 0, n)),
        out_shape=jax.ShapeDtypeStruct((DEPTH, COND_PAD, 6 * D_MODEL), jnp.float32),
        compiler_params=_cparams(("arbitrary", "arbitrary")),
        name="modulation",
    )(cond, w_mod, b_mod.reshape(DEPTH, 1, 6 * D_MODEL))


def _inproj_kernel(x_ref, mod_ref, *refs, gate_product):
    n = len(refs) // 2 + (1 if gate_product else 0)
    w_refs, o_refs = refs[:n], refs[n:]
    u = x_ref[...] * (1.0 + mod_ref[1:2, :]) + mod_ref[0:1, :]
    ub = u.astype(jnp.bfloat16)
    outs = [jnp.dot(ub, w_ref[...], preferred_element_type=jnp.float32) for w_ref in w_refs]
    if gate_product:
        outs = outs[:-2] + [outs[-2] * outs[-1]]
    for o_ref, o in zip(o_refs, outs):
        o_ref[...] = o


def _inproj(x, mod, weights, gate_product=False):
    out_w = [w.shape[1] for w in weights]
    if gate_product:
        out_w = out_w[:-1]
    return pl.pallas_call(
        functools.partial(_inproj_kernel, gate_product=gate_product),
        grid=(N_TOK // TOK_TILE,),
        in_specs=[
            pl.BlockSpec((TOK_TILE, D_MODEL), lambda i: (i, 0)),
            pl.BlockSpec((None, MOD_ROWS, D_MODEL), lambda i: (_cond_row(i), 0, 0)),
        ] + [pl.BlockSpec(w.shape, lambda i: (0, 0)) for w in weights],
        out_specs=[pl.BlockSpec((TOK_TILE, n), lambda i: (i, 0)) for n in out_w],
        out_shape=[jax.ShapeDtypeStruct((N_TOK, n), jnp.float32) for n in out_w],
        compiler_params=_cparams(("arbitrary",)),
        name="inproj",
    )(x, mod, *weights)


def _layer_norm(v, w, b):
    mu = jnp.mean(v, axis=-1, keepdims=True)
    d = v - mu
    var = jnp.mean(d * d, axis=-1, keepdims=True)
    return d * lax.rsqrt(var + LN_EPS) * w + b


def _outproj_kernel(*refs, n_in):
    y_refs = refs[:n_in]
    w_refs = refs[n_in:2 * n_in]
    (x_ref, mod_ref, ln_ref, wr_ref, br_ref,
     x1_ref, u2_ref, idx_ref, gate_ref, rank_ref, cnt_ref, run_ref) = refs[2 * n_in:]
    f32 = jnp.float32

    @pl.when(pl.program_id(0) == 0)
    def _():
        run_ref[...] = jnp.zeros_like(run_ref)

    mix = _bf16_dot(y_refs[0][...], w_refs[0][...])
    for y_ref, w_ref in zip(y_refs[1:], w_refs[1:]):
        mix = mix + _bf16_dot(y_ref[...], w_ref[...])
    x1 = _layer_norm(DEEPNORM_ALPHA * x_ref[...] + mod_ref[2:3, :] * mix, ln_ref[0:1, :], ln_ref[1:2, :])
    x1_ref[...] = x1
    u2 = x1 * (1.0 + mod_ref[4:5, :]) + mod_ref[3:4, :]
    u2_ref[...] = u2
    logits = jnp.dot(u2, wr_ref[...], preferred_element_type=f32, precision=lax.Precision.HIGHEST) + br_ref[...]

    lane = lax.broadcasted_iota(jnp.int32, (TOK_TILE, N_EXPERTS), 1)
    work = logits
    vals, sels, idxs = [], [], []
    for _ in range(TOP_K):
        m = jnp.max(work, axis=-1, keepdims=True)
        idx = jnp.min(jnp.where(work == m, lane, N_EXPERTS), axis=-1, keepdims=True)
        sel = lane == idx
        vals.append(m)
        idxs.append(idx)
        sels.append(sel)
        work = jnp.where(sel, -jnp.inf, work)
    exps = [jnp.exp(v - vals[0]) for v in vals]
    den = exps[0]
    for e in exps[1:]:
        den = den + e
    onehot = sels[0].astype(f32)
    for sel in sels[1:]:
        onehot = onehot + sel.astype(f32)
    r = lax.broadcasted_iota(jnp.int32, (TOK_TILE, TOK_TILE), 0)
    c = lax.broadcasted_iota(jnp.int32, (TOK_TILE, TOK_TILE), 1)
    before = jnp.dot((r > c).astype(jnp.bfloat16), onehot.astype(jnp.bfloat16), preferred_element_type=f32)
    rank_all = before + run_ref[...]
    run = run_ref[...] + jnp.sum(onehot, axis=0, keepdims=True)
    run_ref[...] = run
    cnt_ref[...] = run.astype(jnp.int32)
    col = lax.broadcasted_iota(jnp.int32, (TOK_TILE, TOP_K), 1)
    idx_o = jnp.zeros((TOK_TILE, TOP_K), jnp.int32)
    gate_o = jnp.zeros((TOK_TILE, TOP_K), f32)
    rank_o = jnp.zeros((TOK_TILE, TOP_K), f32)
    for k in range(TOP_K):
        rank_k = jnp.sum(jnp.where(sels[k], rank_all, 0.0), axis=-1, keepdims=True)
        idx_o = jnp.where(col == k, idxs[k], idx_o)
        gate_o = jnp.where(col == k, exps[k] / den, gate_o)
        rank_o = jnp.where(col == k, rank_k, rank_o)
    idx_ref[...] = idx_o
    gate_ref[...] = gate_o
    rank_ref[...] = rank_o.astype(jnp.int32)


def _outproj(ys, ws, x, mod, ln, w_router, b_router):
    n_in = len(ys)
    return pl.pallas_call(
        functools.partial(_outproj_kernel, n_in=n_in),
        grid=(N_TOK // TOK_TILE,),
        in_specs=[pl.BlockSpec((TOK_TILE, y.shape[1]), lambda i: (i, 0)) for y in ys]
        + [pl.BlockSpec(w.shape, lambda i: (0, 0)) for w in ws]
        + [
            pl.BlockSpec((TOK_TILE, D_MODEL), lambda i: (i, 0)),
            pl.BlockSpec((None, MOD_ROWS, D_MODEL), lambda i: (_cond_row(i), 0, 0)),
            pl.BlockSpec((2, D_MODEL), lambda i: (0, 0)),
            pl.BlockSpec((D_MODEL, N_EXPERTS), lambda i: (0, 0)),
            pl.BlockSpec((1, N_EXPERTS), lambda i: (0, 0)),
        ],
        out_specs=[
            pl.BlockSpec((TOK_TILE, D_MODEL), lambda i: (i, 0)),
            pl.BlockSpec((TOK_TILE, D_MODEL), lambda i: (i, 0)),
            pl.BlockSpec((TOK_TILE, TOP_K), lambda i: (i, 0)),
            pl.BlockSpec((TOK_TILE, TOP_K), lambda i: (i, 0)),
            pl.BlockSpec((TOK_TILE, TOP_K), lambda i: (i, 0)),
            pl.BlockSpec((1, N_EXPERTS), lambda i: (0, 0)),
        ],
        out_shape=[
            jax.ShapeDtypeStruct((N_TOK, D_MODEL), jnp.float32),
            jax.ShapeDtypeStruct((N_TOK, D_MODEL), jnp.float32),
            jax.ShapeDtypeStruct((N_TOK, TOP_K), jnp.int32),
            jax.ShapeDtypeStruct((N_TOK, TOP_K), jnp.float32),
            jax.ShapeDtypeStruct((N_TOK, TOP_K), jnp.int32),
            jax.ShapeDtypeStruct((1, N_EXPERTS), jnp.int32),
        ],
        scratch_shapes=[pltpu.VMEM((1, N_EXPERTS), jnp.float32)],
        compiler_params=_cparams(("arbitrary",)),
        name="outproj_ln_router",
    )(*ys, *ws, x, mod, ln, w_router, b_router.reshape(1, N_EXPERTS))


def _moe_kernel(te_ref, nv_ref, src0_ref, last_ref, stok_ref, x_hbm,
                wg_ref, bg_ref, wl_ref, bl_ref, wd_ref, bd_ref, o_ref,
                wg_s, wl_s, wd_s, xbuf, sem):
    i = pl.program_id(0)
    nv = nv_ref[0]
    valid = i < nv
    prev = te_ref[jnp.maximum(i - 1, 0)]
    new_expert = jnp.logical_or(i == 0, te_ref[i] != prev)

    def start_gather(tile, buf):
        base, last = src0_ref[tile], last_ref[tile]
        for r in range(MOE_TILE):
            tok = stok_ref[jnp.minimum(base + r, last)]
            pltpu.make_async_copy(x_hbm.at[pl.ds(tok, 1)], xbuf.at[buf, pl.ds(r, 1)], sem.at[buf]).start()

    def wait_gather(buf):
        pltpu.make_async_copy(x_hbm.at[pl.ds(0, MOE_TILE)], xbuf.at[buf], sem.at[buf]).wait()

    @pl.when(i == 0)
    def _():
        start_gather(0, 0)

    @pl.when(jnp.logical_and(valid, new_expert))
    def _():
        wg_s[...] = wg_ref[...].astype(jnp.bfloat16)
        wl_s[...] = wl_ref[...].astype(jnp.bfloat16)
        wd_s[...] = wd_ref[...].astype(jnp.bfloat16)

    for buf in range(2):
        @pl.when(jnp.logical_and(valid, i % 2 == buf))
        def _():
            wait_gather(buf)
            start_gather(jnp.minimum(i + 1, nv - 1), 1 - buf)
            x = xbuf[buf].astype(jnp.bfloat16)
            hg = jnp.dot(x, wg_s[...], preferred_element_type=jnp.float32) + bg_ref[...]
            hl = jnp.dot(x, wl_s[...], preferred_element_type=jnp.float32) + bl_ref[...]
            hg = jnp.minimum(hg, SWIGLU_LIMIT)
            hl = jnp.clip(hl, -SWIGLU_LIMIT, SWIGLU_LIMIT)
            h = hg * jax.nn.sigmoid(SWIGLU_ALPHA * hg) * (hl + 1.0)
            o_ref[...] = (jnp.dot(h.astype(jnp.bfloat16), wd_s[...], preferred_element_type=jnp.float32)
                          + bd_ref[...])

        @pl.when(jnp.logical_and(i == nv - 1, i % 2 == buf))
        def _():
            wait_gather(1 - buf)

    @pl.when(jnp.logical_not(valid))
    def _():
        o_ref[...] = jnp.zeros_like(o_ref)


def _moe_grouped(layer, tile_expert, n_valid, src0, src_last, sorted_tok, x, w_gate, b_gate, w_lin, b_lin,
                 w_down, b_down):
    wspec = pl.BlockSpec((None, None, D_MODEL, D_EXPERT), lambda i, te, *_: (layer, te[i], 0, 0))
    bspec = pl.BlockSpec((None, None, 1, D_EXPERT), lambda i, te, *_: (layer, te[i], 0, 0))
    grid_spec = pltpu.PrefetchScalarGridSpec(
        num_scalar_prefetch=5,
        grid=(MOE_TILES,),
        in_specs=[pl.BlockSpec(memory_space=pl.ANY), wspec, bspec, wspec, bspec, wspec, bspec],
        out_specs=pl.BlockSpec((MOE_TILE, D_MODEL), lambda i, *_: (i, 0)),
        scratch_shapes=[pltpu.VMEM((D_MODEL, D_EXPERT), jnp.bfloat16)] * 3
        + [pltpu.VMEM((2, MOE_TILE, D_MODEL), jnp.float32), pltpu.SemaphoreType.DMA((2,))],
    )
    bshape = (DEPTH, N_EXPERTS, 1, D_EXPERT)
    return pl.pallas_call(
        _moe_kernel,
        grid_spec=grid_spec,
        out_shape=jax.ShapeDtypeStruct((MOE_ROWS, D_MODEL), jnp.float32),
        compiler_params=_cparams(("arbitrary",)),
        name="moe_grouped",
    )(tile_expert, n_valid, src0, src_last, sorted_tok, x, w_gate, b_gate.reshape(bshape),
      w_lin, b_lin.reshape(bshape), w_down, b_down.reshape(bshape))


def _combine_kernel(y_ref, g_ref, x_ref, mod_ref, ln_ref, o_ref):
    g = g_ref[...]
    ff = g[:, 0:1] * y_ref[0]
    for k in range(1, TOP_K):
        ff = ff + g[:, k:k + 1] * y_ref[k]
    o_ref[...] = _layer_norm(DEEPNORM_ALPHA * x_ref[...] + mod_ref[5:6, :] * ff, ln_ref[0:1, :], ln_ref[1:2, :])


def _combine(yk, gates, x, mod, ln):
    return pl.pallas_call(
        _combine_kernel,
        grid=(N_TOK // TOK_TILE,),
        in_specs=[
            pl.BlockSpec((TOP_K, TOK_TILE, D_MODEL), lambda i: (0, i, 0)),
            pl.BlockSpec((TOK_TILE, TOP_K), lambda i: (i, 0)),
            pl.BlockSpec((TOK_TILE, D_MODEL), lambda i: (i, 0)),
            pl.BlockSpec((None, MOD_ROWS, D_MODEL), lambda i: (_cond_row(i), 0, 0)),
            pl.BlockSpec((2, D_MODEL), lambda i: (0, 0)),
        ],
        out_specs=pl.BlockSpec((TOK_TILE, D_MODEL), lambda i: (i, 0)),
        out_shape=jax.ShapeDtypeStruct((N_TOK, D_MODEL), jnp.float32),
        compiler_params=_cparams(("arbitrary",)),
        name="combine_ln",
    )(yk, gates, x, mod, ln)


def _route(top_idx, rank, counts):
    counts = counts.reshape(N_EXPERTS)
    padded = (counts + MOE_TILE - 1) // MOE_TILE * MOE_TILE
    pend = jnp.cumsum(padded)
    pstart = pend - padded
    start = jnp.cumsum(counts) - counts
    slot = jnp.take(pstart, top_idx, mode="clip") + rank
    key = jnp.take(start, top_idx, mode="clip") + rank
    tok = jnp.broadcast_to(jnp.arange(N_TOK, dtype=jnp.int32)[:, None], (N_TOK, TOP_K))
    _, sorted_tok = lax.sort((key.reshape(-1), tok.reshape(-1)), num_keys=1)
    n_valid = (pend[-1] // MOE_TILE).astype(jnp.int32)
    tile_start = jnp.arange(MOE_TILES, dtype=jnp.int32) * MOE_TILE
    tile_expert = jnp.sum((tile_start[:, None] >= pend[None, :]).astype(jnp.int32), axis=1)
    tile_expert = jnp.where(tile_start < pend[-1], tile_expert, tile_expert[n_valid - 1])
    src0 = tile_start - jnp.take(pstart, tile_expert) + jnp.take(start, tile_expert)
    src_last = jnp.take(start, tile_expert) + jnp.take(counts, tile_expert) - 1
    return slot, sorted_tok, tile_expert, n_valid.reshape(1), src0, src_last


def _moe_layer(i, x1, u2, routing, mod, ln, w_gate, b_gate, w_lin, b_lin, w_down, b_down):
    top_idx, gates, rank, counts = routing
    slot, sorted_tok, tile_expert, n_valid, src0, src_last = _route(top_idx, rank, counts)
    y = _moe_grouped(i, tile_expert, n_valid, src0, src_last, sorted_tok, u2,
                     w_gate, b_gate, w_lin, b_lin, w_down, b_down)
    yk = jnp.take(y, slot.T, axis=0, mode="clip")
    return _combine(yk, gates, x1, mod, ln)


_F_ROW, _F_FIRST, _F_LAST, _F_PREV8, _F_HASPREV, _F_NEXT8, _F_HASNEXT, _F_H0, _F_USEH0, _F_HOUT, _F_WRITEH = range(11)
_N_FIELDS = 11
_SSD_STEPS = N_TOK // SSM_CHUNK


def _ssd_table(reverse):
    rows = []
    for prompt, nseq, slen, base in ((True, BATCH, SEQ, 0), (False, DEC_BATCH, DEC_SEQ, N_PROMPT)):
        nc = slen // SSM_CHUNK
        for s in range(nseq):
            for c in range(nc):
                row = (base + s * slen) // SSM_CHUNK + c
                first, last = (c == nc - 1, c == 0) if reverse else (c == 0, c == nc - 1)
                per = SSM_CHUNK // SUBLANES
                rows.append([
                    row, int(first), int(last),
                    max(row * per - 1, 0), int(c > 0),
                    min(row * per + per, N_TOK // SUBLANES - 1), int(c < nc - 1),
                    0 if prompt else s, int(not prompt),
                    s if prompt else BATCH - 1, int(prompt and last),
                ])
    if reverse:
        rows = rows[::-1]
    return np.asarray(rows, np.int32).T.reshape(-1)


def _fld(tbl, f, s):
    return tbl[f * _SSD_STEPS + s]


def _ssd_chunk(xs, bmat, cmat, dt_raw, dtb_row, alog_row, s_ref, reverse):
    f32, bf16 = jnp.float32, jnp.bfloat16
    base = SSM_HEADS if reverse else 0
    lane = lax.broadcasted_iota(jnp.int32, (1, DT_PAD), 1)
    head_cols = jnp.logical_and(lane >= base, lane < base + SSM_HEADS)
    xr = dt_raw + dtb_row
    dt = jnp.maximum(xr, 0.0) + jnp.log1p(jnp.exp(-jnp.abs(xr)))
    a_row = jnp.where(head_cols, -jnp.exp(alog_row), 0.0)
    d_a = dt * a_row
    r = lax.broadcasted_iota(jnp.int32, (SSM_CHUNK, SSM_CHUNK), 0)
    c = lax.broadcasted_iota(jnp.int32, (SSM_CHUNK, SSM_CHUNK), 1)
    tri = (r <= c) if reverse else (r >= c)
    cum = jnp.dot(tri.astype(f32), d_a, preferred_element_type=f32, precision=lax.Precision.HIGHEST)
    cum_t = cum.T
    total = cum[0:1, :] if reverse else cum[SSM_CHUNK - 1:SSM_CHUNK, :]
    e_in = jnp.exp(cum)
    w_end = dt * jnp.exp(total - cum)
    dec = jnp.exp(total)
    pieces = []
    for g in range(SSM_GROUPS):
        bg = bmat[:, g * D_STATE:(g + 1) * D_STATE]
        cg = cmat[:, g * D_STATE:(g + 1) * D_STATE]
        cb = lax.dot_general(cg, bg, _NT, preferred_element_type=f32)
        for j in range(SSM_HEADS_PER_GROUP):
            h = g * SSM_HEADS_PER_GROUP + j
            ch = base + h
            lo, hi = h * SSM_HEAD_DIM, (h + 1) * SSM_HEAD_DIM
            seg = cum[:, ch:ch + 1] - cum_t[ch:ch + 1, :]
            decay = jnp.exp(jnp.where(tri, seg, NEG_BIG))
            m = (cb * decay).astype(bf16)
            xh = xs[:, lo:hi]
            xdt = (xh * dt[:, ch:ch + 1]).astype(bf16)
            sh = s_ref[lo:hi, :]
            y_off = lax.dot_general(cg, sh.astype(bf16), _NT, preferred_element_type=f32)
            y = jnp.dot(m, xdt, preferred_element_type=f32) + y_off * e_in[:, ch:ch + 1]
            pieces.append(y)
            xw = (xh * w_end[:, ch:ch + 1]).astype(bf16)
            upd = lax.dot_general(xw, bg, _TN, preferred_element_type=f32)
            s_ref[lo:hi, :] = sh * dec[:, ch:ch + 1] + upd
    return jnp.concatenate(pieces, axis=1)


def _ssd_init_state(tbl, step, h0_ref, s_ref):
    @pl.when(_fld(tbl, _F_FIRST, step) == 1)
    def _():
        use = _fld(tbl, _F_USEH0, step) == 1
        s_ref[...] = jnp.where(use, h0_ref[...], 0.0)


def _ssd_fwd_kernel(tbl, xc_ref, xp_ref, xn_ref, dt_ref, cw_ref, cbias_ref, dtb_ref, alog_ref, h0_ref,
                    xs_ref, bc_ref, yf_ref, hout_ref, s_ref):
    step = pl.program_id(0)
    _ssd_init_state(tbl, step, h0_ref, s_ref)
    hp = (_fld(tbl, _F_HASPREV, step) == 1).astype(jnp.float32)
    hn = (_fld(tbl, _F_HASNEXT, step) == 1).astype(jnp.float32)
    window = jnp.concatenate([xp_ref[...] * hp, xc_ref[...], xn_ref[...] * hn], axis=0)
    acc = cbias_ref[...] + cw_ref[0:1, :] * window[SUBLANES - 2:SUBLANES - 2 + SSM_CHUNK, :]
    for k in range(1, SSM_CONV):
        off = SUBLANES - SSM_CONV // 2 + k
        acc = acc + cw_ref[k:k + 1, :] * window[off:off + SSM_CHUNK, :]
    xbc = acc * jax.nn.sigmoid(acc)
    xs = xbc[:, :SSM_INNER]
    bc = xbc[:, SSM_INNER:].astype(jnp.bfloat16)
    xs_ref[...] = xs
    bc_ref[...] = bc
    yf_ref[...] = _ssd_chunk(xs, bc[:, :SSM_BC], bc[:, SSM_BC:], dt_ref[...], dtb_ref[...], alog_ref[...],
                             s_ref, reverse=False)

    @pl.when(_fld(tbl, _F_WRITEH, step) == 1)
    def _():
        hout_ref[...] = s_ref[...]


def _ssd_bwd_kernel(tbl, xs_ref, bc_ref, dt_ref, yf_ref, z_ref, dtb_ref, alog_ref, dskip_ref, nw_ref, h0_ref,
                    y_ref, hout_ref, s_ref):
    step = pl.program_id(0)
    _ssd_init_state(tbl, step, h0_ref, s_ref)
    xs = xs_ref[...]
    bc = bc_ref[...]
    yb = _ssd_chunk(xs, bc[:, :SSM_BC], bc[:, SSM_BC:], dt_ref[...], dtb_ref[...], alog_ref[...],
                    s_ref, reverse=True)
    z = z_ref[...]
    hg = (yf_ref[...] + yb + dskip_ref[...] * xs) * (z * jax.nn.sigmoid(z))
    gw = SSM_INNER // SSM_GROUPS
    outs = []
    for g in range(SSM_GROUPS):
        hgg = hg[:, g * gw:(g + 1) * gw]
        outs.append(hgg * lax.rsqrt(jnp.mean(hgg * hgg, axis=-1, keepdims=True) + RMS_EPS))
    y_ref[...] = (jnp.concatenate(outs, axis=1) * nw_ref[...]).astype(jnp.bfloat16)

    @pl.when(_fld(tbl, _F_WRITEH, step) == 1)
    def _():
        hout_ref[...] = s_ref[...]


def _ssd_mixer(z, xbc_raw, dt_raw, conv_w, conv_b, a_log, dt_bias, d_skip, norm_w, state_in):
    f32 = jnp.float32
    row = lambda f: (lambda s, tbl: (_fld(tbl, f, s), 0))
    chunk_spec = lambda w: pl.BlockSpec((SSM_CHUNK, w), row(_F_ROW))
    const_spec = lambda shape: pl.BlockSpec(shape, lambda s, tbl: (0,) * len(shape))
    h0_spec = lambda d: pl.BlockSpec((None, None, SSM_INNER, D_STATE), lambda s, tbl: (_fld(tbl, _F_H0, s), d, 0, 0))
    hout_spec = pl.BlockSpec((None, SSM_INNER, D_STATE), lambda s, tbl: (_fld(tbl, _F_HOUT, s), 0, 0))
    hout_shape = jax.ShapeDtypeStruct((BATCH, SSM_INNER, D_STATE), f32)
    state_scratch = [pltpu.VMEM((SSM_INNER, D_STATE), f32)]

    cw = jnp.pad(conv_w, ((0, SUBLANES - SSM_CONV), (0, 0)))
    cbias = conv_b.reshape(1, SSM_CONV_DIM)
    dtb = jnp.pad(dt_bias.reshape(1, 2 * SSM_HEADS), ((0, 0), (0, DT_PAD - 2 * SSM_HEADS)))
    alog = jnp.pad(a_log.reshape(1, 2 * SSM_HEADS), ((0, 0), (0, DT_PAD - 2 * SSM_HEADS)))
    dskip = jnp.repeat(d_skip, SSM_HEAD_DIM).reshape(1, SSM_INNER)
    nw = norm_w.reshape(1, SSM_INNER)

    xs, bc, yf, h_f = pl.pallas_call(
        _ssd_fwd_kernel,
        grid_spec=pltpu.PrefetchScalarGridSpec(
            num_scalar_prefetch=1,
            grid=(_SSD_STEPS,),
            in_specs=[
                chunk_spec(SSM_CONV_DIM),
                pl.BlockSpec((SUBLANES, SSM_CONV_DIM), row(_F_PREV8)),
                pl.BlockSpec((SUBLANES, SSM_CONV_DIM), row(_F_NEXT8)),
                chunk_spec(DT_PAD),
                const_spec((SUBLANES, SSM_CONV_DIM)), const_spec((1, SSM_CONV_DIM)),
                const_spec((1, DT_PAD)), const_spec((1, DT_PAD)),
                h0_spec(0),
            ],
            out_specs=[chunk_spec(SSM_INNER), chunk_spec(2 * SSM_BC), chunk_spec(SSM_INNER), hout_spec],
            scratch_shapes=state_scratch,
        ),
        out_shape=[
            jax.ShapeDtypeStruct((N_TOK, SSM_INNER), f32),
            jax.ShapeDtypeStruct((N_TOK, 2 * SSM_BC), jnp.bfloat16),
            jax.ShapeDtypeStruct((N_TOK, SSM_INNER), f32),
            hout_shape,
        ],
        compiler_params=_cparams(("arbitrary",)),
        name="ssd_forward",
    )(jnp.asarray(_ssd_table(False)), xbc_raw, xbc_raw, xbc_raw, dt_raw, cw, cbias, dtb, alog, state_in)

    y, h_b = pl.pallas_call(
        _ssd_bwd_kernel,
        grid_spec=pltpu.PrefetchScalarGridSpec(
            num_scalar_prefetch=1,
            grid=(_SSD_STEPS,),
            in_specs=[
                chunk_spec(SSM_INNER), chunk_spec(2 * SSM_BC), chunk_spec(DT_PAD), chunk_spec(SSM_INNER),
                chunk_spec(SSM_INNER),
                const_spec((1, DT_PAD)), const_spec((1, DT_PAD)),
                const_spec((1, SSM_INNER)), const_spec((1, SSM_INNER)),
                h0_spec(1),
            ],
            out_specs=[chunk_spec(SSM_INNER), hout_spec],
            scratch_shapes=state_scratch,
        ),
        out_shape=[jax.ShapeDtypeStruct((N_TOK, SSM_INNER), jnp.bfloat16), hout_shape],
        compiler_params=_cparams(("arbitrary",)),
        name="ssd_backward",
    )(jnp.asarray(_ssd_table(True)), xs, bc, dt_raw, yf, z, dtb, alog, dskip, nw, state_in)
    return y, h_f, h_b


def _sink_attention(q_heads, keys, vals, sink_ref, kv, masks):
    outs = []
    for g, qh in enumerate(q_heads):
        sk = sink_ref[kv * ATT_GROUP + g]
        scores = []
        for kk, mask in zip(keys, masks):
            s = lax.dot_general(qh, kk, _NT, preferred_element_type=jnp.float32) * ATT_SCALE
            scores.append(s if mask is None else jnp.where(mask, s, NEG_BIG))
        m = sk
        for s in scores:
            m = jnp.maximum(m, jnp.max(s, axis=-1, keepdims=True))
        den = jnp.exp(sk - m)
        acc = None
        for s, vv in zip(scores, vals):
            p = jnp.exp(s - m)
            den = den + jnp.sum(p, axis=-1, keepdims=True)
            pv = jnp.dot(p.astype(jnp.bfloat16), vv, preferred_element_type=jnp.float32)
            acc = pv if acc is None else acc + pv
        outs.append(acc / den)
    return outs


def _head(x, h):
    return x[:, h * ATT_HEAD_DIM:(h + 1) * ATT_HEAD_DIM]


def _attn_ctx_kernel(sink_ref, q_ref, k_ref, v_ref, o_ref):
    bf16 = jnp.bfloat16
    q = q_ref[...].astype(bf16)
    k = k_ref[...].astype(bf16)
    v = v_ref[...].astype(bf16)
    outs = []
    for kv in range(ATT_KV_HEADS):
        qs = [_head(q, kv * ATT_GROUP + g) for g in range(ATT_GROUP)]
        outs += _sink_attention(qs, [_head(k, kv)], [_head(v, kv)], sink_ref, kv, [None])
    o_ref[...] = jnp.concatenate(outs, axis=1).astype(bf16)


def _attend_context(q, k, v, sink):
    return pl.pallas_call(
        _attn_ctx_kernel,
        grid=(BATCH,),
        in_specs=[
            pl.BlockSpec(memory_space=pltpu.SMEM),
            pl.BlockSpec((SEQ, D_MODEL), lambda b: (b, 0)),
            pl.BlockSpec((SEQ, KV_DIM), lambda b: (b, 0)),
            pl.BlockSpec((SEQ, KV_DIM), lambda b: (b, 0)),
        ],
        out_specs=pl.BlockSpec((SEQ, D_MODEL), lambda b: (b, 0)),
        out_shape=jax.ShapeDtypeStruct((N_PROMPT, D_MODEL), jnp.bfloat16),
        compiler_params=_cparams(("arbitrary",)),
        name="attend_context",
    )(sink, q, k, v)


def _rope_tables(width):
    t = np.arange(DEC_SEQ)
    d = np.arange(width) % ATT_HEAD_DIM
    pos = np.where(d[None, :] < ROPE_HALF, (t // GRID_W)[:, None], (t % GRID_W)[:, None]).astype(np.float32)
    inv = (ROPE_BASE ** (-np.arange(ROPE_QUARTER, dtype=np.float32) / ROPE_QUARTER)).astype(np.float32)
    ang = pos * inv[d % ROPE_QUARTER][None, :]
    sign = np.where((d % ROPE_HALF) < ROPE_QUARTER, -1.0, 1.0).astype(np.float32)
    return jnp.asarray(np.cos(ang), jnp.float32), jnp.asarray(np.sin(ang) * sign[None, :], jnp.float32)


def _rope(x, cos, sin_signed):
    width = x.shape[1]
    lane = lax.broadcasted_iota(jnp.int32, (1, width), 1)
    first = (lane % ROPE_HALF) < ROPE_QUARTER
    partner = jnp.where(first, pltpu.roll(x, width - ROPE_QUARTER, 1), pltpu.roll(x, ROPE_QUARTER, 1))
    return x * cos + partner * sin_signed


def _rope_kernel(q_ref, k_ref, v_ref, cos_ref, sin_ref, qo_ref, ko_ref, vo_ref):
    bf16 = jnp.bfloat16
    cos, sin = cos_ref[...], sin_ref[...]
    qo_ref[...] = _rope(q_ref[...], cos, sin).astype(bf16)
    ko_ref[...] = _rope(k_ref[...], cos[:, :KV_DIM], sin[:, :KV_DIM]).astype(bf16)
    vo_ref[...] = v_ref[...].astype(bf16)


def _rope_latent(q, k, v):
    nb = DEC_SEQ // ATT_BLOCK
    off = N_PROMPT // ATT_BLOCK
    cos, sin = _rope_tables(D_MODEL)
    tok = lambda b, i: (off + b * nb + i, 0)
    out = lambda b, i: (b * nb + i, 0)
    return pl.pallas_call(
        _rope_kernel,
        grid=(DEC_BATCH, nb),
        in_specs=[
            pl.BlockSpec((ATT_BLOCK, D_MODEL), tok),
            pl.BlockSpec((ATT_BLOCK, KV_DIM), tok),
            pl.BlockSpec((ATT_BLOCK, KV_DIM), tok),
            pl.BlockSpec((ATT_BLOCK, D_MODEL), lambda b, i: (i, 0)),
            pl.BlockSpec((ATT_BLOCK, D_MODEL), lambda b, i: (i, 0)),
        ],
        out_specs=[
            pl.BlockSpec((ATT_BLOCK, D_MODEL), out),
            pl.BlockSpec((ATT_BLOCK, KV_DIM), out),
            pl.BlockSpec((ATT_BLOCK, KV_DIM), out),
        ],
        out_shape=[
            jax.ShapeDtypeStruct((N_SAMPLE, D_MODEL), jnp.bfloat16),
            jax.ShapeDtypeStruct((N_SAMPLE, KV_DIM), jnp.bfloat16),
            jax.ShapeDtypeStruct((N_SAMPLE, KV_DIM), jnp.bfloat16),
        ],
        compiler_params=_cparams(("arbitrary", "arbitrary")),
        name="rope_latent",
    )(q, k, v, cos, sin)


def _attn_lat_kernel(sink_ref, q_ref, kp_ref, kc_ref, kn_ref, vp_ref, vc_ref, vn_ref, kx_ref, vx_ref, o_ref):
    i = pl.program_id(1)
    nb = pl.num_programs(1)
    q = q_ref[...]
    k_loc = jnp.concatenate([kp_ref[...], kc_ref[...], kn_ref[...]], axis=0)
    v_loc = jnp.concatenate([vp_ref[...], vc_ref[...], vn_ref[...]], axis=0)
    kx, vx = kx_ref[...], vx_ref[...]
    r = lax.broadcasted_iota(jnp.int32, (ATT_BLOCK, 3 * ATT_BLOCK), 0)
    c = lax.broadcasted_iota(jnp.int32, (ATT_BLOCK, 3 * ATT_BLOCK), 1)
    rel = c - ATT_BLOCK - r
    in_window = jnp.logical_and(rel >= -WINDOW, rel <= WINDOW)
    in_seq = jnp.logical_and(jnp.logical_or(c >= ATT_BLOCK, i > 0),
                             jnp.logical_or(c < 2 * ATT_BLOCK, i < nb - 1))
    valid = jnp.logical_and(in_window, in_seq)
    outs = []
    for kv in range(ATT_KV_HEADS):
        qs = [_head(q, kv * ATT_GROUP + g) for g in range(ATT_GROUP)]
        outs += _sink_attention(qs, [_head(k_loc, kv), _head(kx, kv)], [_head(v_loc, kv), _head(vx, kv)],
                                sink_ref, kv, [valid, None])
    o_ref[...] = jnp.concatenate(outs, axis=1).astype(jnp.bfloat16)


def _attend_latent(qr, kr, vb, k_ctx, v_ctx, sink):
    nb = DEC_SEQ // ATT_BLOCK
    cur = lambda b, i: (b * nb + i, 0)
    prv = lambda b, i: (b * nb + jnp.maximum(i - 1, 0), 0)
    nxt = lambda b, i: (b * nb + jnp.minimum(i + 1, nb - 1), 0)
    kvs = lambda f: pl.BlockSpec((ATT_BLOCK, KV_DIM), f)
    ctx = pl.BlockSpec((None, PAST_LEN, KV_DIM), lambda b, i: (b, 0, 0))
    return pl.pallas_call(
        _attn_lat_kernel,
        grid=(DEC_BATCH, nb),
        in_specs=[
            pl.BlockSpec(memory_space=pltpu.SMEM),
            pl.BlockSpec((ATT_BLOCK, D_MODEL), cur),
            kvs(prv), kvs(cur), kvs(nxt), kvs(prv), kvs(cur), kvs(nxt), ctx, ctx,
        ],
        out_specs=pl.BlockSpec((ATT_BLOCK, D_MODEL), cur),
        out_shape=jax.ShapeDtypeStruct((N_SAMPLE, D_MODEL), jnp.bfloat16),
        compiler_params=_cparams(("arbitrary", "arbitrary")),
        name="attend_latent",
    )(sink, qr, kr, kr, kr, vb, vb, vb, k_ctx, v_ctx)


def _gconv_kernel(bg_ref, g_ref, gp_ref, gn_ref, w_ref, o_ref):
    i = pl.program_id(0)
    first = N_PROMPT // CONV_TILE
    per = DEC_SEQ // CONV_TILE
    t = (i - first) % per
    latent = i >= first
    hp = jnp.logical_and(latent, t > 0).astype(jnp.float32)
    hn = jnp.logical_and(latent, t < per - 1).astype(jnp.float32)
    window = jnp.concatenate([gp_ref[...] * hp, g_ref[...], gn_ref[...] * hn], axis=0)
    acc = None
    for k in range(SHORT_CONV):
        off = SUBLANES - SHORT_CONV // 2 + k
        term = w_ref[k:k + 1, :] * window[off:off + CONV_TILE, :]
        acc = term if acc is None else acc + term
    o_ref[...] = (bg_ref[...] * acc).astype(jnp.bfloat16)


def _gated_conv(bg, g, conv_w):
    per = CONV_TILE // SUBLANES
    last = N_TOK // SUBLANES - 1
    cw = jnp.pad(conv_w, ((0, SUBLANES - SHORT_CONV), (0, 0)))
    return pl.pallas_call(
        _gconv_kernel,
        grid=(N_TOK // CONV_TILE,),
        in_specs=[
            pl.BlockSpec((CONV_TILE, D_MODEL), lambda i: (i, 0)),
            pl.BlockSpec((CONV_TILE, D_MODEL), lambda i: (i, 0)),
            pl.BlockSpec((SUBLANES, D_MODEL), lambda i: (jnp.maximum(i * per - 1, 0), 0)),
            pl.BlockSpec((SUBLANES, D_MODEL), lambda i: (jnp.minimum(i * per + per, last), 0)),
            pl.BlockSpec((SUBLANES, D_MODEL), lambda i: (0, 0)),
        ],
        out_specs=pl.BlockSpec((CONV_TILE, D_MODEL), lambda i: (i, 0)),
        out_shape=jax.ShapeDtypeStruct((N_TOK, D_MODEL), jnp.bfloat16),
        compiler_params=_cparams(("arbitrary",)),
        name="gated_conv",
    )(bg, g, g, g, cw)


def kernel(x_prompt, x_sample, cache_k, cache_v, state_ssm, c, c_ctx,
           w_mod, b_mod, ln_w, ln_b,
           w_in_a, conv_w_a, conv_b_a, a_log, dt_bias, d_skip, ssm_norm_w, attn_sink, w_out_a,
           w_in_c, conv_w_c, w_out_c,
           w_router, b_router, w_gate, b_gate, w_lin, b_lin, w_down, b_down):
    bf16 = jnp.bfloat16
    x = jnp.concatenate([x_prompt.reshape(N_PROMPT, D_MODEL), x_sample.reshape(N_SAMPLE, D_MODEL)], axis=0)

    cond = jnp.concatenate([c_ctx[None, :], c, jnp.zeros((COND_PAD - N_COND, D_MODEL), jnp.float32)], axis=0)
    mod_all = _modulation(cond, w_mod, b_mod).reshape(DEPTH, COND_PAD, 6, D_MODEL)
    mod_all = jnp.pad(mod_all, ((0, 0), (0, 0), (0, MOD_ROWS - 6), (0, 0)))

    mod = mod_all[0]
    wa = w_in_a[0].astype(bf16)
    e = np.cumsum((0, SSM_INNER, SSM_CONV_DIM, 2 * SSM_HEADS, D_MODEL, KV_DIM, KV_DIM))
    w_dt = jnp.pad(wa[:, e[2]:e[3]], ((0, 0), (0, DT_PAD - 2 * SSM_HEADS)))
    w_parts = [wa[:, e[0]:e[1]], wa[:, e[1]:e[2]], w_dt, wa[:, e[3]:e[4]], wa[:, e[4]:e[5]], wa[:, e[5]:e[6]]]
    z, xbc_raw, dt_raw, q, k, v = _inproj(x, mod, w_parts)

    state_in = state_ssm[:, 0].reshape(DEC_BATCH, 2, SSM_INNER, D_STATE)
    y_ssm, h_f, h_b = _ssd_mixer(z, xbc_raw, dt_raw, conv_w_a[0], conv_b_a[0], a_log[0], dt_bias[0],
                                 d_skip[0], ssm_norm_w[0], state_in)

    sink = attn_sink[0]
    att_p = _attend_context(q, k, v, sink)
    qr, kr, vb = _rope_latent(q, k, v)
    k_ctx = cache_k[:, 0].reshape(DEC_BATCH, PAST_LEN, KV_DIM).astype(bf16)
    v_ctx = cache_v[:, 0].reshape(DEC_BATCH, PAST_LEN, KV_DIM).astype(bf16)
    att_s = _attend_latent(qr, kr, vb, k_ctx, v_ctx, sink)
    y_att = jnp.concatenate([att_p, att_s], axis=0)

    wo = w_out_a[0].astype(bf16)
    ln = jnp.stack([ln_w[0, 0], ln_b[0, 0]])
    x1, u2, *routing = _outproj([y_ssm, y_att], [wo[:SSM_INNER], wo[SSM_INNER:]], x, mod, ln,
                                w_router[0], b_router[0])
    x = _moe_layer(0, x1, u2, routing, mod, jnp.stack([ln_w[0, 1], ln_b[0, 1]]),
                   w_gate, b_gate, w_lin, b_lin, w_down, b_down)

    mod = mod_all[1]
    wc = w_in_c[0].astype(bf16)
    bg, g = _inproj(x, mod, [wc[:, :D_MODEL], wc[:, D_MODEL:2 * D_MODEL], wc[:, 2 * D_MODEL:]], gate_product=True)
    y_c = _gated_conv(bg, g, conv_w_c[0])
    ln = jnp.stack([ln_w[1, 0], ln_b[1, 0]])
    x1, u2, *routing = _outproj([y_c], [w_out_c[0].astype(bf16)], x, mod, ln, w_router[1], b_router[1])
    x = _moe_layer(1, x1, u2, routing, mod, jnp.stack([ln_w[1, 1], ln_b[1, 1]]),
                   w_gate, b_gate, w_lin, b_lin, w_down, b_down)

    y_prompt = x[:N_PROMPT].reshape(BATCH, SEQ, D_MODEL)
    y_sample = x[N_PROMPT:].reshape(DEC_BATCH, DEC_SEQ, D_MODEL)
    new_k = k[:N_PROMPT].reshape(BATCH, 1, SEQ, ATT_KV_HEADS, ATT_HEAD_DIM)
    new_v = v[:N_PROMPT].reshape(BATCH, 1, SEQ, ATT_KV_HEADS, ATT_HEAD_DIM)
    new_state = jnp.stack([h_f, h_b], axis=1).reshape(BATCH, 1, 2, SSM_HEADS, SSM_HEAD_DIM, D_STATE)
    return (y_prompt, y_sample, new_k, new_v, new_state)
```

```python
import functools

import numpy as np
import jax
import jax.numpy as jnp
from jax import lax
from jax.experimental import pallas as pl
from jax.experimental.pallas import tpu as pltpu

D_MODEL = 1024
BATCH = 32
SEQ = 256
DEPTH = 2
DEC_BATCH = 4
DEC_SEQ = 1024
PAST_LEN = 512
GRID_W = 64
SSM_HEAD_DIM = 64
SSM_INNER = D_MODEL
SSM_HEADS = SSM_INNER // SSM_HEAD_DIM
SSM_GROUPS = 2
SSM_HEADS_PER_GROUP = SSM_HEADS // SSM_GROUPS
D_STATE = 128
SSM_CONV = 5
SSM_CHUNK = 128
SSM_BC = SSM_GROUPS * D_STATE
SSM_CONV_DIM = SSM_INNER + 2 * SSM_BC
ATT_HEAD_DIM = 64
ATT_HEADS = D_MODEL // ATT_HEAD_DIM
ATT_KV_HEADS = 4
ATT_GROUP = ATT_HEADS // ATT_KV_HEADS
WINDOW = 128
ATT_BLOCK = 128
ATT_SCALE = ATT_HEAD_DIM ** -0.5
ROPE_BASE = 10000.0
ROPE_HALF = ATT_HEAD_DIM // 2
ROPE_QUARTER = ATT_HEAD_DIM // 4
SHORT_CONV = 3
N_EXPERTS = 32
TOP_K = 4
D_EXPERT = D_MODEL
SWIGLU_ALPHA = 1.702
SWIGLU_LIMIT = 7.0
N_EVEN = (DEPTH + 1) // 2
DEEPNORM_ALPHA = (2 * DEPTH) ** 0.25
LN_EPS = 1e-5
RMS_EPS = 1e-5
KV_DIM = ATT_KV_HEADS * ATT_HEAD_DIM

N_PROMPT = BATCH * SEQ
N_SAMPLE = DEC_BATCH * DEC_SEQ
N_TOK = N_PROMPT + N_SAMPLE
N_COND = 1 + DEC_BATCH
SUBLANES = 8
LANES = 128
COND_PAD = SUBLANES
MOD_ROWS = SUBLANES
DT_PAD = LANES

TOK_TILE = 512
CONV_TILE = 256
MOE_TILE = 256
MOE_ROWS = N_TOK * TOP_K + N_EXPERTS * MOE_TILE
MOE_TILES = MOE_ROWS // MOE_TILE
VMEM_LIMIT = 56 * 1024 * 1024
NEG_BIG = -1e30
ROW_TILES = D_MODEL // LANES

assert N_PROMPT % TOK_TILE == 0 and DEC_SEQ % TOK_TILE == 0
assert SEQ % CONV_TILE == 0 and DEC_SEQ % CONV_TILE == 0

_NT = (((1,), (1,)), ((), ()))
_TN = (((0,), (0,)), ((), ()))


def _cparams(sem):
    return pltpu.CompilerParams(dimension_semantics=sem, vmem_limit_bytes=VMEM_LIMIT)


def _cond_row(i):
    first = N_PROMPT // TOK_TILE
    per = DEC_SEQ // TOK_TILE
    return jnp.where(i < first, 0, 1 + (i - first) // per)


def _bf16_dot(a, b):
    return jnp.dot(a.astype(jnp.bfloat16), b.astype(jnp.bfloat16), preferred_element_type=jnp.float32)


def _mod_kernel(c_ref, w_ref, b_ref, o_ref):
    c = c_ref[...]
    s = c * jax.nn.sigmoid(c)
    o_ref[...] = _bf16_dot(s, w_ref[...]) + b_ref[...]


def _modulation(cond, w_mod, b_mod):
    tn = 1536
    return pl.pallas_call(
        _mod_kernel,
        grid=(DEPTH, 6 * D_MODEL // tn),
        in_specs=[
            pl.BlockSpec((COND_PAD, D_MODEL), lambda l, n: (0, 0)),
            pl.BlockSpec((None, D_MODEL, tn), lambda l, n: (l, 0, n)),
            pl.BlockSpec((None, 1, tn), lambda l, n: (l, 0, n)),
        ],
        out_specs=pl.BlockSpec((None, COND_PAD, tn), lambda l, n: (l, 0, n)),
        out_shape=jax.ShapeDtypeStruct((DEPTH, COND_PAD, 6 * D_MODEL), jnp.float32),
        compiler_params=_cparams(("arbitrary", "arbitrary")),
        name="modulation",
    )(cond, w_mod, b_mod.reshape(DEPTH, 1, 6 * D_MODEL))


def _inproj_kernel(x_ref, mod_ref, *refs, gate_product):
    n = len(refs) // 2 + (1 if gate_product else 0)
    w_refs, o_refs = refs[:n], refs[n:]
    u = x_ref[...] * (1.0 + mod_ref[1:2, :]) + mod_ref[0:1, :]
    ub = u.astype(jnp.bfloat16)
    outs = [jnp.dot(ub, w_ref[...], preferred_element_type=jnp.float32) for w_ref in w_refs]
    if gate_product:
        outs = outs[:-2] + [outs[-2] * outs[-1]]
    for o_ref, o in zip(o_refs, outs):
        o_ref[...] = o


def _inproj(x, mod, weights, gate_product=False):
    out_w = [w.shape[1] for w in weights]
    if gate_product:
        out_w = out_w[:-1]
    return pl.pallas_call(
        functools.partial(_inproj_kernel, gate_product=gate_product),
        grid=(N_TOK // TOK_TILE,),
        in_specs=[
            pl.BlockSpec((TOK_TILE, D_MODEL), lambda i: (i, 0)),
            pl.BlockSpec((None, MOD_ROWS, D_MODEL), lambda i: (_cond_row(i), 0, 0)),
        ] + [pl.BlockSpec(w.shape, lambda i: (0, 0)) for w in weights],
        out_specs=[pl.BlockSpec((TOK_TILE, n), lambda i: (i, 0)) for n in out_w],
        out_shape=[jax.ShapeDtypeStruct((N_TOK, n), jnp.float32) for n in out_w],
        compiler_params=_cparams(("arbitrary",)),
        name="inproj",
    )(x, mod, *weights)


def _layer_norm(v, w, b):
    mu = jnp.mean(v, axis=-1, keepdims=True)
    d = v - mu
    var = jnp.mean(d * d, axis=-1, keepdims=True)
    return d * lax.rsqrt(var + LN_EPS) * w + b


def _outproj_kernel(*refs, n_in):
    y_refs = refs[:n_in]
    w_refs = refs[n_in:2 * n_in]
    (x_ref, mod_ref, ln_ref, wr_ref, br_ref,
     x1_ref, u2_ref, idx_ref, gate_ref, rank_ref, cnt_ref, run_ref) = refs[2 * n_in:]
    f32 = jnp.float32

    @pl.when(pl.program_id(0) == 0)
    def _():
        run_ref[...] = jnp.zeros_like(run_ref)

    mix = _bf16_dot(y_refs[0][...], w_refs[0][...])
    for y_ref, w_ref in zip(y_refs[1:], w_refs[1:]):
        mix = mix + _bf16_dot(y_ref[...], w_ref[...])
    x1 = _layer_norm(DEEPNORM_ALPHA * x_ref[...] + mod_ref[2:3, :] * mix, ln_ref[0:1, :], ln_ref[1:2, :])
    x1_ref[...] = x1
    u2 = x1 * (1.0 + mod_ref[4:5, :]) + mod_ref[3:4, :]
    u2_ref[...] = u2
    logits = jnp.dot(u2, wr_ref[...], preferred_element_type=f32, precision=lax.Precision.HIGHEST) + br_ref[...]

    lane = lax.broadcasted_iota(jnp.int32, (TOK_TILE, N_EXPERTS), 1)
    work = logits
    vals, sels, idxs = [], [], []
    for _ in range(TOP_K):
        m = jnp.max(work, axis=-1, keepdims=True)
        idx = jnp.min(jnp.where(work == m, lane, N_EXPERTS), axis=-1, keepdims=True)
        sel = lane == idx
        vals.append(m)
        idxs.append(idx)
        sels.append(sel)
        work = jnp.where(sel, -jnp.inf, work)
    exps = [jnp.exp(v - vals[0]) for v in vals]
    den = exps[0]
    for e in exps[1:]:
        den = den + e
    onehot = sels[0].astype(f32)
    for sel in sels[1:]:
        onehot = onehot + sel.astype(f32)
    r = lax.broadcasted_iota(jnp.int32, (TOK_TILE, TOK_TILE), 0)
    c = lax.broadcasted_iota(jnp.int32, (TOK_TILE, TOK_TILE), 1)
    before = jnp.dot((r > c).astype(jnp.bfloat16), onehot.astype(jnp.bfloat16), preferred_element_type=f32)
    rank_all = before + run_ref[...]
    run = run_ref[...] + jnp.sum(onehot, axis=0, keepdims=True)
    run_ref[...] = run
    cnt_ref[...] = run.astype(jnp.int32)
    col = lax.broadcasted_iota(jnp.int32, (TOK_TILE, TOP_K), 1)
    idx_o = jnp.zeros((TOK_TILE, TOP_K), jnp.int32)
    gate_o = jnp.zeros((TOK_TILE, TOP_K), f32)
    rank_o = jnp.zeros((TOK_TILE, TOP_K), f32)
    for k in range(TOP_K):
        rank_k = jnp.sum(jnp.where(sels[k], rank_all, 0.0), axis=-1, keepdims=True)
        idx_o = jnp.where(col == k, idxs[k], idx_o)
        gate_o = jnp.where(col == k, exps[k] / den, gate_o)
        rank_o = jnp.where(col == k, rank_k, rank_o)
    idx_ref[...] = idx_o
    gate_ref[...] = gate_o
    rank_ref[...] = rank_o.astype(jnp.int32)


def _outproj(ys, ws, x, mod, ln, w_router, b_router):
    n_in = len(ys)
    return pl.pallas_call(
        functools.partial(_outproj_kernel, n_in=n_in),
        grid=(N_TOK // TOK_TILE,),
        in_specs=[pl.BlockSpec((TOK_TILE, y.shape[1]), lambda i: (i, 0)) for y in ys]
        + [pl.BlockSpec(w.shape, lambda i: (0, 0)) for w in ws]
        + [
            pl.BlockSpec((TOK_TILE, D_MODEL), lambda i: (i, 0)),
            pl.BlockSpec((None, MOD_ROWS, D_MODEL), lambda i: (_cond_row(i), 0, 0)),
            pl.BlockSpec((2, D_MODEL), lambda i: (0, 0)),
            pl.BlockSpec((D_MODEL, N_EXPERTS), lambda i: (0, 0)),
            pl.BlockSpec((1, N_EXPERTS), lambda i: (0, 0)),
        ],
        out_specs=[
            pl.BlockSpec((TOK_TILE, D_MODEL), lambda i: (i, 0)),
            pl.BlockSpec((TOK_TILE, D_MODEL), lambda i: (i, 0)),
            pl.BlockSpec((TOK_TILE, TOP_K), lambda i: (i, 0)),
            pl.BlockSpec((TOK_TILE, TOP_K), lambda i: (i, 0)),
            pl.BlockSpec((TOK_TILE, TOP_K), lambda i: (i, 0)),
            pl.BlockSpec((1, N_EXPERTS), lambda i: (0, 0)),
        ],
        out_shape=[
            jax.ShapeDtypeStruct((N_TOK, D_MODEL), jnp.float32),
            jax.ShapeDtypeStruct((N_TOK, D_MODEL), jnp.float32),
            jax.ShapeDtypeStruct((N_TOK, TOP_K), jnp.int32),
            jax.ShapeDtypeStruct((N_TOK, TOP_K), jnp.float32),
            jax.ShapeDtypeStruct((N_TOK, TOP_K), jnp.int32),
            jax.ShapeDtypeStruct((1, N_EXPERTS), jnp.int32),
        ],
        scratch_shapes=[pltpu.VMEM((1, N_EXPERTS), jnp.float32)],
        compiler_params=_cparams(("arbitrary",)),
        name="outproj_ln_router",
    )(*ys, *ws, x, mod, ln, w_router, b_router.reshape(1, N_EXPERTS))


def _moe_kernel(te_ref, nv_ref, src0_ref, last_ref, stok_ref, x_hbm,
                wg_ref, bg_ref, wl_ref, bl_ref, wd_ref, bd_ref, o_ref,
                wg_s, wl_s, wd_s, xbuf, sem):
    i = pl.program_id(0)
    nv = nv_ref[0]
    valid = i < nv
    prev = te_ref[jnp.maximum(i - 1, 0)]
    new_expert = jnp.logical_or(i == 0, te_ref[i] != prev)

    def start_gather(tile, buf):
        base, last = src0_ref[tile], last_ref[tile]
        for r in range(MOE_TILE):
            tok = stok_ref[jnp.minimum(base + r, last)]
            pltpu.make_async_copy(x_hbm.at[tok], xbuf.at[buf, r], sem.at[buf]).start()

    def wait_gather(buf):
        pltpu.make_async_copy(x_hbm.at[pl.ds(0, MOE_TILE)], xbuf.at[buf], sem.at[buf]).wait()

    @pl.when(i == 0)
    def _():
        start_gather(0, 0)

    @pl.when(jnp.logical_and(valid, new_expert))
    def _():
        wg_s[...] = wg_ref[...].astype(jnp.bfloat16)
        wl_s[...] = wl_ref[...].astype(jnp.bfloat16)
        wd_s[...] = wd_ref[...].astype(jnp.bfloat16)

    for buf in range(2):
        @pl.when(jnp.logical_and(valid, i % 2 == buf))
        def _():
            wait_gather(buf)
            start_gather(jnp.minimum(i + 1, nv - 1), 1 - buf)
            x = jnp.concatenate([xbuf[buf, :, s, :] for s in range(ROW_TILES)], axis=1).astype(jnp.bfloat16)
            hg = jnp.dot(x, wg_s[...], preferred_element_type=jnp.float32) + bg_ref[...]
            hl = jnp.dot(x, wl_s[...], preferred_element_type=jnp.float32) + bl_ref[...]
            hg = jnp.minimum(hg, SWIGLU_LIMIT)
            hl = jnp.clip(hl, -SWIGLU_LIMIT, SWIGLU_LIMIT)
            h = hg * jax.nn.sigmoid(SWIGLU_ALPHA * hg) * (hl + 1.0)
            o_ref[...] = (jnp.dot(h.astype(jnp.bfloat16), wd_s[...], preferred_element_type=jnp.float32)
                          + bd_ref[...])

        @pl.when(jnp.logical_and(i == nv - 1, i % 2 == buf))
        def _():
            wait_gather(1 - buf)

    @pl.when(jnp.logical_not(valid))
    def _():
        o_ref[...] = jnp.zeros_like(o_ref)


def _moe_grouped(layer, tile_expert, n_valid, src0, src_last, sorted_tok, x, w_gate, b_gate, w_lin, b_lin,
                 w_down, b_down):
    wspec = pl.BlockSpec((None, None, D_MODEL, D_EXPERT), lambda i, te, *_: (layer, te[i], 0, 0))
    bspec = pl.BlockSpec((None, None, 1, D_EXPERT), lambda i, te, *_: (layer, te[i], 0, 0))
    grid_spec = pltpu.PrefetchScalarGridSpec(
        num_scalar_prefetch=5,
        grid=(MOE_TILES,),
        in_specs=[pl.BlockSpec(memory_space=pl.ANY), wspec, bspec, wspec, bspec, wspec, bspec],
        out_specs=pl.BlockSpec((MOE_TILE, D_MODEL), lambda i, *_: (i, 0)),
        scratch_shapes=[pltpu.VMEM((D_MODEL, D_EXPERT), jnp.bfloat16)] * 3
        + [pltpu.VMEM((2, MOE_TILE, ROW_TILES, LANES), jnp.float32), pltpu.SemaphoreType.DMA((2,))],
    )
    bshape = (DEPTH, N_EXPERTS, 1, D_EXPERT)
    return pl.pallas_call(
        _moe_kernel,
        grid_spec=grid_spec,
        out_shape=jax.ShapeDtypeStruct((MOE_ROWS, D_MODEL), jnp.float32),
        compiler_params=_cparams(("arbitrary",)),
        name="moe_grouped",
    )(tile_expert, n_valid, src0, src_last, sorted_tok, x.reshape(N_TOK, ROW_TILES, LANES), w_gate,
      b_gate.reshape(bshape),
      w_lin, b_lin.reshape(bshape), w_down, b_down.reshape(bshape))


def _combine_kernel(y_ref, g_ref, x_ref, mod_ref, ln_ref, o_ref):
    g = g_ref[...]
    ff = g[:, 0:1] * y_ref[0]
    for k in range(1, TOP_K):
        ff = ff + g[:, k:k + 1] * y_ref[k]
    o_ref[...] = _layer_norm(DEEPNORM_ALPHA * x_ref[...] + mod_ref[5:6, :] * ff, ln_ref[0:1, :], ln_ref[1:2, :])


def _combine(yk, gates, x, mod, ln):
    return pl.pallas_call(
        _combine_kernel,
        grid=(N_TOK // TOK_TILE,),
        in_specs=[
            pl.BlockSpec((TOP_K, TOK_TILE, D_MODEL), lambda i: (0, i, 0)),
            pl.BlockSpec((TOK_TILE, TOP_K), lambda i: (i, 0)),
            pl.BlockSpec((TOK_TILE, D_MODEL), lambda i: (i, 0)),
            pl.BlockSpec((None, MOD_ROWS, D_MODEL), lambda i: (_cond_row(i), 0, 0)),
            pl.BlockSpec((2, D_MODEL), lambda i: (0, 0)),
        ],
        out_specs=pl.BlockSpec((TOK_TILE, D_MODEL), lambda i: (i, 0)),
        out_shape=jax.ShapeDtypeStruct((N_TOK, D_MODEL), jnp.float32),
        compiler_params=_cparams(("arbitrary",)),
        name="combine_ln",
    )(yk, gates, x, mod, ln)


def _route(top_idx, rank, counts):
    counts = counts.reshape(N_EXPERTS)
    padded = (counts + MOE_TILE - 1) // MOE_TILE * MOE_TILE
    pend = jnp.cumsum(padded)
    pstart = pend - padded
    start = jnp.cumsum(counts) - counts
    experts = jnp.arange(N_EXPERTS, dtype=jnp.int32)
    lookup = lambda table, idx: jnp.sum(jnp.where(idx[..., None] == experts, table, 0), axis=-1)
    slot = lookup(pstart, top_idx) + rank
    key = lookup(start, top_idx) + rank
    tok = jnp.broadcast_to(jnp.arange(N_TOK, dtype=jnp.int32)[:, None], (N_TOK, TOP_K))
    _, sorted_tok = lax.sort((key.reshape(-1), tok.reshape(-1)), num_keys=1)
    n_valid = (pend[-1] // MOE_TILE).astype(jnp.int32)
    tile_start = jnp.arange(MOE_TILES, dtype=jnp.int32) * MOE_TILE
    tile_expert = jnp.sum((tile_start[:, None] >= pend[None, :]).astype(jnp.int32), axis=1)
    tile_expert = jnp.where(tile_start < pend[-1], tile_expert, tile_expert[n_valid - 1])
    src0 = tile_start - lookup(pstart, tile_expert) + lookup(start, tile_expert)
    src_last = lookup(start, tile_expert) + lookup(counts, tile_expert) - 1
    return slot, sorted_tok, tile_expert, n_valid.reshape(1), src0, src_last


def _moe_layer(i, x1, u2, routing, mod, ln, w_gate, b_gate, w_lin, b_lin, w_down, b_down):
    top_idx, gates, rank, counts = routing
    slot, sorted_tok, tile_expert, n_valid, src0, src_last = _route(top_idx, rank, counts)
    y = _moe_grouped(i, tile_expert, n_valid, src0, src_last, sorted_tok, u2,
                     w_gate, b_gate, w_lin, b_lin, w_down, b_down)
    yk = jnp.take(y, slot.T, axis=0, mode="clip")
    return _combine(yk, gates, x1, mod, ln)


_F_ROW, _F_FIRST, _F_LAST, _F_PREV8, _F_HASPREV, _F_NEXT8, _F_HASNEXT, _F_H0, _F_USEH0, _F_HOUT, _F_WRITEH = range(11)
_N_FIELDS = 11
_SSD_STEPS = N_TOK // SSM_CHUNK


def _ssd_table(reverse):
    rows = []
    for prompt, nseq, slen, base in ((True, BATCH, SEQ, 0), (False, DEC_BATCH, DEC_SEQ, N_PROMPT)):
        nc = slen // SSM_CHUNK
        for s in range(nseq):
            for c in range(nc):
                row = (base + s * slen) // SSM_CHUNK + c
                first, last = (c == nc - 1, c == 0) if reverse else (c == 0, c == nc - 1)
                per = SSM_CHUNK // SUBLANES
                rows.append([
                    row, int(first), int(last),
                    max(row * per - 1, 0), int(c > 0),
                    min(row * per + per, N_TOK // SUBLANES - 1), int(c < nc - 1),
                    0 if prompt else s, int(not prompt),
                    s if prompt else BATCH - 1, int(prompt and last),
                ])
    if reverse:
        rows = rows[::-1]
    return np.asarray(rows, np.int32).T.reshape(-1)


def _fld(tbl, f, s):
    return tbl[f * _SSD_STEPS + s]


def _exact_expand(a, e_ref, passes):
    k = a.shape[1]
    pieces, rem = [], a
    for _ in range(passes):
        piece = rem.astype(jnp.bfloat16)
        rem = rem - piece.astype(jnp.float32)
        pieces.append(piece)
    out = None
    for i in range(0, passes, 2):
        pair = pieces[i:i + 2]
        lhs = pair[0] if len(pair) == 1 else jnp.concatenate(pair, axis=1)
        t = jnp.dot(lhs, e_ref[0:k * len(pair), :], preferred_element_type=jnp.float32)
        out = t if out is None else out + t
    return out


def _ssd_chunk(xs, bmat, cmat, dt_raw, dtb_row, alog_row, ex_ref, eb_ref, s_ref, reverse):
    f32, bf16 = jnp.float32, jnp.bfloat16
    base = SSM_HEADS if reverse else 0
    lane = lax.broadcasted_iota(jnp.int32, (1, DT_PAD), 1)
    head_cols = jnp.logical_and(lane >= base, lane < base + SSM_HEADS)
    xr = dt_raw + dtb_row
    dt = jnp.maximum(xr, 0.0) + jnp.log1p(jnp.exp(-jnp.abs(xr)))
    a_row = jnp.where(head_cols, -jnp.exp(alog_row), 0.0)
    d_a = dt * a_row
    r = lax.broadcasted_iota(jnp.int32, (SSM_CHUNK, SSM_CHUNK), 0)
    c = lax.broadcasted_iota(jnp.int32, (SSM_CHUNK, SSM_CHUNK), 1)
    tri = (r <= c) if reverse else (r >= c)
    cum = jnp.dot(tri.astype(f32), d_a, preferred_element_type=f32, precision=lax.Precision.HIGHEST)
    cum_t = cum.T
    total = cum[0:1, :] if reverse else cum[SSM_CHUNK - 1:SSM_CHUNK, :]
    stacked = jnp.concatenate([dt, jnp.exp(cum), dt * jnp.exp(total - cum),
                               jnp.broadcast_to(jnp.exp(total), (SUBLANES, DT_PAD))], axis=0)
    stacked_x = _exact_expand(stacked, ex_ref, 2)
    dt_x = stacked_x[0:SSM_CHUNK]
    e_in_x = stacked_x[SSM_CHUNK:2 * SSM_CHUNK]
    w_end_x = stacked_x[2 * SSM_CHUNK:3 * SSM_CHUNK]
    dec_x = stacked_x[3 * SSM_CHUNK:3 * SSM_CHUNK + 1]
    cum_b = _exact_expand(cum, eb_ref, 3)
    xdt = xs * dt_x
    xw = (xs * w_end_x).astype(bf16)
    s_in = s_ref[...]
    s_bf = s_in.astype(bf16)
    lo_half = lax.broadcasted_iota(jnp.int32, (1, 2 * SSM_HEAD_DIM), 1) < SSM_HEAD_DIM
    gw = SSM_HEADS_PER_GROUP * SSM_HEAD_DIM
    pieces = []
    for g in range(SSM_GROUPS):
        bg = bmat[:, g * D_STATE:(g + 1) * D_STATE]
        cg = cmat[:, g * D_STATE:(g + 1) * D_STATE]
        gcols = slice(g * gw, (g + 1) * gw)
        cb = lax.dot_general(cg, bg, _NT, preferred_element_type=f32)
        y_off = jnp.dot(cg, s_bf[:, gcols], preferred_element_type=f32) * e_in_x[:, gcols]
        upd = lax.dot_general(bg, xw[:, gcols], _TN, preferred_element_type=f32)
        s_ref[:, gcols] = s_in[:, gcols] * dec_x[:, gcols] + upd
        for q in range(SSM_HEADS_PER_GROUP // 2):
            h0 = g * SSM_HEADS_PER_GROUP + 2 * q
            pcols = slice(h0 * SSM_HEAD_DIM, (h0 + 2) * SSM_HEAD_DIM)
            xp = xdt[:, pcols]
            bd = jnp.concatenate([jnp.where(lo_half, xp, 0.0), jnp.where(lo_half, 0.0, xp)], axis=0).astype(bf16)
            ms = []
            for h in (h0, h0 + 1):
                ch = base + h
                seg = cum_b[:, h * SSM_CHUNK:(h + 1) * SSM_CHUNK] - cum_t[ch:ch + 1, :]
                decay = jnp.exp(jnp.where(tri, seg, NEG_BIG))
                ms.append((cb * decay).astype(bf16))
            y_diag = jnp.dot(jnp.concatenate(ms, axis=1), bd, preferred_element_type=f32)
            pieces.append(y_diag + y_off[:, 2 * q * SSM_HEAD_DIM:(2 * q + 2) * SSM_HEAD_DIM])
    return jnp.concatenate(pieces, axis=1)


def _ssd_init_state(tbl, step, h0_ref, s_ref):
    @pl.when(_fld(tbl, _F_FIRST, step) == 1)
    def _():
        use = _fld(tbl, _F_USEH0, step) == 1
        s_ref[...] = jnp.where(use, h0_ref[...], 0.0).T


def _ssd_write_state(tbl, step, hout_ref, s_ref):
    @pl.when(_fld(tbl, _F_WRITEH, step) == 1)
    def _():
        hout_ref[...] = s_ref[...].T


def _ssd_fwd_kernel(tbl, xc_ref, xp_ref, xn_ref, dt_ref, cw_ref, cbias_ref, dtb_ref, alog_ref, ex_ref, eb_ref,
                    h0_ref, xs_ref, bc_ref, yf_ref, hout_ref, s_ref):
    step = pl.program_id(0)
    _ssd_init_state(tbl, step, h0_ref, s_ref)
    hp = (_fld(tbl, _F_HASPREV, step) == 1).astype(jnp.float32)
    hn = (_fld(tbl, _F_HASNEXT, step) == 1).astype(jnp.float32)
    window = jnp.concatenate([xp_ref[...] * hp, xc_ref[...], xn_ref[...] * hn], axis=0)
    acc = cbias_ref[...] + cw_ref[0:1, :] * window[SUBLANES - 2:SUBLANES - 2 + SSM_CHUNK, :]
    for k in range(1, SSM_CONV):
        off = SUBLANES - SSM_CONV // 2 + k
        acc = acc + cw_ref[k:k + 1, :] * window[off:off + SSM_CHUNK, :]
    xbc = acc * jax.nn.sigmoid(acc)
    xs = xbc[:, :SSM_INNER]
    bc = xbc[:, SSM_INNER:].astype(jnp.bfloat16)
    xs_ref[...] = xs
    bc_ref[...] = bc
    yf_ref[...] = _ssd_chunk(xs, bc[:, :SSM_BC], bc[:, SSM_BC:], dt_ref[...], dtb_ref[...], alog_ref[...],
                             ex_ref, eb_ref, s_ref, reverse=False)
    _ssd_write_state(tbl, step, hout_ref, s_ref)


def _ssd_bwd_kernel(tbl, xs_ref, bc_ref, dt_ref, yf_ref, z_ref, dtb_ref, alog_ref, dskip_ref, nw_ref,
                    ex_ref, eb_ref, h0_ref, y_ref, hout_ref, s_ref):
    step = pl.program_id(0)
    _ssd_init_state(tbl, step, h0_ref, s_ref)
    xs = xs_ref[...]
    bc = bc_ref[...]
    yb = _ssd_chunk(xs, bc[:, :SSM_BC], bc[:, SSM_BC:], dt_ref[...], dtb_ref[...], alog_ref[...],
                    ex_ref, eb_ref, s_ref, reverse=True)
    z = z_ref[...]
    hg = (yf_ref[...] + yb + dskip_ref[...] * xs) * (z * jax.nn.sigmoid(z))
    gw = SSM_INNER // SSM_GROUPS
    outs = []
    for g in range(SSM_GROUPS):
        hgg = hg[:, g * gw:(g + 1) * gw]
        outs.append(hgg * lax.rsqrt(jnp.mean(hgg * hgg, axis=-1, keepdims=True) + RMS_EPS))
    y_ref[...] = (jnp.concatenate(outs, axis=1) * nw_ref[...]).astype(jnp.bfloat16)
    _ssd_write_state(tbl, step, hout_ref, s_ref)


def _ssd_mixer(z, xbc_raw, dt_raw, conv_w, conv_b, a_log, dt_bias, d_skip, norm_w, state_in):
    f32 = jnp.float32
    row = lambda f: (lambda s, tbl: (_fld(tbl, f, s), 0))
    chunk_spec = lambda w: pl.BlockSpec((SSM_CHUNK, w), row(_F_ROW))
    const_spec = lambda shape: pl.BlockSpec(shape, lambda s, tbl: (0,) * len(shape))
    h0_spec = lambda d: pl.BlockSpec((None, None, SSM_INNER, D_STATE), lambda s, tbl: (_fld(tbl, _F_H0, s), d, 0, 0))
    hout_spec = pl.BlockSpec((None, SSM_INNER, D_STATE), lambda s, tbl: (_fld(tbl, _F_HOUT, s), 0, 0))
    hout_shape = jax.ShapeDtypeStruct((BATCH, SSM_INNER, D_STATE), f32)
    state_scratch = [pltpu.VMEM((D_STATE, SSM_INNER), f32)]

    def expanders(reverse):
        base = SSM_HEADS if reverse else 0
        ex = np.zeros((DT_PAD, SSM_INNER), np.float32)
        eb = np.zeros((DT_PAD, SSM_HEADS * SSM_CHUNK), np.float32)
        for h in range(SSM_HEADS):
            ex[base + h, h * SSM_HEAD_DIM:(h + 1) * SSM_HEAD_DIM] = 1.0
            eb[base + h, h * SSM_CHUNK:(h + 1) * SSM_CHUNK] = 1.0
        twice = lambda e: jnp.asarray(np.concatenate([e, e], axis=0), jnp.bfloat16)
        return twice(ex), twice(eb)

    expander_specs = [const_spec((2 * DT_PAD, SSM_INNER)), const_spec((2 * DT_PAD, SSM_HEADS * SSM_CHUNK))]

    cw = jnp.pad(conv_w, ((0, SUBLANES - SSM_CONV), (0, 0)))
    cbias = conv_b.reshape(1, SSM_CONV_DIM)
    dtb = jnp.pad(dt_bias.reshape(1, 2 * SSM_HEADS), ((0, 0), (0, DT_PAD - 2 * SSM_HEADS)))
    alog = jnp.pad(a_log.reshape(1, 2 * SSM_HEADS), ((0, 0), (0, DT_PAD - 2 * SSM_HEADS)))
    dskip = jnp.repeat(d_skip, SSM_HEAD_DIM).reshape(1, SSM_INNER)
    nw = norm_w.reshape(1, SSM_INNER)

    xs, bc, yf, h_f = pl.pallas_call(
        _ssd_fwd_kernel,
        grid_spec=pltpu.PrefetchScalarGridSpec(
            num_scalar_prefetch=1,
            grid=(_SSD_STEPS,),
            in_specs=[
                chunk_spec(SSM_CONV_DIM),
                pl.BlockSpec((SUBLANES, SSM_CONV_DIM), row(_F_PREV8)),
                pl.BlockSpec((SUBLANES, SSM_CONV_DIM), row(_F_NEXT8)),
                chunk_spec(DT_PAD),
                const_spec((SUBLANES, SSM_CONV_DIM)), const_spec((1, SSM_CONV_DIM)),
                const_spec((1, DT_PAD)), const_spec((1, DT_PAD)), *expander_specs,
                h0_spec(0),
            ],
            out_specs=[chunk_spec(SSM_INNER), chunk_spec(2 * SSM_BC), chunk_spec(SSM_INNER), hout_spec],
            scratch_shapes=state_scratch,
        ),
        out_shape=[
            jax.ShapeDtypeStruct((N_TOK, SSM_INNER), f32),
            jax.ShapeDtypeStruct((N_TOK, 2 * SSM_BC), jnp.bfloat16),
            jax.ShapeDtypeStruct((N_TOK, SSM_INNER), f32),
            hout_shape,
        ],
        compiler_params=_cparams(("arbitrary",)),
        name="ssd_forward",
    )(jnp.asarray(_ssd_table(False)), xbc_raw, xbc_raw, xbc_raw, dt_raw, cw, cbias, dtb, alog,
      *expanders(False), state_in)

    y, h_b = pl.pallas_call(
        _ssd_bwd_kernel,
        grid_spec=pltpu.PrefetchScalarGridSpec(
            num_scalar_prefetch=1,
            grid=(_SSD_STEPS,),
            in_specs=[
                chunk_spec(SSM_INNER), chunk_spec(2 * SSM_BC), chunk_spec(DT_PAD), chunk_spec(SSM_INNER),
                chunk_spec(SSM_INNER),
                const_spec((1, DT_PAD)), const_spec((1, DT_PAD)),
                const_spec((1, SSM_INNER)), const_spec((1, SSM_INNER)), *expander_specs,
                h0_spec(1),
            ],
            out_specs=[chunk_spec(SSM_INNER), hout_spec],
            scratch_shapes=state_scratch,
        ),
        out_shape=[jax.ShapeDtypeStruct((N_TOK, SSM_INNER), jnp.bfloat16), hout_shape],
        compiler_params=_cparams(("arbitrary",)),
        name="ssd_backward",
    )(jnp.asarray(_ssd_table(True)), xs, bc, dt_raw, yf, z, dtb, alog, dskip, nw, *expanders(True), state_in)
    return y, h_f, h_b


def _sink_attention(q_heads, keys, vals, sink_ref, kv, masks):
    outs = []
    for g, qh in enumerate(q_heads):
        sk = sink_ref[kv * ATT_GROUP + g]
        scores = []
        for kk, mask in zip(keys, masks):
            s = lax.dot_general(qh, kk, _NT, preferred_element_type=jnp.float32) * ATT_SCALE
            scores.append(s if mask is None else jnp.where(mask, s, NEG_BIG))
        m = sk
        for s in scores:
            m = jnp.maximum(m, jnp.max(s, axis=-1, keepdims=True))
        den = jnp.exp(sk - m)
        acc = None
        for s, vv in zip(scores, vals):
            p = jnp.exp(s - m)
            den = den + jnp.sum(p, axis=-1, keepdims=True)
            pv = jnp.dot(p.astype(jnp.bfloat16), vv, preferred_element_type=jnp.float32)
            acc = pv if acc is None else acc + pv
        outs.append(acc / den)
    return outs


def _head(x, h):
    return x[:, h * ATT_HEAD_DIM:(h + 1) * ATT_HEAD_DIM]


def _attn_ctx_kernel(sink_ref, q_ref, k_ref, v_ref, o_ref):
    bf16 = jnp.bfloat16
    q = q_ref[...].astype(bf16)
    k = k_ref[...].astype(bf16)
    v = v_ref[...].astype(bf16)
    outs = []
    for kv in range(ATT_KV_HEADS):
        qs = [_head(q, kv * ATT_GROUP + g) for g in range(ATT_GROUP)]
        outs += _sink_attention(qs, [_head(k, kv)], [_head(v, kv)], sink_ref, kv, [None])
    o_ref[...] = jnp.concatenate(outs, axis=1).astype(bf16)


def _attend_context(q, k, v, sink):
    return pl.pallas_call(
        _attn_ctx_kernel,
        grid=(BATCH,),
        in_specs=[
            pl.BlockSpec(memory_space=pltpu.SMEM),
            pl.BlockSpec((SEQ, D_MODEL), lambda b: (b, 0)),
            pl.BlockSpec((SEQ, KV_DIM), lambda b: (b, 0)),
            pl.BlockSpec((SEQ, KV_DIM), lambda b: (b, 0)),
        ],
        out_specs=pl.BlockSpec((SEQ, D_MODEL), lambda b: (b, 0)),
        out_shape=jax.ShapeDtypeStruct((N_PROMPT, D_MODEL), jnp.bfloat16),
        compiler_params=_cparams(("arbitrary",)),
        name="attend_context",
    )(sink, q, k, v)


def _rope_tables(width):
    t = np.arange(DEC_SEQ)
    d = np.arange(width) % ATT_HEAD_DIM
    pos = np.where(d[None, :] < ROPE_HALF, (t // GRID_W)[:, None], (t % GRID_W)[:, None]).astype(np.float32)
    inv = (ROPE_BASE ** (-np.arange(ROPE_QUARTER, dtype=np.float32) / ROPE_QUARTER)).astype(np.float32)
    ang = pos * inv[d % ROPE_QUARTER][None, :]
    sign = np.where((d % ROPE_HALF) < ROPE_QUARTER, -1.0, 1.0).astype(np.float32)
    return jnp.asarray(np.cos(ang), jnp.float32), jnp.asarray(np.sin(ang) * sign[None, :], jnp.float32)


def _rope(x, cos, sin_signed):
    width = x.shape[1]
    lane = lax.broadcasted_iota(jnp.int32, (1, width), 1)
    first = (lane % ROPE_HALF) < ROPE_QUARTER
    partner = jnp.where(first, pltpu.roll(x, width - ROPE_QUARTER, 1), pltpu.roll(x, ROPE_QUARTER, 1))
    return x * cos + partner * sin_signed


def _rope_kernel(q_ref, k_ref, v_ref, cos_ref, sin_ref, qo_ref, ko_ref, vo_ref):
    bf16 = jnp.bfloat16
    cos, sin = cos_ref[...], sin_ref[...]
    qo_ref[...] = _rope(q_ref[...], cos, sin).astype(bf16)
    ko_ref[...] = _rope(k_ref[...], cos[:, :KV_DIM], sin[:, :KV_DIM]).astype(bf16)
    vo_ref[...] = v_ref[...].astype(bf16)


def _rope_latent(q, k, v):
    nb = DEC_SEQ // ATT_BLOCK
    off = N_PROMPT // ATT_BLOCK
    cos, sin = _rope_tables(D_MODEL)
    tok = lambda b, i: (off + b * nb + i, 0)
    out = lambda b, i: (b * nb + i, 0)
    return pl.pallas_call(
        _rope_kernel,
        grid=(DEC_BATCH, nb),
        in_specs=[
            pl.BlockSpec((ATT_BLOCK, D_MODEL), tok),
            pl.BlockSpec((ATT_BLOCK, KV_DIM), tok),
            pl.BlockSpec((ATT_BLOCK, KV_DIM), tok),
            pl.BlockSpec((ATT_BLOCK, D_MODEL), lambda b, i: (i, 0)),
            pl.BlockSpec((ATT_BLOCK, D_MODEL), lambda b, i: (i, 0)),
        ],
        out_specs=[
            pl.BlockSpec((ATT_BLOCK, D_MODEL), out),
            pl.BlockSpec((ATT_BLOCK, KV_DIM), out),
            pl.BlockSpec((ATT_BLOCK, KV_DIM), out),
        ],
        out_shape=[
            jax.ShapeDtypeStruct((N_SAMPLE, D_MODEL), jnp.bfloat16),
            jax.ShapeDtypeStruct((N_SAMPLE, KV_DIM), jnp.bfloat16),
            jax.ShapeDtypeStruct((N_SAMPLE, KV_DIM), jnp.bfloat16),
        ],
        compiler_params=_cparams(("arbitrary", "arbitrary")),
        name="rope_latent",
    )(q, k, v, cos, sin)


def _attn_lat_kernel(sink_ref, q_ref, kp_ref, kc_ref, kn_ref, vp_ref, vc_ref, vn_ref, kx_ref, vx_ref, o_ref):
    i = pl.program_id(1)
    nb = pl.num_programs(1)
    q = q_ref[...]
    k_loc = jnp.concatenate([kp_ref[...], kc_ref[...], kn_ref[...]], axis=0)
    v_loc = jnp.concatenate([vp_ref[...], vc_ref[...], vn_ref[...]], axis=0)
    kx, vx = kx_ref[...], vx_ref[...]
    r = lax.broadcasted_iota(jnp.int32, (ATT_BLOCK, 3 * ATT_BLOCK), 0)
    c = lax.broadcasted_iota(jnp.int32, (ATT_BLOCK, 3 * ATT_BLOCK), 1)
    rel = c - ATT_BLOCK - r
    in_window = jnp.logical_and(rel >= -WINDOW, rel <= WINDOW)
    in_seq = jnp.logical_and(jnp.logical_or(c >= ATT_BLOCK, i > 0),
                             jnp.logical_or(c < 2 * ATT_BLOCK, i < nb - 1))
    valid = jnp.logical_and(in_window, in_seq)
    outs = []
    for kv in range(ATT_KV_HEADS):
        qs = [_head(q, kv * ATT_GROUP + g) for g in range(ATT_GROUP)]
        outs += _sink_attention(qs, [_head(k_loc, kv), _head(kx, kv)], [_head(v_loc, kv), _head(vx, kv)],
                                sink_ref, kv, [valid, None])
    o_ref[...] = jnp.concatenate(outs, axis=1).astype(jnp.bfloat16)


def _attend_latent(qr, kr, vb, k_ctx, v_ctx, sink):
    nb = DEC_SEQ // ATT_BLOCK
    cur = lambda b, i: (b * nb + i, 0)
    prv = lambda b, i: (b * nb + jnp.maximum(i - 1, 0), 0)
    nxt = lambda b, i: (b * nb + jnp.minimum(i + 1, nb - 1), 0)
    kvs = lambda f: pl.BlockSpec((ATT_BLOCK, KV_DIM), f)
    ctx = pl.BlockSpec((None, PAST_LEN, KV_DIM), lambda b, i: (b, 0, 0))
    return pl.pallas_call(
        _attn_lat_kernel,
        grid=(DEC_BATCH, nb),
        in_specs=[
            pl.BlockSpec(memory_space=pltpu.SMEM),
            pl.BlockSpec((ATT_BLOCK, D_MODEL), cur),
            kvs(prv), kvs(cur), kvs(nxt), kvs(prv), kvs(cur), kvs(nxt), ctx, ctx,
        ],
        out_specs=pl.BlockSpec((ATT_BLOCK, D_MODEL), cur),
        out_shape=jax.ShapeDtypeStruct((N_SAMPLE, D_MODEL), jnp.bfloat16),
        compiler_params=_cparams(("arbitrary", "arbitrary")),
        name="attend_latent",
    )(sink, qr, kr, kr, kr, vb, vb, vb, k_ctx, v_ctx)


def _gconv_kernel(bg_ref, g_ref, gp_ref, gn_ref, w_ref, o_ref):
    i = pl.program_id(0)
    first = N_PROMPT // CONV_TILE
    per = DEC_SEQ // CONV_TILE
    t = (i - first) % per
    latent = i >= first
    hp = jnp.logical_and(latent, t > 0).astype(jnp.float32)
    hn = jnp.logical_and(latent, t < per - 1).astype(jnp.float32)
    window = jnp.concatenate([gp_ref[...] * hp, g_ref[...], gn_ref[...] * hn], axis=0)
    acc = None
    for k in range(SHORT_CONV):
        off = SUBLANES - SHORT_CONV // 2 + k
        term = w_ref[k:k + 1, :] * window[off:off + CONV_TILE, :]
        acc = term if acc is None else acc + term
    o_ref[...] = (bg_ref[...] * acc).astype(jnp.bfloat16)


def _gated_conv(bg, g, conv_w):
    per = CONV_TILE // SUBLANES
    last = N_TOK // SUBLANES - 1
    cw = jnp.pad(conv_w, ((0, SUBLANES - SHORT_CONV), (0, 0)))
    return pl.pallas_call(
        _gconv_kernel,
        grid=(N_TOK // CONV_TILE,),
        in_specs=[
            pl.BlockSpec((CONV_TILE, D_MODEL), lambda i: (i, 0)),
            pl.BlockSpec((CONV_TILE, D_MODEL), lambda i: (i, 0)),
            pl.BlockSpec((SUBLANES, D_MODEL), lambda i: (jnp.maximum(i * per - 1, 0), 0)),
            pl.BlockSpec((SUBLANES, D_MODEL), lambda i: (jnp.minimum(i * per + per, last), 0)),
            pl.BlockSpec((SUBLANES, D_MODEL), lambda i: (0, 0)),
        ],
        out_specs=pl.BlockSpec((CONV_TILE, D_MODEL), lambda i: (i, 0)),
        out_shape=jax.ShapeDtypeStruct((N_TOK, D_MODEL), jnp.bfloat16),
        compiler_params=_cparams(("arbitrary",)),
        name="gated_conv",
    )(bg, g, g, g, cw)


def kernel(x_prompt, x_sample, cache_k, cache_v, state_ssm, c, c_ctx,
           w_mod, b_mod, ln_w, ln_b,
           w_in_a, conv_w_a, conv_b_a, a_log, dt_bias, d_skip, ssm_norm_w, attn_sink, w_out_a,
           w_in_c, conv_w_c, w_out_c,
           w_router, b_router, w_gate, b_gate, w_lin, b_lin, w_down, b_down):
    bf16 = jnp.bfloat16
    x = jnp.concatenate([x_prompt.reshape(N_PROMPT, D_MODEL), x_sample.reshape(N_SAMPLE, D_MODEL)], axis=0)

    cond = jnp.concatenate([c_ctx[None, :], c, jnp.zeros((COND_PAD - N_COND, D_MODEL), jnp.float32)], axis=0)
    mod_all = _modulation(cond, w_mod, b_mod).reshape(DEPTH, COND_PAD, 6, D_MODEL)
    mod_all = jnp.pad(mod_all, ((0, 0), (0, 0), (0, MOD_ROWS - 6), (0, 0)))

    mod = mod_all[0]
    wa = w_in_a[0].astype(bf16)
    e = np.cumsum((0, SSM_INNER, SSM_CONV_DIM, 2 * SSM_HEADS, D_MODEL, KV_DIM, KV_DIM))
    w_dt = jnp.pad(wa[:, e[2]:e[3]], ((0, 0), (0, DT_PAD - 2 * SSM_HEADS)))
    w_parts = [wa[:, e[0]:e[1]], wa[:, e[1]:e[2]], w_dt, wa[:, e[3]:e[4]], wa[:, e[4]:e[5]], wa[:, e[5]:e[6]]]
    z, xbc_raw, dt_raw, q, k, v = _inproj(x, mod, w_parts)

    state_in = state_ssm[:, 0].reshape(DEC_BATCH, 2, SSM_INNER, D_STATE)
    y_ssm, h_f, h_b = _ssd_mixer(z, xbc_raw, dt_raw, conv_w_a[0], conv_b_a[0], a_log[0], dt_bias[0],
                                 d_skip[0], ssm_norm_w[0], state_in)

    sink = attn_sink[0]
    att_p = _attend_context(q, k, v, sink)
    qr, kr, vb = _rope_latent(q, k, v)
    k_ctx = cache_k[:, 0].reshape(DEC_BATCH, PAST_LEN, KV_DIM).astype(bf16)
    v_ctx = cache_v[:, 0].reshape(DEC_BATCH, PAST_LEN, KV_DIM).astype(bf16)
    att_s = _attend_latent(qr, kr, vb, k_ctx, v_ctx, sink)
    y_att = jnp.concatenate([att_p, att_s], axis=0)

    wo = w_out_a[0].astype(bf16)
    ln = jnp.stack([ln_w[0, 0], ln_b[0, 0]])
    x1, u2, *routing = _outproj([y_ssm, y_att], [wo[:SSM_INNER], wo[SSM_INNER:]], x, mod, ln,
                                w_router[0], b_router[0])
    x = _moe_layer(0, x1, u2, routing, mod, jnp.stack([ln_w[0, 1], ln_b[0, 1]]),
                   w_gate, b_gate, w_lin, b_lin, w_down, b_down)

    mod = mod_all[1]
    wc = w_in_c[0].astype(bf16)
    bg, g = _inproj(x, mod, [wc[:, :D_MODEL], wc[:, D_MODEL:2 * D_MODEL], wc[:, 2 * D_MODEL:]], gate_product=True)
    y_c = _gated_conv(bg, g, conv_w_c[0])
    ln = jnp.stack([ln_w[1, 0], ln_b[1, 0]])
    x1, u2, *routing = _outproj([y_c], [w_out_c[0].astype(bf16)], x, mod, ln, w_router[1], b_router[1])
    x = _moe_layer(1, x1, u2, routing, mod, jnp.stack([ln_w[1, 1], ln_b[1, 1]]),
                   w_gate, b_gate, w_lin, b_lin, w_down, b_down)

    y_prompt = x[:N_PROMPT].reshape(BATCH, SEQ, D_MODEL)
    y_sample = x[N_PROMPT:].reshape(DEC_BATCH, DEC_SEQ, D_MODEL)
    new_k = k[:N_PROMPT].reshape(BATCH, 1, SEQ, ATT_KV_HEADS, ATT_HEAD_DIM)
    new_v = v[:N_PROMPT].reshape(BATCH, 1, SEQ, ATT_KV_HEADS, ATT_HEAD_DIM)
    new_state = jnp.stack([h_f, h_b], axis=1).reshape(BATCH, 1, 2, SSM_HEADS, SSM_HEAD_DIM, D_STATE)
    return (y_prompt, y_sample, new_k, new_v, new_state)
```

```python
import functools

import numpy as np
import jax
import jax.numpy as jnp
from jax import lax
from jax.experimental import pallas as pl
from jax.experimental.pallas import tpu as pltpu

D_MODEL = 1024
BATCH = 32
SEQ = 256
DEPTH = 2
DEC_BATCH = 4
DEC_SEQ = 1024
PAST_LEN = 512
GRID_W = 64
SSM_HEAD_DIM = 64
SSM_INNER = D_MODEL
SSM_HEADS = SSM_INNER // SSM_HEAD_DIM
SSM_GROUPS = 2
SSM_HEADS_PER_GROUP = SSM_HEADS // SSM_GROUPS
D_STATE = 128
SSM_CONV = 5
SSM_CHUNK = 128
SSM_BC = SSM_GROUPS * D_STATE
SSM_CONV_DIM = SSM_INNER + 2 * SSM_BC
ATT_HEAD_DIM = 64
ATT_HEADS = D_MODEL // ATT_HEAD_DIM
ATT_KV_HEADS = 4
ATT_GROUP = ATT_HEADS // ATT_KV_HEADS
WINDOW = 128
ATT_BLOCK = 128
ATT_SCALE = ATT_HEAD_DIM ** -0.5
ROPE_BASE = 10000.0
ROPE_HALF = ATT_HEAD_DIM // 2
ROPE_QUARTER = ATT_HEAD_DIM // 4
SHORT_CONV = 3
N_EXPERTS = 32
TOP_K = 4
D_EXPERT = D_MODEL
SWIGLU_ALPHA = 1.702
SWIGLU_LIMIT = 7.0
N_EVEN = (DEPTH + 1) // 2
DEEPNORM_ALPHA = (2 * DEPTH) ** 0.25
LN_EPS = 1e-5
RMS_EPS = 1e-5
KV_DIM = ATT_KV_HEADS * ATT_HEAD_DIM

N_PROMPT = BATCH * SEQ
N_SAMPLE = DEC_BATCH * DEC_SEQ
N_TOK = N_PROMPT + N_SAMPLE
N_COND = 1 + DEC_BATCH
SUBLANES = 8
LANES = 128
COND_PAD = SUBLANES
MOD_ROWS = SUBLANES
DT_PAD = LANES

TOK_TILE = 512
CONV_TILE = 256
MOE_TILE = 256
MOE_ROWS = N_TOK * TOP_K + N_EXPERTS * MOE_TILE
MOE_TILES = MOE_ROWS // MOE_TILE
VMEM_LIMIT = 56 * 1024 * 1024
NEG_BIG = -1e30
ROW_TILES = D_MODEL // LANES

assert N_PROMPT % TOK_TILE == 0 and DEC_SEQ % TOK_TILE == 0
assert SEQ % CONV_TILE == 0 and DEC_SEQ % CONV_TILE == 0

_NT = (((1,), (1,)), ((), ()))
_TN = (((0,), (0,)), ((), ()))


def _cparams(sem):
    return pltpu.CompilerParams(dimension_semantics=sem, vmem_limit_bytes=VMEM_LIMIT)


def _cond_row(i):
    first = N_PROMPT // TOK_TILE
    per = DEC_SEQ // TOK_TILE
    return jnp.where(i < first, 0, 1 + (i - first) // per)


def _bf16_dot(a, b):
    return jnp.dot(a.astype(jnp.bfloat16), b.astype(jnp.bfloat16), preferred_element_type=jnp.float32)


def _mod_kernel(c_ref, w_ref, b_ref, o_ref):
    c = c_ref[...]
    s = c * jax.nn.sigmoid(c)
    o_ref[...] = _bf16_dot(s, w_ref[...]) + b_ref[...]


def _modulation(cond, w_mod, b_mod):
    tn = 1536
    return pl.pallas_call(
        _mod_kernel,
        grid=(DEPTH, 6 * D_MODEL // tn),
        in_specs=[
            pl.BlockSpec((COND_PAD, D_MODEL), lambda l, n: (0, 0)),
            pl.BlockSpec((None, D_MODEL, tn), lambda l, n: (l, 0, n)),
            pl.BlockSpec((None, 1, tn), lambda l, n: (l, 0, n)),
        ],
        out_specs=pl.BlockSpec((None, COND_PAD, tn), lambda l, n: (l, 0, n)),
        out_shape=jax.ShapeDtypeStruct((DEPTH, COND_PAD, 6 * D_MODEL), jnp.float32),
        compiler_params=_cparams(("arbitrary", "arbitrary")),
        name="modulation",
    )(cond, w_mod, b_mod.reshape(DEPTH, 1, 6 * D_MODEL))


def _inproj_kernel(x_ref, mod_ref, *refs, gate_product):
    n = len(refs) // 2 + (1 if gate_product else 0)
    w_refs, o_refs = refs[:n], refs[n:]
    u = x_ref[...] * (1.0 + mod_ref[1:2, :]) + mod_ref[0:1, :]
    ub = u.astype(jnp.bfloat16)
    outs = [jnp.dot(ub, w_ref[...], preferred_element_type=jnp.float32) for w_ref in w_refs]
    if gate_product:
        outs = outs[:-2] + [outs[-2] * outs[-1]]
    for o_ref, o in zip(o_refs, outs):
        o_ref[...] = o


def _inproj(x, mod, weights, gate_product=False):
    out_w = [w.shape[1] for w in weights]
    if gate_product:
        out_w = out_w[:-1]
    return pl.pallas_call(
        functools.partial(_inproj_kernel, gate_product=gate_product),
        grid=(N_TOK // TOK_TILE,),
        in_specs=[
            pl.BlockSpec((TOK_TILE, D_MODEL), lambda i: (i, 0)),
            pl.BlockSpec((None, MOD_ROWS, D_MODEL), lambda i: (_cond_row(i), 0, 0)),
        ] + [pl.BlockSpec(w.shape, lambda i: (0, 0)) for w in weights],
        out_specs=[pl.BlockSpec((TOK_TILE, n), lambda i: (i, 0)) for n in out_w],
        out_shape=[jax.ShapeDtypeStruct((N_TOK, n), jnp.float32) for n in out_w],
        compiler_params=_cparams(("arbitrary",)),
        name="inproj",
    )(x, mod, *weights)


def _layer_norm(v, w, b):
    mu = jnp.mean(v, axis=-1, keepdims=True)
    d = v - mu
    var = jnp.mean(d * d, axis=-1, keepdims=True)
    return d * lax.rsqrt(var + LN_EPS) * w + b


def _outproj_kernel(*refs, n_in):
    y_refs = refs[:n_in]
    w_refs = refs[n_in:2 * n_in]
    (x_ref, mod_ref, ln_ref, wr_ref, br_ref,
     x1_ref, u2_ref, idx_ref, gate_ref, rank_ref, cnt_ref, run_ref) = refs[2 * n_in:]
    f32 = jnp.float32

    @pl.when(pl.program_id(0) == 0)
    def _():
        run_ref[...] = jnp.zeros_like(run_ref)

    mix = _bf16_dot(y_refs[0][...], w_refs[0][...])
    for y_ref, w_ref in zip(y_refs[1:], w_refs[1:]):
        mix = mix + _bf16_dot(y_ref[...], w_ref[...])
    x1 = _layer_norm(DEEPNORM_ALPHA * x_ref[...] + mod_ref[2:3, :] * mix, ln_ref[0:1, :], ln_ref[1:2, :])
    x1_ref[...] = x1
    u2 = x1 * (1.0 + mod_ref[4:5, :]) + mod_ref[3:4, :]
    u2_ref[...] = u2
    logits = jnp.dot(u2, wr_ref[...], preferred_element_type=f32, precision=lax.Precision.HIGHEST) + br_ref[...]

    lane = lax.broadcasted_iota(jnp.int32, (TOK_TILE, N_EXPERTS), 1)
    work = logits
    vals, sels, idxs = [], [], []
    for _ in range(TOP_K):
        m = jnp.max(work, axis=-1, keepdims=True)
        idx = jnp.min(jnp.where(work == m, lane, N_EXPERTS), axis=-1, keepdims=True)
        sel = lane == idx
        vals.append(m)
        idxs.append(idx)
        sels.append(sel)
        work = jnp.where(sel, -jnp.inf, work)
    exps = [jnp.exp(v - vals[0]) for v in vals]
    den = exps[0]
    for e in exps[1:]:
        den = den + e
    onehot = sels[0].astype(f32)
    for sel in sels[1:]:
        onehot = onehot + sel.astype(f32)
    r = lax.broadcasted_iota(jnp.int32, (TOK_TILE, TOK_TILE), 0)
    c = lax.broadcasted_iota(jnp.int32, (TOK_TILE, TOK_TILE), 1)
    before = jnp.dot((r > c).astype(jnp.bfloat16), onehot.astype(jnp.bfloat16), preferred_element_type=f32)
    rank_all = before + run_ref[...]
    run = run_ref[...] + jnp.sum(onehot, axis=0, keepdims=True)
    run_ref[...] = run
    cnt_ref[...] = run.astype(jnp.int32)
    col = lax.broadcasted_iota(jnp.int32, (TOK_TILE, TOP_K), 1)
    idx_o = jnp.zeros((TOK_TILE, TOP_K), jnp.int32)
    gate_o = jnp.zeros((TOK_TILE, TOP_K), f32)
    rank_o = jnp.zeros((TOK_TILE, TOP_K), f32)
    for k in range(TOP_K):
        rank_k = jnp.sum(jnp.where(sels[k], rank_all, 0.0), axis=-1, keepdims=True)
        idx_o = jnp.where(col == k, idxs[k], idx_o)
        gate_o = jnp.where(col == k, exps[k] / den, gate_o)
        rank_o = jnp.where(col == k, rank_k, rank_o)
    idx_ref[...] = idx_o
    gate_ref[...] = gate_o
    rank_ref[...] = rank_o.astype(jnp.int32)


def _outproj(ys, ws, x, mod, ln, w_router, b_router):
    n_in = len(ys)
    return pl.pallas_call(
        functools.partial(_outproj_kernel, n_in=n_in),
        grid=(N_TOK // TOK_TILE,),
        in_specs=[pl.BlockSpec((TOK_TILE, y.shape[1]), lambda i: (i, 0)) for y in ys]
        + [pl.BlockSpec(w.shape, lambda i: (0, 0)) for w in ws]
        + [
            pl.BlockSpec((TOK_TILE, D_MODEL), lambda i: (i, 0)),
            pl.BlockSpec((None, MOD_ROWS, D_MODEL), lambda i: (_cond_row(i), 0, 0)),
            pl.BlockSpec((2, D_MODEL), lambda i: (0, 0)),
            pl.BlockSpec((D_MODEL, N_EXPERTS), lambda i: (0, 0)),
            pl.BlockSpec((1, N_EXPERTS), lambda i: (0, 0)),
        ],
        out_specs=[
            pl.BlockSpec((TOK_TILE, D_MODEL), lambda i: (i, 0)),
            pl.BlockSpec((TOK_TILE, D_MODEL), lambda i: (i, 0)),
            pl.BlockSpec((TOK_TILE, TOP_K), lambda i: (i, 0)),
            pl.BlockSpec((TOK_TILE, TOP_K), lambda i: (i, 0)),
            pl.BlockSpec((TOK_TILE, TOP_K), lambda i: (i, 0)),
            pl.BlockSpec((1, N_EXPERTS), lambda i: (0, 0)),
        ],
        out_shape=[
            jax.ShapeDtypeStruct((N_TOK, D_MODEL), jnp.float32),
            jax.ShapeDtypeStruct((N_TOK, D_MODEL), jnp.float32),
            jax.ShapeDtypeStruct((N_TOK, TOP_K), jnp.int32),
            jax.ShapeDtypeStruct((N_TOK, TOP_K), jnp.float32),
            jax.ShapeDtypeStruct((N_TOK, TOP_K), jnp.int32),
            jax.ShapeDtypeStruct((1, N_EXPERTS), jnp.int32),
        ],
        scratch_shapes=[pltpu.VMEM((1, N_EXPERTS), jnp.float32)],
        compiler_params=_cparams(("arbitrary",)),
        name="outproj_ln_router",
    )(*ys, *ws, x, mod, ln, w_router, b_router.reshape(1, N_EXPERTS))


def _moe_kernel(te_ref, nv_ref, src0_ref, last_ref, wslot_ref, nexte_ref, stok_ref, x_hbm,
                wg_hbm, bg_ref, wl_hbm, bl_ref, wd_hbm, bd_ref, o_ref,
                w_bf, xbuf, sem, wst, wsem, *, layer):
    i = pl.program_id(0)
    nv = nv_ref[0]
    valid = i < nv
    prev = te_ref[jnp.maximum(i - 1, 0)]
    new_expert = jnp.logical_or(i == 0, te_ref[i] != prev)

    def start_gather(tile, buf):
        base, last = src0_ref[tile], last_ref[tile]
        for r in range(MOE_TILE):
            tok = stok_ref[jnp.minimum(base + r, last)]
            pltpu.make_async_copy(x_hbm.at[tok], xbuf.at[buf, r], sem.at[buf]).start()

    def wait_gather(buf):
        pltpu.make_async_copy(x_hbm.at[pl.ds(0, MOE_TILE)], xbuf.at[buf], sem.at[buf]).wait()

    def weight_copies(expert, slot):
        return [pltpu.make_async_copy(w.at[layer, expert], wst.at[slot, j], wsem.at[slot])
                for j, w in enumerate((wg_hbm, wl_hbm, wd_hbm))]

    @pl.when(i == 0)
    def _():
        start_gather(0, 0)
        for cp in weight_copies(te_ref[0], 0):
            cp.start()

    @pl.when(jnp.logical_and(valid, new_expert))
    def _():
        slot = wslot_ref[i]
        nxt = nexte_ref[i]

        @pl.when(nxt >= 0)
        def _():
            for cp in weight_copies(nxt, 1 - slot):
                cp.start()

        for cp in weight_copies(te_ref[i], slot):
            cp.wait()
        for j in range(3):
            w_bf[j] = wst[slot, j].astype(jnp.bfloat16)

    for buf in range(2):
        @pl.when(jnp.logical_and(valid, i % 2 == buf))
        def _():
            wait_gather(buf)
            start_gather(jnp.minimum(i + 1, nv - 1), 1 - buf)
            x = jnp.concatenate([xbuf[buf, :, s, :] for s in range(ROW_TILES)], axis=1).astype(jnp.bfloat16)
            hg = jnp.dot(x, w_bf[0], preferred_element_type=jnp.float32) + bg_ref[...]
            hl = jnp.dot(x, w_bf[1], preferred_element_type=jnp.float32) + bl_ref[...]
            hg = jnp.minimum(hg, SWIGLU_LIMIT)
            hl = jnp.clip(hl, -SWIGLU_LIMIT, SWIGLU_LIMIT)
            h = hg * jax.nn.sigmoid(SWIGLU_ALPHA * hg) * (hl + 1.0)
            o_ref[...] = (jnp.dot(h.astype(jnp.bfloat16), w_bf[2], preferred_element_type=jnp.float32)
                          + bd_ref[...])

        @pl.when(jnp.logical_and(i == nv - 1, i % 2 == buf))
        def _():
            wait_gather(1 - buf)

    @pl.when(jnp.logical_not(valid))
    def _():
        o_ref[...] = jnp.zeros_like(o_ref)


def _moe_grouped(layer, tables, sorted_tok, x, w_gate, b_gate, w_lin, b_lin, w_down, b_down):
    hbm = pl.BlockSpec(memory_space=pl.ANY)
    bspec = pl.BlockSpec((None, None, 1, D_EXPERT), lambda i, te, *_: (layer, te[i], 0, 0))
    grid_spec = pltpu.PrefetchScalarGridSpec(
        num_scalar_prefetch=len(tables) + 1,
        grid=(MOE_TILES,),
        in_specs=[hbm, hbm, bspec, hbm, bspec, hbm, bspec],
        out_specs=pl.BlockSpec((MOE_TILE, D_MODEL), lambda i, *_: (i, 0)),
        scratch_shapes=[
            pltpu.VMEM((3, D_MODEL, D_EXPERT), jnp.bfloat16),
            pltpu.VMEM((2, MOE_TILE, ROW_TILES, LANES), jnp.float32), pltpu.SemaphoreType.DMA((2,)),
            pltpu.VMEM((2, 3, D_MODEL, D_EXPERT), jnp.float32), pltpu.SemaphoreType.DMA((2,)),
        ],
    )
    bshape = (DEPTH, N_EXPERTS, 1, D_EXPERT)
    return pl.pallas_call(
        functools.partial(_moe_kernel, layer=layer),
        grid_spec=grid_spec,
        out_shape=jax.ShapeDtypeStruct((MOE_ROWS, D_MODEL), jnp.float32),
        compiler_params=_cparams(("arbitrary",)),
        name="moe_grouped",
    )(*tables, sorted_tok, x.reshape(N_TOK, ROW_TILES, LANES), w_gate, b_gate.reshape(bshape),
      w_lin, b_lin.reshape(bshape), w_down, b_down.reshape(bshape))


def _combine_kernel(y_ref, g_ref, x_ref, mod_ref, ln_ref, o_ref):
    g = g_ref[...]
    ff = g[:, 0:1] * y_ref[0]
    for k in range(1, TOP_K):
        ff = ff + g[:, k:k + 1] * y_ref[k]
    o_ref[...] = _layer_norm(DEEPNORM_ALPHA * x_ref[...] + mod_ref[5:6, :] * ff, ln_ref[0:1, :], ln_ref[1:2, :])


def _combine(yk, gates, x, mod, ln):
    return pl.pallas_call(
        _combine_kernel,
        grid=(N_TOK // TOK_TILE,),
        in_specs=[
            pl.BlockSpec((TOP_K, TOK_TILE, D_MODEL), lambda i: (0, i, 0)),
            pl.BlockSpec((TOK_TILE, TOP_K), lambda i: (i, 0)),
            pl.BlockSpec((TOK_TILE, D_MODEL), lambda i: (i, 0)),
            pl.BlockSpec((None, MOD_ROWS, D_MODEL), lambda i: (_cond_row(i), 0, 0)),
            pl.BlockSpec((2, D_MODEL), lambda i: (0, 0)),
        ],
        out_specs=pl.BlockSpec((TOK_TILE, D_MODEL), lambda i: (i, 0)),
        out_shape=jax.ShapeDtypeStruct((N_TOK, D_MODEL), jnp.float32),
        compiler_params=_cparams(("arbitrary",)),
        name="combine_ln",
    )(yk, gates, x, mod, ln)


def _route(top_idx, rank, counts):
    counts = counts.reshape(N_EXPERTS)
    padded = (counts + MOE_TILE - 1) // MOE_TILE * MOE_TILE
    pend = jnp.cumsum(padded)
    pstart = pend - padded
    start = jnp.cumsum(counts) - counts
    experts = jnp.arange(N_EXPERTS, dtype=jnp.int32)
    lookup = lambda table, idx: jnp.sum(jnp.where(idx[..., None] == experts, table, 0), axis=-1)
    slot = lookup(pstart, top_idx) + rank
    key = lookup(start, top_idx) + rank
    tok = jnp.broadcast_to(jnp.arange(N_TOK, dtype=jnp.int32)[:, None], (N_TOK, TOP_K))
    _, sorted_tok = lax.sort((key.reshape(-1), tok.reshape(-1)), num_keys=1)
    n_valid = (pend[-1] // MOE_TILE).astype(jnp.int32)
    tile_start = jnp.arange(MOE_TILES, dtype=jnp.int32) * MOE_TILE
    tile_expert = jnp.sum((tile_start[:, None] >= pend[None, :]).astype(jnp.int32), axis=1)
    tile_expert = jnp.where(tile_start < pend[-1], tile_expert, tile_expert[n_valid - 1])
    src0 = tile_start - lookup(pstart, tile_expert) + lookup(start, tile_expert)
    src_last = lookup(start, tile_expert) + lookup(counts, tile_expert) - 1
    used = (counts > 0).astype(jnp.int32)
    order = jnp.cumsum(used) - used
    later = jnp.where(jnp.logical_and(experts[None, :] > experts[:, None], used[None, :] > 0),
                      experts[None, :], N_EXPERTS)
    successor = jnp.min(later, axis=1)
    successor = jnp.where(successor < N_EXPERTS, successor, -1)
    wslot = lookup(order, tile_expert) % 2
    next_expert = lookup(successor, tile_expert)
    tables = (tile_expert, n_valid.reshape(1), src0, src_last, wslot, next_expert)
    return slot, sorted_tok, tables


def _moe_layer(i, x1, u2, routing, mod, ln, w_gate, b_gate, w_lin, b_lin, w_down, b_down):
    top_idx, gates, rank, counts = routing
    slot, sorted_tok, tables = _route(top_idx, rank, counts)
    y = _moe_grouped(i, tables, sorted_tok, u2, w_gate, b_gate, w_lin, b_lin, w_down, b_down)
    yk = jnp.take(y, slot.T, axis=0, mode="clip")
    return _combine(yk, gates, x1, mod, ln)


_F_ROW, _F_FIRST, _F_LAST, _F_PREV8, _F_HASPREV, _F_NEXT8, _F_HASNEXT, _F_H0, _F_USEH0, _F_HOUT, _F_WRITEH = range(11)
_N_FIELDS = 11
_SSD_STEPS = N_TOK // SSM_CHUNK


def _ssd_table(reverse):
    rows = []
    for prompt, nseq, slen, base in ((True, BATCH, SEQ, 0), (False, DEC_BATCH, DEC_SEQ, N_PROMPT)):
        nc = slen // SSM_CHUNK
        for s in range(nseq):
            for c in range(nc):
                row = (base + s * slen) // SSM_CHUNK + c
                first, last = (c == nc - 1, c == 0) if reverse else (c == 0, c == nc - 1)
                per = SSM_CHUNK // SUBLANES
                rows.append([
                    row, int(first), int(last),
                    max(row * per - 1, 0), int(c > 0),
                    min(row * per + per, N_TOK // SUBLANES - 1), int(c < nc - 1),
                    0 if prompt else s, int(not prompt),
                    s if prompt else BATCH - 1, int(prompt and last),
                ])
    if reverse:
        rows = rows[::-1]
    return np.asarray(rows, np.int32).T.reshape(-1)


def _fld(tbl, f, s):
    return tbl[f * _SSD_STEPS + s]


def _exact_expand(a, e_ref, passes):
    k = a.shape[1]
    pieces, rem = [], a
    for _ in range(passes):
        piece = rem.astype(jnp.bfloat16)
        rem = rem - piece.astype(jnp.float32)
        pieces.append(piece)
    out = None
    for i in range(0, passes, 2):
        pair = pieces[i:i + 2]
        lhs = pair[0] if len(pair) == 1 else jnp.concatenate(pair, axis=1)
        t = jnp.dot(lhs, e_ref[0:k * len(pair), :], preferred_element_type=jnp.float32)
        out = t if out is None else out + t
    return out


def _ssd_chunk(xs, bmat, cmat, dt_raw, dtb_row, alog_row, ex_ref, eb_ref, s_ref, reverse):
    f32, bf16 = jnp.float32, jnp.bfloat16
    base = SSM_HEADS if reverse else 0
    lane = lax.broadcasted_iota(jnp.int32, (1, DT_PAD), 1)
    head_cols = jnp.logical_and(lane >= base, lane < base + SSM_HEADS)
    xr = dt_raw + dtb_row
    dt = jnp.maximum(xr, 0.0) + jnp.log1p(jnp.exp(-jnp.abs(xr)))
    a_row = jnp.where(head_cols, -jnp.exp(alog_row), 0.0)
    d_a = dt * a_row
    r = lax.broadcasted_iota(jnp.int32, (SSM_CHUNK, SSM_CHUNK), 0)
    c = lax.broadcasted_iota(jnp.int32, (SSM_CHUNK, SSM_CHUNK), 1)
    tri = (r <= c) if reverse else (r >= c)
    cum = jnp.dot(tri.astype(f32), d_a, preferred_element_type=f32, precision=lax.Precision.HIGHEST)
    cum_t = cum.T
    total = cum[0:1, :] if reverse else cum[SSM_CHUNK - 1:SSM_CHUNK, :]
    stacked = jnp.concatenate([dt, jnp.exp(cum), dt * jnp.exp(total - cum),
                               jnp.broadcast_to(jnp.exp(total), (SUBLANES, DT_PAD))], axis=0)
    stacked_x = _exact_expand(stacked, ex_ref, 2)
    dt_x = stacked_x[0:SSM_CHUNK]
    e_in_x = stacked_x[SSM_CHUNK:2 * SSM_CHUNK]
    w_end_x = stacked_x[2 * SSM_CHUNK:3 * SSM_CHUNK]
    dec_x = stacked_x[3 * SSM_CHUNK:3 * SSM_CHUNK + 1]
    cum_b = _exact_expand(cum, eb_ref, 3)
    xdt = xs * dt_x
    xw = (xs * w_end_x).astype(bf16)
    s_in = s_ref[...]
    s_bf = s_in.astype(bf16)
    lo_half = lax.broadcasted_iota(jnp.int32, (1, 2 * SSM_HEAD_DIM), 1) < SSM_HEAD_DIM
    gw = SSM_HEADS_PER_GROUP * SSM_HEAD_DIM
    pieces = []
    for g in range(SSM_GROUPS):
        bg = bmat[:, g * D_STATE:(g + 1) * D_STATE]
        cg = cmat[:, g * D_STATE:(g + 1) * D_STATE]
        gcols = slice(g * gw, (g + 1) * gw)
        cb = lax.dot_general(cg, bg, _NT, preferred_element_type=f32)
        y_off = jnp.dot(cg, s_bf[:, gcols], preferred_element_type=f32) * e_in_x[:, gcols]
        upd = lax.dot_general(bg, xw[:, gcols], _TN, preferred_element_type=f32)
        s_ref[:, gcols] = s_in[:, gcols] * dec_x[:, gcols] + upd
        for q in range(SSM_HEADS_PER_GROUP // 2):
            h0 = g * SSM_HEADS_PER_GROUP + 2 * q
            pcols = slice(h0 * SSM_HEAD_DIM, (h0 + 2) * SSM_HEAD_DIM)
            xp = xdt[:, pcols]
            bd = jnp.concatenate([jnp.where(lo_half, xp, 0.0), jnp.where(lo_half, 0.0, xp)], axis=0).astype(bf16)
            ms = []
            for h in (h0, h0 + 1):
                ch = base + h
                seg = cum_b[:, h * SSM_CHUNK:(h + 1) * SSM_CHUNK] - cum_t[ch:ch + 1, :]
                decay = jnp.exp(jnp.where(tri, seg, NEG_BIG))
                ms.append((cb * decay).astype(bf16))
            y_diag = jnp.dot(jnp.concatenate(ms, axis=1), bd, preferred_element_type=f32)
            pieces.append(y_diag + y_off[:, 2 * q * SSM_HEAD_DIM:(2 * q + 2) * SSM_HEAD_DIM])
    return jnp.concatenate(pieces, axis=1)


def _ssd_init_state(tbl, step, h0_ref, s_ref):
    @pl.when(_fld(tbl, _F_FIRST, step) == 1)
    def _():
        use = _fld(tbl, _F_USEH0, step) == 1
        s_ref[...] = jnp.where(use, h0_ref[...], 0.0).T


def _ssd_write_state(tbl, step, hout_ref, s_ref):
    @pl.when(_fld(tbl, _F_WRITEH, step) == 1)
    def _():
        hout_ref[...] = s_ref[...].T


def _ssd_fwd_kernel(tbl, xc_ref, xp_ref, xn_ref, dt_ref, cw_ref, cbias_ref, dtb_ref, alog_ref, ex_ref, eb_ref,
                    h0_ref, xs_ref, bc_ref, yf_ref, hout_ref, s_ref):
    step = pl.program_id(0)
    _ssd_init_state(tbl, step, h0_ref, s_ref)
    hp = (_fld(tbl, _F_HASPREV, step) == 1).astype(jnp.float32)
    hn = (_fld(tbl, _F_HASNEXT, step) == 1).astype(jnp.float32)
    window = jnp.concatenate([xp_ref[...] * hp, xc_ref[...], xn_ref[...] * hn], axis=0)
    acc = cbias_ref[...] + cw_ref[0:1, :] * window[SUBLANES - 2:SUBLANES - 2 + SSM_CHUNK, :]
    for k in range(1, SSM_CONV):
        off = SUBLANES - SSM_CONV // 2 + k
        acc = acc + cw_ref[k:k + 1, :] * window[off:off + SSM_CHUNK, :]
    xbc = acc * jax.nn.sigmoid(acc)
    xs = xbc[:, :SSM_INNER]
    bc = xbc[:, SSM_INNER:].astype(jnp.bfloat16)
    xs_ref[...] = xs
    bc_ref[...] = bc
    yf_ref[...] = _ssd_chunk(xs, bc[:, :SSM_BC], bc[:, SSM_BC:], dt_ref[...], dtb_ref[...], alog_ref[...],
                             ex_ref, eb_ref, s_ref, reverse=False)
    _ssd_write_state(tbl, step, hout_ref, s_ref)


def _ssd_bwd_kernel(tbl, xs_ref, bc_ref, dt_ref, yf_ref, z_ref, dtb_ref, alog_ref, dskip_ref, nw_ref,
                    ex_ref, eb_ref, h0_ref, y_ref, hout_ref, s_ref):
    step = pl.program_id(0)
    _ssd_init_state(tbl, step, h0_ref, s_ref)
    xs = xs_ref[...]
    bc = bc_ref[...]
    yb = _ssd_chunk(xs, bc[:, :SSM_BC], bc[:, SSM_BC:], dt_ref[...], dtb_ref[...], alog_ref[...],
                    ex_ref, eb_ref, s_ref, reverse=True)
    z = z_ref[...]
    hg = (yf_ref[...] + yb + dskip_ref[...] * xs) * (z * jax.nn.sigmoid(z))
    gw = SSM_INNER // SSM_GROUPS
    outs = []
    for g in range(SSM_GROUPS):
        hgg = hg[:, g * gw:(g + 1) * gw]
        outs.append(hgg * lax.rsqrt(jnp.mean(hgg * hgg, axis=-1, keepdims=True) + RMS_EPS))
    y_ref[...] = (jnp.concatenate(outs, axis=1) * nw_ref[...]).astype(jnp.bfloat16)
    _ssd_write_state(tbl, step, hout_ref, s_ref)


def _ssd_mixer(z, xbc_raw, dt_raw, conv_w, conv_b, a_log, dt_bias, d_skip, norm_w, state_in):
    f32 = jnp.float32
    row = lambda f: (lambda s, tbl: (_fld(tbl, f, s), 0))
    chunk_spec = lambda w: pl.BlockSpec((SSM_CHUNK, w), row(_F_ROW))
    const_spec = lambda shape: pl.BlockSpec(shape, lambda s, tbl: (0,) * len(shape))
    h0_spec = lambda d: pl.BlockSpec((None, None, SSM_INNER, D_STATE), lambda s, tbl: (_fld(tbl, _F_H0, s), d, 0, 0))
    hout_spec = pl.BlockSpec((None, SSM_INNER, D_STATE), lambda s, tbl: (_fld(tbl, _F_HOUT, s), 0, 0))
    hout_shape = jax.ShapeDtypeStruct((BATCH, SSM_INNER, D_STATE), f32)
    state_scratch = [pltpu.VMEM((D_STATE, SSM_INNER), f32)]

    def expanders(reverse):
        base = SSM_HEADS if reverse else 0
        ex = np.zeros((DT_PAD, SSM_INNER), np.float32)
        eb = np.zeros((DT_PAD, SSM_HEADS * SSM_CHUNK), np.float32)
        for h in range(SSM_HEADS):
            ex[base + h, h * SSM_HEAD_DIM:(h + 1) * SSM_HEAD_DIM] = 1.0
            eb[base + h, h * SSM_CHUNK:(h + 1) * SSM_CHUNK] = 1.0
        twice = lambda e: jnp.asarray(np.concatenate([e, e], axis=0), jnp.bfloat16)
        return twice(ex), twice(eb)

    expander_specs = [const_spec((2 * DT_PAD, SSM_INNER)), const_spec((2 * DT_PAD, SSM_HEADS * SSM_CHUNK))]

    cw = jnp.pad(conv_w, ((0, SUBLANES - SSM_CONV), (0, 0)))
    cbias = conv_b.reshape(1, SSM_CONV_DIM)
    dtb = jnp.pad(dt_bias.reshape(1, 2 * SSM_HEADS), ((0, 0), (0, DT_PAD - 2 * SSM_HEADS)))
    alog = jnp.pad(a_log.reshape(1, 2 * SSM_HEADS), ((0, 0), (0, DT_PAD - 2 * SSM_HEADS)))
    dskip = jnp.repeat(d_skip, SSM_HEAD_DIM).reshape(1, SSM_INNER)
    nw = norm_w.reshape(1, SSM_INNER)

    xs, bc, yf, h_f = pl.pallas_call(
        _ssd_fwd_kernel,
        grid_spec=pltpu.PrefetchScalarGridSpec(
            num_scalar_prefetch=1,
            grid=(_SSD_STEPS,),
            in_specs=[
                chunk_spec(SSM_CONV_DIM),
                pl.BlockSpec((SUBLANES, SSM_CONV_DIM), row(_F_PREV8)),
                pl.BlockSpec((SUBLANES, SSM_CONV_DIM), row(_F_NEXT8)),
                chunk_spec(DT_PAD),
                const_spec((SUBLANES, SSM_CONV_DIM)), const_spec((1, SSM_CONV_DIM)),
                const_spec((1, DT_PAD)), const_spec((1, DT_PAD)), *expander_specs,
                h0_spec(0),
            ],
            out_specs=[chunk_spec(SSM_INNER), chunk_spec(2 * SSM_BC), chunk_spec(SSM_INNER), hout_spec],
            scratch_shapes=state_scratch,
        ),
        out_shape=[
            jax.ShapeDtypeStruct((N_TOK, SSM_INNER), f32),
            jax.ShapeDtypeStruct((N_TOK, 2 * SSM_BC), jnp.bfloat16),
            jax.ShapeDtypeStruct((N_TOK, SSM_INNER), f32),
            hout_shape,
        ],
        compiler_params=_cparams(("arbitrary",)),
        name="ssd_forward",
    )(jnp.asarray(_ssd_table(False)), xbc_raw, xbc_raw, xbc_raw, dt_raw, cw, cbias, dtb, alog,
      *expanders(False), state_in)

    y, h_b = pl.pallas_call(
        _ssd_bwd_kernel,
        grid_spec=pltpu.PrefetchScalarGridSpec(
            num_scalar_prefetch=1,
            grid=(_SSD_STEPS,),
            in_specs=[
                chunk_spec(SSM_INNER), chunk_spec(2 * SSM_BC), chunk_spec(DT_PAD), chunk_spec(SSM_INNER),
                chunk_spec(SSM_INNER),
                const_spec((1, DT_PAD)), const_spec((1, DT_PAD)),
                const_spec((1, SSM_INNER)), const_spec((1, SSM_INNER)), *expander_specs,
                h0_spec(1),
            ],
            out_specs=[chunk_spec(SSM_INNER), hout_spec],
            scratch_shapes=state_scratch,
        ),
        out_shape=[jax.ShapeDtypeStruct((N_TOK, SSM_INNER), jnp.bfloat16), hout_shape],
        compiler_params=_cparams(("arbitrary",)),
        name="ssd_backward",
    )(jnp.asarray(_ssd_table(True)), xs, bc, dt_raw, yf, z, dtb, alog, dskip, nw, *expanders(True), state_in)
    return y, h_f, h_b


def _sink_attention(q_heads, keys, vals, sink_ref, kv, masks):
    outs = []
    for g, qh in enumerate(q_heads):
        sk = sink_ref[kv * ATT_GROUP + g]
        scores = []
        for kk, mask in zip(keys, masks):
            s = lax.dot_general(qh, kk, _NT, preferred_element_type=jnp.float32) * ATT_SCALE
            scores.append(s if mask is None else jnp.where(mask, s, NEG_BIG))
        m = sk
        for s in scores:
            m = jnp.maximum(m, jnp.max(s, axis=-1, keepdims=True))
        den = jnp.exp(sk - m)
        acc = None
        for s, vv in zip(scores, vals):
            p = jnp.exp(s - m)
            den = den + jnp.sum(p, axis=-1, keepdims=True)
            pv = jnp.dot(p.astype(jnp.bfloat16), vv, preferred_element_type=jnp.float32)
            acc = pv if acc is None else acc + pv
        outs.append(acc / den)
    return outs


def _head(x, h):
    return x[:, h * ATT_HEAD_DIM:(h + 1) * ATT_HEAD_DIM]


def _attn_ctx_kernel(sink_ref, q_ref, k_ref, v_ref, o_ref):
    bf16 = jnp.bfloat16
    q = q_ref[...].astype(bf16)
    k = k_ref[...].astype(bf16)
    v = v_ref[...].astype(bf16)
    outs = []
    for kv in range(ATT_KV_HEADS):
        qs = [_head(q, kv * ATT_GROUP + g) for g in range(ATT_GROUP)]
        outs += _sink_attention(qs, [_head(k, kv)], [_head(v, kv)], sink_ref, kv, [None])
    o_ref[...] = jnp.concatenate(outs, axis=1).astype(bf16)


def _attend_context(q, k, v, sink):
    return pl.pallas_call(
        _attn_ctx_kernel,
        grid=(BATCH,),
        in_specs=[
            pl.BlockSpec(memory_space=pltpu.SMEM),
            pl.BlockSpec((SEQ, D_MODEL), lambda b: (b, 0)),
            pl.BlockSpec((SEQ, KV_DIM), lambda b: (b, 0)),
            pl.BlockSpec((SEQ, KV_DIM), lambda b: (b, 0)),
        ],
        out_specs=pl.BlockSpec((SEQ, D_MODEL), lambda b: (b, 0)),
        out_shape=jax.ShapeDtypeStruct((N_PROMPT, D_MODEL), jnp.bfloat16),
        compiler_params=_cparams(("arbitrary",)),
        name="attend_context",
    )(sink, q, k, v)


def _rope_tables(width):
    t = np.arange(DEC_SEQ)
    d = np.arange(width) % ATT_HEAD_DIM
    pos = np.where(d[None, :] < ROPE_HALF, (t // GRID_W)[:, None], (t % GRID_W)[:, None]).astype(np.float32)
    inv = (ROPE_BASE ** (-np.arange(ROPE_QUARTER, dtype=np.float32) / ROPE_QUARTER)).astype(np.float32)
    ang = pos * inv[d % ROPE_QUARTER][None, :]
    sign = np.where((d % ROPE_HALF) < ROPE_QUARTER, -1.0, 1.0).astype(np.float32)
    return jnp.asarray(np.cos(ang), jnp.float32), jnp.asarray(np.sin(ang) * sign[None, :], jnp.float32)


def _rope(x, cos, sin_signed):
    width = x.shape[1]
    lane = lax.broadcasted_iota(jnp.int32, (1, width), 1)
    first = (lane % ROPE_HALF) < ROPE_QUARTER
    partner = jnp.where(first, pltpu.roll(x, width - ROPE_QUARTER, 1), pltpu.roll(x, ROPE_QUARTER, 1))
    return x * cos + partner * sin_signed


def _rope_kernel(q_ref, k_ref, v_ref, cos_ref, sin_ref, qo_ref, ko_ref, vo_ref):
    bf16 = jnp.bfloat16
    cos, sin = cos_ref[...], sin_ref[...]
    qo_ref[...] = _rope(q_ref[...], cos, sin).astype(bf16)
    ko_ref[...] = _rope(k_ref[...], cos[:, :KV_DIM], sin[:, :KV_DIM]).astype(bf16)
    vo_ref[...] = v_ref[...].astype(bf16)


def _rope_latent(q, k, v):
    nb = DEC_SEQ // ATT_BLOCK
    off = N_PROMPT // ATT_BLOCK
    cos, sin = _rope_tables(D_MODEL)
    tok = lambda b, i: (off + b * nb + i, 0)
    out = lambda b, i: (b * nb + i, 0)
    return pl.pallas_call(
        _rope_kernel,
        grid=(DEC_BATCH, nb),
        in_specs=[
            pl.BlockSpec((ATT_BLOCK, D_MODEL), tok),
            pl.BlockSpec((ATT_BLOCK, KV_DIM), tok),
            pl.BlockSpec((ATT_BLOCK, KV_DIM), tok),
            pl.BlockSpec((ATT_BLOCK, D_MODEL), lambda b, i: (i, 0)),
            pl.BlockSpec((ATT_BLOCK, D_MODEL), lambda b, i: (i, 0)),
        ],
        out_specs=[
            pl.BlockSpec((ATT_BLOCK, D_MODEL), out),
            pl.BlockSpec((ATT_BLOCK, KV_DIM), out),
            pl.BlockSpec((ATT_BLOCK, KV_DIM), out),
        ],
        out_shape=[
            jax.ShapeDtypeStruct((N_SAMPLE, D_MODEL), jnp.bfloat16),
            jax.ShapeDtypeStruct((N_SAMPLE, KV_DIM), jnp.bfloat16),
            jax.ShapeDtypeStruct((N_SAMPLE, KV_DIM), jnp.bfloat16),
        ],
        compiler_params=_cparams(("arbitrary", "arbitrary")),
        name="rope_latent",
    )(q, k, v, cos, sin)


def _attn_lat_kernel(sink_ref, q_ref, kp_ref, kc_ref, kn_ref, vp_ref, vc_ref, vn_ref, kx_ref, vx_ref, o_ref):
    i = pl.program_id(1)
    nb = pl.num_programs(1)
    q = q_ref[...]
    k_loc = jnp.concatenate([kp_ref[...], kc_ref[...], kn_ref[...]], axis=0)
    v_loc = jnp.concatenate([vp_ref[...], vc_ref[...], vn_ref[...]], axis=0)
    kx, vx = kx_ref[...], vx_ref[...]
    r = lax.broadcasted_iota(jnp.int32, (ATT_BLOCK, 3 * ATT_BLOCK), 0)
    c = lax.broadcasted_iota(jnp.int32, (ATT_BLOCK, 3 * ATT_BLOCK), 1)
    rel = c - ATT_BLOCK - r
    in_window = jnp.logical_and(rel >= -WINDOW, rel <= WINDOW)
    in_seq = jnp.logical_and(jnp.logical_or(c >= ATT_BLOCK, i > 0),
                             jnp.logical_or(c < 2 * ATT_BLOCK, i < nb - 1))
    valid = jnp.logical_and(in_window, in_seq)
    outs = []
    for kv in range(ATT_KV_HEADS):
        qs = [_head(q, kv * ATT_GROUP + g) for g in range(ATT_GROUP)]
        outs += _sink_attention(qs, [_head(k_loc, kv), _head(kx, kv)], [_head(v_loc, kv), _head(vx, kv)],
                                sink_ref, kv, [valid, None])
    o_ref[...] = jnp.concatenate(outs, axis=1).astype(jnp.bfloat16)


def _attend_latent(qr, kr, vb, k_ctx, v_ctx, sink):
    nb = DEC_SEQ // ATT_BLOCK
    cur = lambda b, i: (b * nb + i, 0)
    prv = lambda b, i: (b * nb + jnp.maximum(i - 1, 0), 0)
    nxt = lambda b, i: (b * nb + jnp.minimum(i + 1, nb - 1), 0)
    kvs = lambda f: pl.BlockSpec((ATT_BLOCK, KV_DIM), f)
    ctx = pl.BlockSpec((None, PAST_LEN, KV_DIM), lambda b, i: (b, 0, 0))
    return pl.pallas_call(
        _attn_lat_kernel,
        grid=(DEC_BATCH, nb),
        in_specs=[
            pl.BlockSpec(memory_space=pltpu.SMEM),
            pl.BlockSpec((ATT_BLOCK, D_MODEL), cur),
            kvs(prv), kvs(cur), kvs(nxt), kvs(prv), kvs(cur), kvs(nxt), ctx, ctx,
        ],
        out_specs=pl.BlockSpec((ATT_BLOCK, D_MODEL), cur),
        out_shape=jax.ShapeDtypeStruct((N_SAMPLE, D_MODEL), jnp.bfloat16),
        compiler_params=_cparams(("arbitrary", "arbitrary")),
        name="attend_latent",
    )(sink, qr, kr, kr, kr, vb, vb, vb, k_ctx, v_ctx)


def _gconv_kernel(bg_ref, g_ref, gp_ref, gn_ref, w_ref, o_ref):
    i = pl.program_id(0)
    first = N_PROMPT // CONV_TILE
    per = DEC_SEQ // CONV_TILE
    t = (i - first) % per
    latent = i >= first
    hp = jnp.logical_and(latent, t > 0).astype(jnp.float32)
    hn = jnp.logical_and(latent, t < per - 1).astype(jnp.float32)
    window = jnp.concatenate([gp_ref[...] * hp, g_ref[...], gn_ref[...] * hn], axis=0)
    acc = None
    for k in range(SHORT_CONV):
        off = SUBLANES - SHORT_CONV // 2 + k
        term = w_ref[k:k + 1, :] * window[off:off + CONV_TILE, :]
        acc = term if acc is None else acc + term
    o_ref[...] = (bg_ref[...] * acc).astype(jnp.bfloat16)


def _gated_conv(bg, g, conv_w):
    per = CONV_TILE // SUBLANES
    last = N_TOK // SUBLANES - 1
    cw = jnp.pad(conv_w, ((0, SUBLANES - SHORT_CONV), (0, 0)))
    return pl.pallas_call(
        _gconv_kernel,
        grid=(N_TOK // CONV_TILE,),
        in_specs=[
            pl.BlockSpec((CONV_TILE, D_MODEL), lambda i: (i, 0)),
            pl.BlockSpec((CONV_TILE, D_MODEL), lambda i: (i, 0)),
            pl.BlockSpec((SUBLANES, D_MODEL), lambda i: (jnp.maximum(i * per - 1, 0), 0)),
            pl.BlockSpec((SUBLANES, D_MODEL), lambda i: (jnp.minimum(i * per + per, last), 0)),
            pl.BlockSpec((SUBLANES, D_MODEL), lambda i: (0, 0)),
        ],
        out_specs=pl.BlockSpec((CONV_TILE, D_MODEL), lambda i: (i, 0)),
        out_shape=jax.ShapeDtypeStruct((N_TOK, D_MODEL), jnp.bfloat16),
        compiler_params=_cparams(("arbitrary",)),
        name="gated_conv",
    )(bg, g, g, g, cw)


def kernel(x_prompt, x_sample, cache_k, cache_v, state_ssm, c, c_ctx,
           w_mod, b_mod, ln_w, ln_b,
           w_in_a, conv_w_a, conv_b_a, a_log, dt_bias, d_skip, ssm_norm_w, attn_sink, w_out_a,
           w_in_c, conv_w_c, w_out_c,
           w_router, b_router, w_gate, b_gate, w_lin, b_lin, w_down, b_down):
    bf16 = jnp.bfloat16
    x = jnp.concatenate([x_prompt.reshape(N_PROMPT, D_MODEL), x_sample.reshape(N_SAMPLE, D_MODEL)], axis=0)

    cond = jnp.concatenate([c_ctx[None, :], c, jnp.zeros((COND_PAD - N_COND, D_MODEL), jnp.float32)], axis=0)
    mod_all = _modulation(cond, w_mod, b_mod).reshape(DEPTH, COND_PAD, 6, D_MODEL)
    mod_all = jnp.pad(mod_all, ((0, 0), (0, 0), (0, MOD_ROWS - 6), (0, 0)))

    mod = mod_all[0]
    wa = w_in_a[0].astype(bf16)
    e = np.cumsum((0, SSM_INNER, SSM_CONV_DIM, 2 * SSM_HEADS, D_MODEL, KV_DIM, KV_DIM))
    w_dt = jnp.pad(wa[:, e[2]:e[3]], ((0, 0), (0, DT_PAD - 2 * SSM_HEADS)))
    w_parts = [wa[:, e[0]:e[1]], wa[:, e[1]:e[2]], w_dt, wa[:, e[3]:e[4]], wa[:, e[4]:e[5]], wa[:, e[5]:e[6]]]
    z, xbc_raw, dt_raw, q, k, v = _inproj(x, mod, w_parts)

    state_in = state_ssm[:, 0].reshape(DEC_BATCH, 2, SSM_INNER, D_STATE)
    y_ssm, h_f, h_b = _ssd_mixer(z, xbc_raw, dt_raw, conv_w_a[0], conv_b_a[0], a_log[0], dt_bias[0],
                                 d_skip[0], ssm_norm_w[0], state_in)

    sink = attn_sink[0]
    att_p = _attend_context(q, k, v, sink)
    qr, kr, vb = _rope_latent(q, k, v)
    k_ctx = cache_k[:, 0].reshape(DEC_BATCH, PAST_LEN, KV_DIM).astype(bf16)
    v_ctx = cache_v[:, 0].reshape(DEC_BATCH, PAST_LEN, KV_DIM).astype(bf16)
    att_s = _attend_latent(qr, kr, vb, k_ctx, v_ctx, sink)
    y_att = jnp.concatenate([att_p, att_s], axis=0)

    wo = w_out_a[0].astype(bf16)
    ln = jnp.stack([ln_w[0, 0], ln_b[0, 0]])
    x1, u2, *routing = _outproj([y_ssm, y_att], [wo[:SSM_INNER], wo[SSM_INNER:]], x, mod, ln,
                                w_router[0], b_router[0])
    x = _moe_layer(0, x1, u2, routing, mod, jnp.stack([ln_w[0, 1], ln_b[0, 1]]),
                   w_gate, b_gate, w_lin, b_lin, w_down, b_down)

    mod = mod_all[1]
    wc = w_in_c[0].astype(bf16)
    bg, g = _inproj(x, mod, [wc[:, :D_MODEL], wc[:, D_MODEL:2 * D_MODEL], wc[:, 2 * D_MODEL:]], gate_product=True)
    y_c = _gated_conv(bg, g, conv_w_c[0])
    ln = jnp.stack([ln_w[1, 0], ln_b[1, 0]])
    x1, u2, *routing = _outproj([y_c], [w_out_c[0].astype(bf16)], x, mod, ln, w_router[1], b_router[1])
    x = _moe_layer(1, x1, u2, routing, mod, jnp.stack([ln_w[1, 1], ln_b[1, 1]]),
                   w_gate, b_gate, w_lin, b_lin, w_down, b_down)

    y_prompt = x[:N_PROMPT].reshape(BATCH, SEQ, D_MODEL)
    y_sample = x[N_PROMPT:].reshape(DEC_BATCH, DEC_SEQ, D_MODEL)
    new_k = k[:N_PROMPT].reshape(BATCH, 1, SEQ, ATT_KV_HEADS, ATT_HEAD_DIM)
    new_v = v[:N_PROMPT].reshape(BATCH, 1, SEQ, ATT_KV_HEADS, ATT_HEAD_DIM)
    new_state = jnp.stack([h_f, h_b], axis=1).reshape(BATCH, 1, 2, SSM_HEADS, SSM_HEAD_DIM, D_STATE)
    return (y_prompt, y_sample, new_k, new_v, new_state)
```

```python
import functools

import numpy as np
import jax
import jax.numpy as jnp
from jax import lax
from jax.experimental import pallas as pl
from jax.experimental.pallas import tpu as pltpu

D_MODEL = 1024
BATCH = 32
SEQ = 256
DEPTH = 2
DEC_BATCH = 4
DEC_SEQ = 1024
PAST_LEN = 512
GRID_W = 64
SSM_HEAD_DIM = 64
SSM_INNER = D_MODEL
SSM_HEADS = SSM_INNER // SSM_HEAD_DIM
SSM_GROUPS = 2
SSM_HEADS_PER_GROUP = SSM_HEADS // SSM_GROUPS
D_STATE = 128
SSM_CONV = 5
SSM_CHUNK = 128
SSM_BC = SSM_GROUPS * D_STATE
SSM_CONV_DIM = SSM_INNER + 2 * SSM_BC
ATT_HEAD_DIM = 64
ATT_HEADS = D_MODEL // ATT_HEAD_DIM
ATT_KV_HEADS = 4
ATT_GROUP = ATT_HEADS // ATT_KV_HEADS
WINDOW = 128
ATT_BLOCK = 128
ATT_SCALE = ATT_HEAD_DIM ** -0.5
ROPE_BASE = 10000.0
ROPE_HALF = ATT_HEAD_DIM // 2
ROPE_QUARTER = ATT_HEAD_DIM // 4
SHORT_CONV = 3
N_EXPERTS = 32
TOP_K = 4
D_EXPERT = D_MODEL
SWIGLU_ALPHA = 1.702
SWIGLU_LIMIT = 7.0
N_EVEN = (DEPTH + 1) // 2
DEEPNORM_ALPHA = (2 * DEPTH) ** 0.25
LN_EPS = 1e-5
RMS_EPS = 1e-5
KV_DIM = ATT_KV_HEADS * ATT_HEAD_DIM

N_PROMPT = BATCH * SEQ
N_SAMPLE = DEC_BATCH * DEC_SEQ
N_TOK = N_PROMPT + N_SAMPLE
N_COND = 1 + DEC_BATCH
SUBLANES = 8
LANES = 128
COND_PAD = SUBLANES
MOD_ROWS = SUBLANES
DT_PAD = LANES

TOK_TILE = 512
CONV_TILE = 256
MOE_TILE = 256
MOE_ROWS = N_TOK * TOP_K + N_EXPERTS * MOE_TILE
MOE_TILES = MOE_ROWS // MOE_TILE
VMEM_LIMIT = 56 * 1024 * 1024
NEG_BIG = -1e30
WEIGHT_DMA_PRIORITY = 1
ROW_TILES = D_MODEL // LANES

assert N_PROMPT % TOK_TILE == 0 and DEC_SEQ % TOK_TILE == 0
assert SEQ % CONV_TILE == 0 and DEC_SEQ % CONV_TILE == 0

_NT = (((1,), (1,)), ((), ()))
_TN = (((0,), (0,)), ((), ()))


def _cparams(sem):
    return pltpu.CompilerParams(dimension_semantics=sem, vmem_limit_bytes=VMEM_LIMIT)


def _cond_row(i):
    first = N_PROMPT // TOK_TILE
    per = DEC_SEQ // TOK_TILE
    return jnp.where(i < first, 0, 1 + (i - first) // per)


def _bf16_dot(a, b):
    return jnp.dot(a.astype(jnp.bfloat16), b.astype(jnp.bfloat16), preferred_element_type=jnp.float32)


def _mod_kernel(c_ref, w_ref, b_ref, o_ref):
    c = c_ref[...]
    s = c * jax.nn.sigmoid(c)
    o_ref[...] = _bf16_dot(s, w_ref[...]) + b_ref[...]


def _modulation(cond, w_mod, b_mod):
    tn = 1536
    return pl.pallas_call(
        _mod_kernel,
        grid=(DEPTH, 6 * D_MODEL // tn),
        in_specs=[
            pl.BlockSpec((COND_PAD, D_MODEL), lambda l, n: (0, 0)),
            pl.BlockSpec((None, D_MODEL, tn), lambda l, n: (l, 0, n)),
            pl.BlockSpec((None, 1, tn), lambda l, n: (l, 0, n)),
        ],
        out_specs=pl.BlockSpec((None, COND_PAD, tn), lambda l, n: (l, 0, n)),
        out_shape=jax.ShapeDtypeStruct((DEPTH, COND_PAD, 6 * D_MODEL), jnp.float32),
        compiler_params=_cparams(("arbitrary", "arbitrary")),
        name="modulation",
    )(cond, w_mod, b_mod.reshape(DEPTH, 1, 6 * D_MODEL))


def _inproj_kernel(x_ref, mod_ref, *refs, gate_product):
    n = len(refs) // 2 + (1 if gate_product else 0)
    w_refs, o_refs = refs[:n], refs[n:]
    u = x_ref[...] * (1.0 + mod_ref[1:2, :]) + mod_ref[0:1, :]
    ub = u.astype(jnp.bfloat16)
    outs = [jnp.dot(ub, w_ref[...], preferred_element_type=jnp.float32) for w_ref in w_refs]
    if gate_product:
        outs = outs[:-2] + [outs[-2] * outs[-1]]
    for o_ref, o in zip(o_refs, outs):
        o_ref[...] = o


def _inproj(x, mod, weights, gate_product=False):
    out_w = [w.shape[1] for w in weights]
    if gate_product:
        out_w = out_w[:-1]
    return pl.pallas_call(
        functools.partial(_inproj_kernel, gate_product=gate_product),
        grid=(N_TOK // TOK_TILE,),
        in_specs=[
            pl.BlockSpec((TOK_TILE, D_MODEL), lambda i: (i, 0)),
            pl.BlockSpec((None, MOD_ROWS, D_MODEL), lambda i: (_cond_row(i), 0, 0)),
        ] + [pl.BlockSpec(w.shape, lambda i: (0, 0)) for w in weights],
        out_specs=[pl.BlockSpec((TOK_TILE, n), lambda i: (i, 0)) for n in out_w],
        out_shape=[jax.ShapeDtypeStruct((N_TOK, n), jnp.float32) for n in out_w],
        compiler_params=_cparams(("arbitrary",)),
        name="inproj",
    )(x, mod, *weights)


def _layer_norm(v, w, b):
    mu = jnp.mean(v, axis=-1, keepdims=True)
    d = v - mu
    var = jnp.mean(d * d, axis=-1, keepdims=True)
    return d * lax.rsqrt(var + LN_EPS) * w + b


def _outproj_kernel(*refs, n_in):
    y_refs = refs[:n_in]
    w_refs = refs[n_in:2 * n_in]
    (x_ref, mod_ref, ln_ref, wr_ref, br_ref,
     x1_ref, u2_ref, idx_ref, gate_ref, rank_ref, cnt_ref, run_ref) = refs[2 * n_in:]
    f32 = jnp.float32

    @pl.when(pl.program_id(0) == 0)
    def _():
        run_ref[...] = jnp.zeros_like(run_ref)

    mix = _bf16_dot(y_refs[0][...], w_refs[0][...])
    for y_ref, w_ref in zip(y_refs[1:], w_refs[1:]):
        mix = mix + _bf16_dot(y_ref[...], w_ref[...])
    x1 = _layer_norm(DEEPNORM_ALPHA * x_ref[...] + mod_ref[2:3, :] * mix, ln_ref[0:1, :], ln_ref[1:2, :])
    x1_ref[...] = x1
    u2 = x1 * (1.0 + mod_ref[4:5, :]) + mod_ref[3:4, :]
    u2_ref[...] = u2
    logits = jnp.dot(u2, wr_ref[...], preferred_element_type=f32, precision=lax.Precision.HIGHEST) + br_ref[...]

    lane = lax.broadcasted_iota(jnp.int32, (TOK_TILE, N_EXPERTS), 1)
    work = logits
    vals, sels, idxs = [], [], []
    for _ in range(TOP_K):
        m = jnp.max(work, axis=-1, keepdims=True)
        idx = jnp.min(jnp.where(work == m, lane, N_EXPERTS), axis=-1, keepdims=True)
        sel = lane == idx
        vals.append(m)
        idxs.append(idx)
        sels.append(sel)
        work = jnp.where(sel, -jnp.inf, work)
    exps = [jnp.exp(v - vals[0]) for v in vals]
    den = exps[0]
    for e in exps[1:]:
        den = den + e
    onehot = sels[0].astype(f32)
    for sel in sels[1:]:
        onehot = onehot + sel.astype(f32)
    r = lax.broadcasted_iota(jnp.int32, (TOK_TILE, TOK_TILE), 0)
    c = lax.broadcasted_iota(jnp.int32, (TOK_TILE, TOK_TILE), 1)
    before = jnp.dot((r > c).astype(jnp.bfloat16), onehot.astype(jnp.bfloat16), preferred_element_type=f32)
    rank_all = before + run_ref[...]
    run = run_ref[...] + jnp.sum(onehot, axis=0, keepdims=True)
    run_ref[...] = run
    cnt_ref[...] = run.astype(jnp.int32)
    col = lax.broadcasted_iota(jnp.int32, (TOK_TILE, TOP_K), 1)
    idx_o = jnp.zeros((TOK_TILE, TOP_K), jnp.int32)
    gate_o = jnp.zeros((TOK_TILE, TOP_K), f32)
    rank_o = jnp.zeros((TOK_TILE, TOP_K), f32)
    for k in range(TOP_K):
        rank_k = jnp.sum(jnp.where(sels[k], rank_all, 0.0), axis=-1, keepdims=True)
        idx_o = jnp.where(col == k, idxs[k], idx_o)
        gate_o = jnp.where(col == k, exps[k] / den, gate_o)
        rank_o = jnp.where(col == k, rank_k, rank_o)
    idx_ref[...] = idx_o
    gate_ref[...] = gate_o
    rank_ref[...] = rank_o.astype(jnp.int32)


def _outproj(ys, ws, x, mod, ln, w_router, b_router):
    n_in = len(ys)
    return pl.pallas_call(
        functools.partial(_outproj_kernel, n_in=n_in),
        grid=(N_TOK // TOK_TILE,),
        in_specs=[pl.BlockSpec((TOK_TILE, y.shape[1]), lambda i: (i, 0)) for y in ys]
        + [pl.BlockSpec(w.shape, lambda i: (0, 0)) for w in ws]
        + [
            pl.BlockSpec((TOK_TILE, D_MODEL), lambda i: (i, 0)),
            pl.BlockSpec((None, MOD_ROWS, D_MODEL), lambda i: (_cond_row(i), 0, 0)),
            pl.BlockSpec((2, D_MODEL), lambda i: (0, 0)),
            pl.BlockSpec((D_MODEL, N_EXPERTS), lambda i: (0, 0)),
            pl.BlockSpec((1, N_EXPERTS), lambda i: (0, 0)),
        ],
        out_specs=[
            pl.BlockSpec((TOK_TILE, D_MODEL), lambda i: (i, 0)),
            pl.BlockSpec((TOK_TILE, D_MODEL), lambda i: (i, 0)),
            pl.BlockSpec((TOK_TILE, TOP_K), lambda i: (i, 0)),
            pl.BlockSpec((TOK_TILE, TOP_K), lambda i: (i, 0)),
            pl.BlockSpec((TOK_TILE, TOP_K), lambda i: (i, 0)),
            pl.BlockSpec((1, N_EXPERTS), lambda i: (0, 0)),
        ],
        out_shape=[
            jax.ShapeDtypeStruct((N_TOK, D_MODEL), jnp.float32),
            jax.ShapeDtypeStruct((N_TOK, D_MODEL), jnp.float32),
            jax.ShapeDtypeStruct((N_TOK, TOP_K), jnp.int32),
            jax.ShapeDtypeStruct((N_TOK, TOP_K), jnp.float32),
            jax.ShapeDtypeStruct((N_TOK, TOP_K), jnp.int32),
            jax.ShapeDtypeStruct((1, N_EXPERTS), jnp.int32),
        ],
        scratch_shapes=[pltpu.VMEM((1, N_EXPERTS), jnp.float32)],
        compiler_params=_cparams(("arbitrary",)),
        name="outproj_ln_router",
    )(*ys, *ws, x, mod, ln, w_router, b_router.reshape(1, N_EXPERTS))


def _moe_kernel(te_ref, nv_ref, src0_ref, last_ref, wslot_ref, nexte_ref, stok_ref, x_hbm,
                wg_hbm, bg_ref, wl_hbm, bl_ref, wd_hbm, bd_ref, o_ref,
                w_bf, xbuf, sem, wst, wsem, *, layer):
    i = pl.program_id(0)
    nv = nv_ref[0]
    valid = i < nv
    prev = te_ref[jnp.maximum(i - 1, 0)]
    new_expert = jnp.logical_or(i == 0, te_ref[i] != prev)

    def start_gather(tile, buf):
        base, last = src0_ref[tile], last_ref[tile]
        for r in range(MOE_TILE):
            tok = stok_ref[jnp.minimum(base + r, last)]
            pltpu.make_async_copy(x_hbm.at[tok], xbuf.at[buf, r], sem.at[buf]).start()

    def wait_gather(buf):
        pltpu.make_async_copy(x_hbm.at[pl.ds(0, MOE_TILE)], xbuf.at[buf], sem.at[buf]).wait()

    def weight_copies(expert, slot):
        return [pltpu.make_async_copy(w.at[layer, expert], wst.at[slot, j], wsem.at[slot])
                for j, w in enumerate((wg_hbm, wl_hbm, wd_hbm))]

    @pl.when(i == 0)
    def _():
        start_gather(0, 0)
        for cp in weight_copies(te_ref[0], 0):
            cp.start(priority=WEIGHT_DMA_PRIORITY)

    @pl.when(jnp.logical_and(valid, new_expert))
    def _():
        slot = wslot_ref[i]
        nxt = nexte_ref[i]

        @pl.when(nxt >= 0)
        def _():
            for cp in weight_copies(nxt, 1 - slot):
                cp.start(priority=WEIGHT_DMA_PRIORITY)

        for cp in weight_copies(te_ref[i], slot):
            cp.wait()
        for j in range(3):
            w_bf[j] = wst[slot, j].astype(jnp.bfloat16)

    for buf in range(2):
        @pl.when(jnp.logical_and(valid, i % 2 == buf))
        def _():
            wait_gather(buf)
            start_gather(jnp.minimum(i + 1, nv - 1), 1 - buf)
            x = jnp.concatenate([xbuf[buf, :, s, :] for s in range(ROW_TILES)], axis=1).astype(jnp.bfloat16)
            hg = jnp.dot(x, w_bf[0], preferred_element_type=jnp.float32) + bg_ref[...]
            hl = jnp.dot(x, w_bf[1], preferred_element_type=jnp.float32) + bl_ref[...]
            hg = jnp.minimum(hg, SWIGLU_LIMIT)
            hl = jnp.clip(hl, -SWIGLU_LIMIT, SWIGLU_LIMIT)
            h = hg * jax.nn.sigmoid(SWIGLU_ALPHA * hg) * (hl + 1.0)
            o_ref[...] = (jnp.dot(h.astype(jnp.bfloat16), w_bf[2], preferred_element_type=jnp.float32)
                          + bd_ref[...])

        @pl.when(jnp.logical_and(i == nv - 1, i % 2 == buf))
        def _():
            wait_gather(1 - buf)

    @pl.when(jnp.logical_not(valid))
    def _():
        o_ref[...] = jnp.zeros_like(o_ref)


def _moe_grouped(layer, tables, sorted_tok, x, w_gate, b_gate, w_lin, b_lin, w_down, b_down):
    hbm = pl.BlockSpec(memory_space=pl.ANY)
    bspec = pl.BlockSpec((None, None, 1, D_EXPERT), lambda i, te, *_: (layer, te[i], 0, 0))
    grid_spec = pltpu.PrefetchScalarGridSpec(
        num_scalar_prefetch=len(tables) + 1,
        grid=(MOE_TILES,),
        in_specs=[hbm, hbm, bspec, hbm, bspec, hbm, bspec],
        out_specs=pl.BlockSpec((MOE_TILE, D_MODEL), lambda i, *_: (i, 0)),
        scratch_shapes=[
            pltpu.VMEM((3, D_MODEL, D_EXPERT), jnp.bfloat16),
            pltpu.VMEM((2, MOE_TILE, ROW_TILES, LANES), jnp.float32), pltpu.SemaphoreType.DMA((2,)),
            pltpu.VMEM((2, 3, D_MODEL, D_EXPERT), jnp.float32), pltpu.SemaphoreType.DMA((2,)),
        ],
    )
    bshape = (DEPTH, N_EXPERTS, 1, D_EXPERT)
    return pl.pallas_call(
        functools.partial(_moe_kernel, layer=layer),
        grid_spec=grid_spec,
        out_shape=jax.ShapeDtypeStruct((MOE_ROWS, D_MODEL), jnp.float32),
        compiler_params=_cparams(("arbitrary",)),
        name="moe_grouped",
    )(*tables, sorted_tok, x.reshape(N_TOK, ROW_TILES, LANES), w_gate, b_gate.reshape(bshape),
      w_lin, b_lin.reshape(bshape), w_down, b_down.reshape(bshape))


def _combine_kernel(y_ref, g_ref, x_ref, mod_ref, ln_ref, o_ref):
    g = g_ref[...]
    ff = g[:, 0:1] * y_ref[0]
    for k in range(1, TOP_K):
        ff = ff + g[:, k:k + 1] * y_ref[k]
    o_ref[...] = _layer_norm(DEEPNORM_ALPHA * x_ref[...] + mod_ref[5:6, :] * ff, ln_ref[0:1, :], ln_ref[1:2, :])


def _combine(yk, gates, x, mod, ln):
    return pl.pallas_call(
        _combine_kernel,
        grid=(N_TOK // TOK_TILE,),
        in_specs=[
            pl.BlockSpec((TOP_K, TOK_TILE, D_MODEL), lambda i: (0, i, 0)),
            pl.BlockSpec((TOK_TILE, TOP_K), lambda i: (i, 0)),
            pl.BlockSpec((TOK_TILE, D_MODEL), lambda i: (i, 0)),
            pl.BlockSpec((None, MOD_ROWS, D_MODEL), lambda i: (_cond_row(i), 0, 0)),
            pl.BlockSpec((2, D_MODEL), lambda i: (0, 0)),
        ],
        out_specs=pl.BlockSpec((TOK_TILE, D_MODEL), lambda i: (i, 0)),
        out_shape=jax.ShapeDtypeStruct((N_TOK, D_MODEL), jnp.float32),
        compiler_params=_cparams(("arbitrary",)),
        name="combine_ln",
    )(yk, gates, x, mod, ln)


def _route(top_idx, rank, counts):
    counts = counts.reshape(N_EXPERTS)
    padded = (counts + MOE_TILE - 1) // MOE_TILE * MOE_TILE
    pend = jnp.cumsum(padded)
    pstart = pend - padded
    start = jnp.cumsum(counts) - counts
    experts = jnp.arange(N_EXPERTS, dtype=jnp.int32)
    lookup = lambda table, idx: jnp.sum(jnp.where(idx[..., None] == experts, table, 0), axis=-1)
    slot = lookup(pstart, top_idx) + rank
    key = lookup(start, top_idx) + rank
    tok = jnp.broadcast_to(jnp.arange(N_TOK, dtype=jnp.int32)[:, None], (N_TOK, TOP_K))
    _, sorted_tok = lax.sort((key.reshape(-1), tok.reshape(-1)), num_keys=1)
    n_valid = (pend[-1] // MOE_TILE).astype(jnp.int32)
    tile_start = jnp.arange(MOE_TILES, dtype=jnp.int32) * MOE_TILE
    tile_expert = jnp.sum((tile_start[:, None] >= pend[None, :]).astype(jnp.int32), axis=1)
    tile_expert = jnp.where(tile_start < pend[-1], tile_expert, tile_expert[n_valid - 1])
    src0 = tile_start - lookup(pstart, tile_expert) + lookup(start, tile_expert)
    src_last = lookup(start, tile_expert) + lookup(counts, tile_expert) - 1
    used = (counts > 0).astype(jnp.int32)
    order = jnp.cumsum(used) - used
    later = jnp.where(jnp.logical_and(experts[None, :] > experts[:, None], used[None, :] > 0),
                      experts[None, :], N_EXPERTS)
    successor = jnp.min(later, axis=1)
    successor = jnp.where(successor < N_EXPERTS, successor, -1)
    wslot = lookup(order, tile_expert) % 2
    next_expert = lookup(successor, tile_expert)
    tables = (tile_expert, n_valid.reshape(1), src0, src_last, wslot, next_expert)
    return slot, sorted_tok, tables


def _moe_layer(i, x1, u2, routing, mod, ln, w_gate, b_gate, w_lin, b_lin, w_down, b_down):
    top_idx, gates, rank, counts = routing
    slot, sorted_tok, tables = _route(top_idx, rank, counts)
    y = _moe_grouped(i, tables, sorted_tok, u2, w_gate, b_gate, w_lin, b_lin, w_down, b_down)
    yk = jnp.take(y, slot.T, axis=0, mode="clip")
    return _combine(yk, gates, x1, mod, ln)


_F_ROW, _F_FIRST, _F_LAST, _F_PREV8, _F_HASPREV, _F_NEXT8, _F_HASNEXT, _F_H0, _F_USEH0, _F_HOUT, _F_WRITEH = range(11)
_N_FIELDS = 11
_SSD_STEPS = N_TOK // SSM_CHUNK


def _ssd_table(reverse):
    rows = []
    for prompt, nseq, slen, base in ((True, BATCH, SEQ, 0), (False, DEC_BATCH, DEC_SEQ, N_PROMPT)):
        nc = slen // SSM_CHUNK
        for s in range(nseq):
            for c in range(nc):
                row = (base + s * slen) // SSM_CHUNK + c
                first, last = (c == nc - 1, c == 0) if reverse else (c == 0, c == nc - 1)
                per = SSM_CHUNK // SUBLANES
                rows.append([
                    row, int(first), int(last),
                    max(row * per - 1, 0), int(c > 0),
                    min(row * per + per, N_TOK // SUBLANES - 1), int(c < nc - 1),
                    0 if prompt else s, int(not prompt),
                    s if prompt else BATCH - 1, int(prompt and last),
                ])
    if reverse:
        rows = rows[::-1]
    return np.asarray(rows, np.int32).T.reshape(-1)


def _fld(tbl, f, s):
    return tbl[f * _SSD_STEPS + s]


def _exact_expand(a, e_ref, passes):
    k = a.shape[1]
    pieces, rem = [], a
    for _ in range(passes):
        piece = rem.astype(jnp.bfloat16)
        rem = rem - piece.astype(jnp.float32)
        pieces.append(piece)
    out = None
    for i in range(0, passes, 2):
        pair = pieces[i:i + 2]
        lhs = pair[0] if len(pair) == 1 else jnp.concatenate(pair, axis=1)
        t = jnp.dot(lhs, e_ref[0:k * len(pair), :], preferred_element_type=jnp.float32)
        out = t if out is None else out + t
    return out


def _ssd_chunk(xs, bmat, cmat, dt_raw, dtb_row, alog_row, ex_ref, eb_ref, s_ref, reverse):
    f32, bf16 = jnp.float32, jnp.bfloat16
    base = SSM_HEADS if reverse else 0
    lane = lax.broadcasted_iota(jnp.int32, (1, DT_PAD), 1)
    head_cols = jnp.logical_and(lane >= base, lane < base + SSM_HEADS)
    xr = dt_raw + dtb_row
    dt = jnp.maximum(xr, 0.0) + jnp.log1p(jnp.exp(-jnp.abs(xr)))
    a_row = jnp.where(head_cols, -jnp.exp(alog_row), 0.0)
    d_a = dt * a_row
    r = lax.broadcasted_iota(jnp.int32, (SSM_CHUNK, SSM_CHUNK), 0)
    c = lax.broadcasted_iota(jnp.int32, (SSM_CHUNK, SSM_CHUNK), 1)
    tri = (r <= c) if reverse else (r >= c)
    cum = jnp.dot(tri.astype(f32), d_a, preferred_element_type=f32, precision=lax.Precision.HIGHEST)
    cum_t = cum.T
    total = cum[0:1, :] if reverse else cum[SSM_CHUNK - 1:SSM_CHUNK, :]
    stacked = jnp.concatenate([dt, jnp.exp(cum), dt * jnp.exp(total - cum),
                               jnp.broadcast_to(jnp.exp(total), (SUBLANES, DT_PAD))], axis=0)
    stacked_x = _exact_expand(stacked, ex_ref, 2)
    dt_x = stacked_x[0:SSM_CHUNK]
    e_in_x = stacked_x[SSM_CHUNK:2 * SSM_CHUNK]
    w_end_x = stacked_x[2 * SSM_CHUNK:3 * SSM_CHUNK]
    dec_x = stacked_x[3 * SSM_CHUNK:3 * SSM_CHUNK + 1]
    cum_b = _exact_expand(cum, eb_ref, 3)
    xdt = xs * dt_x
    xw = (xs * w_end_x).astype(bf16)
    s_in = s_ref[...]
    s_bf = s_in.astype(bf16)
    lo_half = lax.broadcasted_iota(jnp.int32, (1, 2 * SSM_HEAD_DIM), 1) < SSM_HEAD_DIM
    gw = SSM_HEADS_PER_GROUP * SSM_HEAD_DIM
    pieces = []
    for g in range(SSM_GROUPS):
        bg = bmat[:, g * D_STATE:(g + 1) * D_STATE]
        cg = cmat[:, g * D_STATE:(g + 1) * D_STATE]
        gcols = slice(g * gw, (g + 1) * gw)
        cb = lax.dot_general(cg, bg, _NT, preferred_element_type=f32)
        y_off = jnp.dot(cg, s_bf[:, gcols], preferred_element_type=f32) * e_in_x[:, gcols]
        upd = lax.dot_general(bg, xw[:, gcols], _TN, preferred_element_type=f32)
        s_ref[:, gcols] = s_in[:, gcols] * dec_x[:, gcols] + upd
        for q in range(SSM_HEADS_PER_GROUP // 2):
            h0 = g * SSM_HEADS_PER_GROUP + 2 * q
            pcols = slice(h0 * SSM_HEAD_DIM, (h0 + 2) * SSM_HEAD_DIM)
            xp = xdt[:, pcols]
            bd = jnp.concatenate([jnp.where(lo_half, xp, 0.0), jnp.where(lo_half, 0.0, xp)], axis=0).astype(bf16)
            ms = []
            for h in (h0, h0 + 1):
                ch = base + h
                seg = cum_b[:, h * SSM_CHUNK:(h + 1) * SSM_CHUNK] - cum_t[ch:ch + 1, :]
                decay = jnp.exp(jnp.where(tri, seg, NEG_BIG))
                ms.append((cb * decay).astype(bf16))
            y_diag = jnp.dot(jnp.concatenate(ms, axis=1), bd, preferred_element_type=f32)
            pieces.append(y_diag + y_off[:, 2 * q * SSM_HEAD_DIM:(2 * q + 2) * SSM_HEAD_DIM])
    return jnp.concatenate(pieces, axis=1)


def _ssd_init_state(tbl, step, h0_ref, s_ref):
    @pl.when(_fld(tbl, _F_FIRST, step) == 1)
    def _():
        use = _fld(tbl, _F_USEH0, step) == 1
        s_ref[...] = jnp.where(use, h0_ref[...], 0.0).T


def _ssd_write_state(tbl, step, hout_ref, s_ref):
    @pl.when(_fld(tbl, _F_WRITEH, step) == 1)
    def _():
        hout_ref[...] = s_ref[...].T


def _ssd_fwd_kernel(tbl, xc_ref, xp_ref, xn_ref, dt_ref, cw_ref, cbias_ref, dtb_ref, alog_ref, ex_ref, eb_ref,
                    h0_ref, xs_ref, bc_ref, yf_ref, hout_ref, s_ref):
    step = pl.program_id(0)
    _ssd_init_state(tbl, step, h0_ref, s_ref)
    hp = (_fld(tbl, _F_HASPREV, step) == 1).astype(jnp.float32)
    hn = (_fld(tbl, _F_HASNEXT, step) == 1).astype(jnp.float32)
    window = jnp.concatenate([xp_ref[...] * hp, xc_ref[...], xn_ref[...] * hn], axis=0)
    acc = cbias_ref[...] + cw_ref[0:1, :] * window[SUBLANES - 2:SUBLANES - 2 + SSM_CHUNK, :]
    for k in range(1, SSM_CONV):
        off = SUBLANES - SSM_CONV // 2 + k
        acc = acc + cw_ref[k:k + 1, :] * window[off:off + SSM_CHUNK, :]
    xbc = acc * jax.nn.sigmoid(acc)
    xs = xbc[:, :SSM_INNER]
    bc = xbc[:, SSM_INNER:].astype(jnp.bfloat16)
    xs_ref[...] = xs
    bc_ref[...] = bc
    yf_ref[...] = _ssd_chunk(xs, bc[:, :SSM_BC], bc[:, SSM_BC:], dt_ref[...], dtb_ref[...], alog_ref[...],
                             ex_ref, eb_ref, s_ref, reverse=False)
    _ssd_write_state(tbl, step, hout_ref, s_ref)


def _ssd_bwd_kernel(tbl, xs_ref, bc_ref, dt_ref, yf_ref, z_ref, dtb_ref, alog_ref, dskip_ref, nw_ref,
                    ex_ref, eb_ref, h0_ref, y_ref, hout_ref, s_ref):
    step = pl.program_id(0)
    _ssd_init_state(tbl, step, h0_ref, s_ref)
    xs = xs_ref[...]
    bc = bc_ref[...]
    yb = _ssd_chunk(xs, bc[:, :SSM_BC], bc[:, SSM_BC:], dt_ref[...], dtb_ref[...], alog_ref[...],
                    ex_ref, eb_ref, s_ref, reverse=True)
    z = z_ref[...]
    hg = (yf_ref[...] + yb + dskip_ref[...] * xs) * (z * jax.nn.sigmoid(z))
    gw = SSM_INNER // SSM_GROUPS
    outs = []
    for g in range(SSM_GROUPS):
        hgg = hg[:, g * gw:(g + 1) * gw]
        outs.append(hgg * lax.rsqrt(jnp.mean(hgg * hgg, axis=-1, keepdims=True) + RMS_EPS))
    y_ref[...] = (jnp.concatenate(outs, axis=1) * nw_ref[...]).astype(jnp.bfloat16)
    _ssd_write_state(tbl, step, hout_ref, s_ref)


def _ssd_mixer(z, xbc_raw, dt_raw, conv_w, conv_b, a_log, dt_bias, d_skip, norm_w, state_in):
    f32 = jnp.float32
    row = lambda f: (lambda s, tbl: (_fld(tbl, f, s), 0))
    chunk_spec = lambda w: pl.BlockSpec((SSM_CHUNK, w), row(_F_ROW))
    const_spec = lambda shape: pl.BlockSpec(shape, lambda s, tbl: (0,) * len(shape))
    h0_spec = lambda d: pl.BlockSpec((None, None, SSM_INNER, D_STATE), lambda s, tbl: (_fld(tbl, _F_H0, s), d, 0, 0))
    hout_spec = pl.BlockSpec((None, SSM_INNER, D_STATE), lambda s, tbl: (_fld(tbl, _F_HOUT, s), 0, 0))
    hout_shape = jax.ShapeDtypeStruct((BATCH, SSM_INNER, D_STATE), f32)
    state_scratch = [pltpu.VMEM((D_STATE, SSM_INNER), f32)]

    def expanders(reverse):
        base = SSM_HEADS if reverse else 0
        ex = np.zeros((DT_PAD, SSM_INNER), np.float32)
        eb = np.zeros((DT_PAD, SSM_HEADS * SSM_CHUNK), np.float32)
        for h in range(SSM_HEADS):
            ex[base + h, h * SSM_HEAD_DIM:(h + 1) * SSM_HEAD_DIM] = 1.0
            eb[base + h, h * SSM_CHUNK:(h + 1) * SSM_CHUNK] = 1.0
        twice = lambda e: jnp.asarray(np.concatenate([e, e], axis=0), jnp.bfloat16)
        return twice(ex), twice(eb)

    expander_specs = [const_spec((2 * DT_PAD, SSM_INNER)), const_spec((2 * DT_PAD, SSM_HEADS * SSM_CHUNK))]

    cw = jnp.pad(conv_w, ((0, SUBLANES - SSM_CONV), (0, 0)))
    cbias = conv_b.reshape(1, SSM_CONV_DIM)
    dtb = jnp.pad(dt_bias.reshape(1, 2 * SSM_HEADS), ((0, 0), (0, DT_PAD - 2 * SSM_HEADS)))
    alog = jnp.pad(a_log.reshape(1, 2 * SSM_HEADS), ((0, 0), (0, DT_PAD - 2 * SSM_HEADS)))
    dskip = jnp.repeat(d_skip, SSM_HEAD_DIM).reshape(1, SSM_INNER)
    nw = norm_w.reshape(1, SSM_INNER)

    xs, bc, yf, h_f = pl.pallas_call(
        _ssd_fwd_kernel,
        grid_spec=pltpu.PrefetchScalarGridSpec(
            num_scalar_prefetch=1,
            grid=(_SSD_STEPS,),
            in_specs=[
                chunk_spec(SSM_CONV_DIM),
                pl.BlockSpec((SUBLANES, SSM_CONV_DIM), row(_F_PREV8)),
                pl.BlockSpec((SUBLANES, SSM_CONV_DIM), row(_F_NEXT8)),
                chunk_spec(DT_PAD),
                const_spec((SUBLANES, SSM_CONV_DIM)), const_spec((1, SSM_CONV_DIM)),
                const_spec((1, DT_PAD)), const_spec((1, DT_PAD)), *expander_specs,
                h0_spec(0),
            ],
            out_specs=[chunk_spec(SSM_INNER), chunk_spec(2 * SSM_BC), chunk_spec(SSM_INNER), hout_spec],
            scratch_shapes=state_scratch,
        ),
        out_shape=[
            jax.ShapeDtypeStruct((N_TOK, SSM_INNER), f32),
            jax.ShapeDtypeStruct((N_TOK, 2 * SSM_BC), jnp.bfloat16),
            jax.ShapeDtypeStruct((N_TOK, SSM_INNER), f32),
            hout_shape,
        ],
        compiler_params=_cparams(("arbitrary",)),
        name="ssd_forward",
    )(jnp.asarray(_ssd_table(False)), xbc_raw, xbc_raw, xbc_raw, dt_raw, cw, cbias, dtb, alog,
      *expanders(False), state_in)

    y, h_b = pl.pallas_call(
        _ssd_bwd_kernel,
        grid_spec=pltpu.PrefetchScalarGridSpec(
            num_scalar_prefetch=1,
            grid=(_SSD_STEPS,),
            in_specs=[
                chunk_spec(SSM_INNER), chunk_spec(2 * SSM_BC), chunk_spec(DT_PAD), chunk_spec(SSM_INNER),
                chunk_spec(SSM_INNER),
                const_spec((1, DT_PAD)), const_spec((1, DT_PAD)),
                const_spec((1, SSM_INNER)), const_spec((1, SSM_INNER)), *expander_specs,
                h0_spec(1),
            ],
            out_specs=[chunk_spec(SSM_INNER), hout_spec],
            scratch_shapes=state_scratch,
        ),
        out_shape=[jax.ShapeDtypeStruct((N_TOK, SSM_INNER), jnp.bfloat16), hout_shape],
        compiler_params=_cparams(("arbitrary",)),
        name="ssd_backward",
    )(jnp.asarray(_ssd_table(True)), xs, bc, dt_raw, yf, z, dtb, alog, dskip, nw, *expanders(True), state_in)
    return y, h_f, h_b


def _sink_attention(q_heads, keys, vals, sink_ref, kv, masks):
    outs = []
    for g, qh in enumerate(q_heads):
        sk = sink_ref[kv * ATT_GROUP + g]
        scores = []
        for kk, mask in zip(keys, masks):
            s = lax.dot_general(qh, kk, _NT, preferred_element_type=jnp.float32) * ATT_SCALE
            scores.append(s if mask is None else jnp.where(mask, s, NEG_BIG))
        m = sk
        for s in scores:
            m = jnp.maximum(m, jnp.max(s, axis=-1, keepdims=True))
        den = jnp.exp(sk - m)
        acc = None
        for s, vv in zip(scores, vals):
            p = jnp.exp(s - m)
            den = den + jnp.sum(p, axis=-1, keepdims=True)
            pv = jnp.dot(p.astype(jnp.bfloat16), vv, preferred_element_type=jnp.float32)
            acc = pv if acc is None else acc + pv
        outs.append(acc / den)
    return outs


def _head(x, h):
    return x[:, h * ATT_HEAD_DIM:(h + 1) * ATT_HEAD_DIM]


def _attn_ctx_kernel(sink_ref, q_ref, k_ref, v_ref, o_ref):
    bf16 = jnp.bfloat16
    q = q_ref[...].astype(bf16)
    k = k_ref[...].astype(bf16)
    v = v_ref[...].astype(bf16)
    outs = []
    for kv in range(ATT_KV_HEADS):
        qs = [_head(q, kv * ATT_GROUP + g) for g in range(ATT_GROUP)]
        outs += _sink_attention(qs, [_head(k, kv)], [_head(v, kv)], sink_ref, kv, [None])
    o_ref[...] = jnp.concatenate(outs, axis=1).astype(bf16)


def _attend_context(q, k, v, sink):
    return pl.pallas_call(
        _attn_ctx_kernel,
        grid=(BATCH,),
        in_specs=[
            pl.BlockSpec(memory_space=pltpu.SMEM),
            pl.BlockSpec((SEQ, D_MODEL), lambda b: (b, 0)),
            pl.BlockSpec((SEQ, KV_DIM), lambda b: (b, 0)),
            pl.BlockSpec((SEQ, KV_DIM), lambda b: (b, 0)),
        ],
        out_specs=pl.BlockSpec((SEQ, D_MODEL), lambda b: (b, 0)),
        out_shape=jax.ShapeDtypeStruct((N_PROMPT, D_MODEL), jnp.bfloat16),
        compiler_params=_cparams(("arbitrary",)),
        name="attend_context",
    )(sink, q, k, v)


def _rope_tables(width):
    t = np.arange(DEC_SEQ)
    d = np.arange(width) % ATT_HEAD_DIM
    pos = np.where(d[None, :] < ROPE_HALF, (t // GRID_W)[:, None], (t % GRID_W)[:, None]).astype(np.float32)
    inv = (ROPE_BASE ** (-np.arange(ROPE_QUARTER, dtype=np.float32) / ROPE_QUARTER)).astype(np.float32)
    ang = pos * inv[d % ROPE_QUARTER][None, :]
    sign = np.where((d % ROPE_HALF) < ROPE_QUARTER, -1.0, 1.0).astype(np.float32)
    return jnp.asarray(np.cos(ang), jnp.float32), jnp.asarray(np.sin(ang) * sign[None, :], jnp.float32)


def _rope(x, cos, sin_signed):
    width = x.shape[1]
    lane = lax.broadcasted_iota(jnp.int32, (1, width), 1)
    first = (lane % ROPE_HALF) < ROPE_QUARTER
    partner = jnp.where(first, pltpu.roll(x, width - ROPE_QUARTER, 1), pltpu.roll(x, ROPE_QUARTER, 1))
    return x * cos + partner * sin_signed


def _rope_kernel(q_ref, k_ref, v_ref, cos_ref, sin_ref, qo_ref, ko_ref, vo_ref):
    bf16 = jnp.bfloat16
    cos, sin = cos_ref[...], sin_ref[...]
    qo_ref[...] = _rope(q_ref[...], cos, sin).astype(bf16)
    ko_ref[...] = _rope(k_ref[...], cos[:, :KV_DIM], sin[:, :KV_DIM]).astype(bf16)
    vo_ref[...] = v_ref[...].astype(bf16)


def _rope_latent(q, k, v):
    nb = DEC_SEQ // ATT_BLOCK
    off = N_PROMPT // ATT_BLOCK
    cos, sin = _rope_tables(D_MODEL)
    tok = lambda b, i: (off + b * nb + i, 0)
    out = lambda b, i: (b * nb + i, 0)
    return pl.pallas_call(
        _rope_kernel,
        grid=(DEC_BATCH, nb),
        in_specs=[
            pl.BlockSpec((ATT_BLOCK, D_MODEL), tok),
            pl.BlockSpec((ATT_BLOCK, KV_DIM), tok),
            pl.BlockSpec((ATT_BLOCK, KV_DIM), tok),
            pl.BlockSpec((ATT_BLOCK, D_MODEL), lambda b, i: (i, 0)),
            pl.BlockSpec((ATT_BLOCK, D_MODEL), lambda b, i: (i, 0)),
        ],
        out_specs=[
            pl.BlockSpec((ATT_BLOCK, D_MODEL), out),
            pl.BlockSpec((ATT_BLOCK, KV_DIM), out),
            pl.BlockSpec((ATT_BLOCK, KV_DIM), out),
        ],
        out_shape=[
            jax.ShapeDtypeStruct((N_SAMPLE, D_MODEL), jnp.bfloat16),
            jax.ShapeDtypeStruct((N_SAMPLE, KV_DIM), jnp.bfloat16),
            jax.ShapeDtypeStruct((N_SAMPLE, KV_DIM), jnp.bfloat16),
        ],
        compiler_params=_cparams(("arbitrary", "arbitrary")),
        name="rope_latent",
    )(q, k, v, cos, sin)


def _attn_lat_kernel(sink_ref, q_ref, kp_ref, kc_ref, kn_ref, vp_ref, vc_ref, vn_ref, kx_ref, vx_ref, o_ref):
    i = pl.program_id(1)
    nb = pl.num_programs(1)
    q = q_ref[...]
    k_loc = jnp.concatenate([kp_ref[...], kc_ref[...], kn_ref[...]], axis=0)
    v_loc = jnp.concatenate([vp_ref[...], vc_ref[...], vn_ref[...]], axis=0)
    kx, vx = kx_ref[...], vx_ref[...]
    r = lax.broadcasted_iota(jnp.int32, (ATT_BLOCK, 3 * ATT_BLOCK), 0)
    c = lax.broadcasted_iota(jnp.int32, (ATT_BLOCK, 3 * ATT_BLOCK), 1)
    rel = c - ATT_BLOCK - r
    in_window = jnp.logical_and(rel >= -WINDOW, rel <= WINDOW)
    in_seq = jnp.logical_and(jnp.logical_or(c >= ATT_BLOCK, i > 0),
                             jnp.logical_or(c < 2 * ATT_BLOCK, i < nb - 1))
    valid = jnp.logical_and(in_window, in_seq)
    outs = []
    for kv in range(ATT_KV_HEADS):
        qs = [_head(q, kv * ATT_GROUP + g) for g in range(ATT_GROUP)]
        outs += _sink_attention(qs, [_head(k_loc, kv), _head(kx, kv)], [_head(v_loc, kv), _head(vx, kv)],
                                sink_ref, kv, [valid, None])
    o_ref[...] = jnp.concatenate(outs, axis=1).astype(jnp.bfloat16)


def _attend_latent(qr, kr, vb, k_ctx, v_ctx, sink):
    nb = DEC_SEQ // ATT_BLOCK
    cur = lambda b, i: (b * nb + i, 0)
    prv = lambda b, i: (b * nb + jnp.maximum(i - 1, 0), 0)
    nxt = lambda b, i: (b * nb + jnp.minimum(i + 1, nb - 1), 0)
    kvs = lambda f: pl.BlockSpec((ATT_BLOCK, KV_DIM), f)
    ctx = pl.BlockSpec((None, PAST_LEN, KV_DIM), lambda b, i: (b, 0, 0))
    return pl.pallas_call(
        _attn_lat_kernel,
        grid=(DEC_BATCH, nb),
        in_specs=[
            pl.BlockSpec(memory_space=pltpu.SMEM),
            pl.BlockSpec((ATT_BLOCK, D_MODEL), cur),
            kvs(prv), kvs(cur), kvs(nxt), kvs(prv), kvs(cur), kvs(nxt), ctx, ctx,
        ],
        out_specs=pl.BlockSpec((ATT_BLOCK, D_MODEL), cur),
        out_shape=jax.ShapeDtypeStruct((N_SAMPLE, D_MODEL), jnp.bfloat16),
        compiler_params=_cparams(("arbitrary", "arbitrary")),
        name="attend_latent",
    )(sink, qr, kr, kr, kr, vb, vb, vb, k_ctx, v_ctx)


def _gconv_kernel(bg_ref, g_ref, gp_ref, gn_ref, w_ref, o_ref):
    i = pl.program_id(0)
    first = N_PROMPT // CONV_TILE
    per = DEC_SEQ // CONV_TILE
    t = (i - first) % per
    latent = i >= first
    hp = jnp.logical_and(latent, t > 0).astype(jnp.float32)
    hn = jnp.logical_and(latent, t < per - 1).astype(jnp.float32)
    window = jnp.concatenate([gp_ref[...] * hp, g_ref[...], gn_ref[...] * hn], axis=0)
    acc = None
    for k in range(SHORT_CONV):
        off = SUBLANES - SHORT_CONV // 2 + k
        term = w_ref[k:k + 1, :] * window[off:off + CONV_TILE, :]
        acc = term if acc is None else acc + term
    o_ref[...] = (bg_ref[...] * acc).astype(jnp.bfloat16)


def _gated_conv(bg, g, conv_w):
    per = CONV_TILE // SUBLANES
    last = N_TOK // SUBLANES - 1
    cw = jnp.pad(conv_w, ((0, SUBLANES - SHORT_CONV), (0, 0)))
    return pl.pallas_call(
        _gconv_kernel,
        grid=(N_TOK // CONV_TILE,),
        in_specs=[
            pl.BlockSpec((CONV_TILE, D_MODEL), lambda i: (i, 0)),
            pl.BlockSpec((CONV_TILE, D_MODEL), lambda i: (i, 0)),
            pl.BlockSpec((SUBLANES, D_MODEL), lambda i: (jnp.maximum(i * per - 1, 0), 0)),
            pl.BlockSpec((SUBLANES, D_MODEL), lambda i: (jnp.minimum(i * per + per, last), 0)),
            pl.BlockSpec((SUBLANES, D_MODEL), lambda i: (0, 0)),
        ],
        out_specs=pl.BlockSpec((CONV_TILE, D_MODEL), lambda i: (i, 0)),
        out_shape=jax.ShapeDtypeStruct((N_TOK, D_MODEL), jnp.bfloat16),
        compiler_params=_cparams(("arbitrary",)),
        name="gated_conv",
    )(bg, g, g, g, cw)


def kernel(x_prompt, x_sample, cache_k, cache_v, state_ssm, c, c_ctx,
           w_mod, b_mod, ln_w, ln_b,
           w_in_a, conv_w_a, conv_b_a, a_log, dt_bias, d_skip, ssm_norm_w, attn_sink, w_out_a,
           w_in_c, conv_w_c, w_out_c,
           w_router, b_router, w_gate, b_gate, w_lin, b_lin, w_down, b_down):
    bf16 = jnp.bfloat16
    x = jnp.concatenate([x_prompt.reshape(N_PROMPT, D_MODEL), x_sample.reshape(N_SAMPLE, D_MODEL)], axis=0)

    cond = jnp.concatenate([c_ctx[None, :], c, jnp.zeros((COND_PAD - N_COND, D_MODEL), jnp.float32)], axis=0)
    mod_all = _modulation(cond, w_mod, b_mod).reshape(DEPTH, COND_PAD, 6, D_MODEL)
    mod_all = jnp.pad(mod_all, ((0, 0), (0, 0), (0, MOD_ROWS - 6), (0, 0)))

    mod = mod_all[0]
    wa = w_in_a[0].astype(bf16)
    e = np.cumsum((0, SSM_INNER, SSM_CONV_DIM, 2 * SSM_HEADS, D_MODEL, KV_DIM, KV_DIM))
    w_dt = jnp.pad(wa[:, e[2]:e[3]], ((0, 0), (0, DT_PAD - 2 * SSM_HEADS)))
    w_parts = [wa[:, e[0]:e[1]], wa[:, e[1]:e[2]], w_dt, wa[:, e[3]:e[4]], wa[:, e[4]:e[5]], wa[:, e[5]:e[6]]]
    z, xbc_raw, dt_raw, q, k, v = _inproj(x, mod, w_parts)

    state_in = state_ssm[:, 0].reshape(DEC_BATCH, 2, SSM_INNER, D_STATE)
    y_ssm, h_f, h_b = _ssd_mixer(z, xbc_raw, dt_raw, conv_w_a[0], conv_b_a[0], a_log[0], dt_bias[0],
                                 d_skip[0], ssm_norm_w[0], state_in)

    sink = attn_sink[0]
    att_p = _attend_context(q, k, v, sink)
    qr, kr, vb = _rope_latent(q, k, v)
    k_ctx = cache_k[:, 0].reshape(DEC_BATCH, PAST_LEN, KV_DIM).astype(bf16)
    v_ctx = cache_v[:, 0].reshape(DEC_BATCH, PAST_LEN, KV_DIM).astype(bf16)
    att_s = _attend_latent(qr, kr, vb, k_ctx, v_ctx, sink)
    y_att = jnp.concatenate([att_p, att_s], axis=0)

    wo = w_out_a[0].astype(bf16)
    ln = jnp.stack([ln_w[0, 0], ln_b[0, 0]])
    x1, u2, *routing = _outproj([y_ssm, y_att], [wo[:SSM_INNER], wo[SSM_INNER:]], x, mod, ln,
                                w_router[0], b_router[0])
    x = _moe_layer(0, x1, u2, routing, mod, jnp.stack([ln_w[0, 1], ln_b[0, 1]]),
                   w_gate, b_gate, w_lin, b_lin, w_down, b_down)

    mod = mod_all[1]
    wc = w_in_c[0].astype(bf16)
    bg, g = _inproj(x, mod, [wc[:, :D_MODEL], wc[:, D_MODEL:2 * D_MODEL], wc[:, 2 * D_MODEL:]], gate_product=True)
    y_c = _gated_conv(bg, g, conv_w_c[0])
    ln = jnp.stack([ln_w[1, 0], ln_b[1, 0]])
    x1, u2, *routing = _outproj([y_c], [w_out_c[0].astype(bf16)], x, mod, ln, w_router[1], b_router[1])
    x = _moe_layer(1, x1, u2, routing, mod, jnp.stack([ln_w[1, 1], ln_b[1, 1]]),
                   w_gate, b_gate, w_lin, b_lin, w_down, b_down)

    y_prompt = x[:N_PROMPT].reshape(BATCH, SEQ, D_MODEL)
    y_sample = x[N_PROMPT:].reshape(DEC_BATCH, DEC_SEQ, D_MODEL)
    new_k = k[:N_PROMPT].reshape(BATCH, 1, SEQ, ATT_KV_HEADS, ATT_HEAD_DIM)
    new_v = v[:N_PROMPT].reshape(BATCH, 1, SEQ, ATT_KV_HEADS, ATT_HEAD_DIM)
    new_state = jnp.stack([h_f, h_b], axis=1).reshape(BATCH, 1, 2, SSM_HEADS, SSM_HEAD_DIM, D_STATE)
    return (y_prompt, y_sample, new_k, new_v, new_state)
```

```python
import functools

import numpy as np
import jax
import jax.numpy as jnp
from jax import lax
from jax.experimental import pallas as pl
from jax.experimental.pallas import tpu as pltpu

D_MODEL = 1024
BATCH = 32
SEQ = 256
DEPTH = 2
DEC_BATCH = 4
DEC_SEQ = 1024
PAST_LEN = 512
GRID_W = 64
SSM_HEAD_DIM = 64
SSM_INNER = D_MODEL
SSM_HEADS = SSM_INNER // SSM_HEAD_DIM
SSM_GROUPS = 2
SSM_HEADS_PER_GROUP = SSM_HEADS // SSM_GROUPS
D_STATE = 128
SSM_CONV = 5
SSM_CHUNK = 128
SSM_BC = SSM_GROUPS * D_STATE
SSM_CONV_DIM = SSM_INNER + 2 * SSM_BC
ATT_HEAD_DIM = 64
ATT_HEADS = D_MODEL // ATT_HEAD_DIM
ATT_KV_HEADS = 4
ATT_GROUP = ATT_HEADS // ATT_KV_HEADS
WINDOW = 128
ATT_BLOCK = 128
ATT_SCALE = ATT_HEAD_DIM ** -0.5
ROPE_BASE = 10000.0
ROPE_HALF = ATT_HEAD_DIM // 2
ROPE_QUARTER = ATT_HEAD_DIM // 4
SHORT_CONV = 3
N_EXPERTS = 32
TOP_K = 4
D_EXPERT = D_MODEL
SWIGLU_ALPHA = 1.702
SWIGLU_LIMIT = 7.0
N_EVEN = (DEPTH + 1) // 2
DEEPNORM_ALPHA = (2 * DEPTH) ** 0.25
LN_EPS = 1e-5
RMS_EPS = 1e-5
KV_DIM = ATT_KV_HEADS * ATT_HEAD_DIM

N_PROMPT = BATCH * SEQ
N_SAMPLE = DEC_BATCH * DEC_SEQ
N_TOK = N_PROMPT + N_SAMPLE
N_COND = 1 + DEC_BATCH
SUBLANES = 8
LANES = 128
COND_PAD = SUBLANES
MOD_ROWS = SUBLANES
DT_PAD = LANES

TOK_TILE = 512
CONV_TILE = 256
MOE_TILE = 256
MOE_ROWS = N_TOK * TOP_K + N_EXPERTS * MOE_TILE
MOE_TILES = MOE_ROWS // MOE_TILE
VMEM_LIMIT = 56 * 1024 * 1024
NEG_BIG = -1e30
WEIGHT_DMA_PRIORITY = 1
ROW_TILES = D_MODEL // LANES

assert N_PROMPT % TOK_TILE == 0 and DEC_SEQ % TOK_TILE == 0
PROMPT_TILES = N_PROMPT // TOK_TILE
assert SEQ % CONV_TILE == 0 and DEC_SEQ % CONV_TILE == 0

_NT = (((1,), (1,)), ((), ()))
_TN = (((0,), (0,)), ((), ()))


def _cparams(sem):
    return pltpu.CompilerParams(dimension_semantics=sem, vmem_limit_bytes=VMEM_LIMIT)


def _cond_row(i):
    first = N_PROMPT // TOK_TILE
    per = DEC_SEQ // TOK_TILE
    return jnp.where(i < first, 0, 1 + (i - first) // per)


def _bf16_dot(a, b):
    return jnp.dot(a.astype(jnp.bfloat16), b.astype(jnp.bfloat16), preferred_element_type=jnp.float32)


def _mod_kernel(c_ref, w_ref, b_ref, o_ref):
    c = c_ref[...]
    s = c * jax.nn.sigmoid(c)
    o_ref[...] = _bf16_dot(s, w_ref[...]) + b_ref[...]


def _modulation(cond, w_mod, b_mod):
    tn = 1536
    return pl.pallas_call(
        _mod_kernel,
        grid=(DEPTH, 6 * D_MODEL // tn),
        in_specs=[
            pl.BlockSpec((COND_PAD, D_MODEL), lambda l, n: (0, 0)),
            pl.BlockSpec((None, D_MODEL, tn), lambda l, n: (l, 0, n)),
            pl.BlockSpec((None, 1, tn), lambda l, n: (l, 0, n)),
        ],
        out_specs=pl.BlockSpec((None, COND_PAD, tn), lambda l, n: (l, 0, n)),
        out_shape=jax.ShapeDtypeStruct((DEPTH, COND_PAD, 6 * D_MODEL), jnp.float32),
        compiler_params=_cparams(("arbitrary", "arbitrary")),
        name="modulation",
    )(cond, w_mod, b_mod.reshape(DEPTH, 1, 6 * D_MODEL))


def _stream_specs(x):
    if not isinstance(x, tuple):
        return [pl.BlockSpec((TOK_TILE, D_MODEL), lambda i: (i, 0))]
    return [pl.BlockSpec((TOK_TILE, D_MODEL), lambda i: (jnp.minimum(i, PROMPT_TILES - 1), 0)),
            pl.BlockSpec((TOK_TILE, D_MODEL), lambda i: (jnp.maximum(i - PROMPT_TILES, 0), 0))]


def _stream_tile(x_refs):
    if len(x_refs) == 1:
        return x_refs[0][...]
    return jnp.where(pl.program_id(0) < PROMPT_TILES, x_refs[0][...], x_refs[1][...])


def _inproj_kernel(*refs, n_x, gate_product):
    x_refs, mod_ref, refs = refs[:n_x], refs[n_x], refs[n_x + 1:]
    n = len(refs) // 2 + (1 if gate_product else 0)
    w_refs, o_refs = refs[:n], refs[n:]
    u = _stream_tile(x_refs) * (1.0 + mod_ref[1:2, :]) + mod_ref[0:1, :]
    ub = u.astype(jnp.bfloat16)
    outs = [jnp.dot(ub, w_ref[...], preferred_element_type=jnp.float32) for w_ref in w_refs]
    if gate_product:
        outs = outs[:-2] + [outs[-2] * outs[-1]]
    for o_ref, o in zip(o_refs, outs):
        o_ref[...] = o


def _inproj(x, mod, weights, gate_product=False):
    out_w = [w.shape[1] for w in weights]
    if gate_product:
        out_w = out_w[:-1]
    xs = x if isinstance(x, tuple) else (x,)
    return pl.pallas_call(
        functools.partial(_inproj_kernel, n_x=len(xs), gate_product=gate_product),
        grid=(N_TOK // TOK_TILE,),
        in_specs=_stream_specs(x)
        + [pl.BlockSpec((None, MOD_ROWS, D_MODEL), lambda i: (_cond_row(i), 0, 0))]
        + [pl.BlockSpec(w.shape, lambda i: (0, 0)) for w in weights],
        out_specs=[pl.BlockSpec((TOK_TILE, n), lambda i: (i, 0)) for n in out_w],
        out_shape=[jax.ShapeDtypeStruct((N_TOK, n), jnp.float32) for n in out_w],
        compiler_params=_cparams(("arbitrary",)),
        name="inproj",
    )(*xs, mod, *weights)


def _layer_norm(v, w, b):
    mu = jnp.mean(v, axis=-1, keepdims=True)
    d = v - mu
    var = jnp.mean(d * d, axis=-1, keepdims=True)
    return d * lax.rsqrt(var + LN_EPS) * w + b


def _outproj_kernel(*refs, n_in, n_x):
    y_refs = refs[:n_in]
    w_refs = refs[n_in:2 * n_in]
    x_refs = refs[2 * n_in:2 * n_in + n_x]
    (mod_ref, ln_ref, wr_ref, br_ref,
     x1_ref, u2_ref, idx_ref, gate_ref, rank_ref, cnt_ref, run_ref) = refs[2 * n_in + n_x:]
    f32, bf16 = jnp.float32, jnp.bfloat16

    @pl.when(pl.program_id(0) == 0)
    def _():
        run_ref[...] = jnp.zeros_like(run_ref)

    mix = _bf16_dot(y_refs[0][...], w_refs[0][...])
    for y_ref, w_ref in zip(y_refs[1:], w_refs[1:]):
        mix = mix + _bf16_dot(y_ref[...], w_ref[...])
    x1 = _layer_norm(DEEPNORM_ALPHA * _stream_tile(x_refs) + mod_ref[2:3, :] * mix, ln_ref[0:1, :], ln_ref[1:2, :])
    x1_ref[...] = x1
    u2 = x1 * (1.0 + mod_ref[4:5, :]) + mod_ref[3:4, :]
    u2_ref[...] = u2
    wr = wr_ref[...]
    u_hi, w_hi = u2.astype(bf16), wr.astype(bf16)
    u_lo, w_lo = (u2 - u_hi.astype(f32)).astype(bf16), (wr - w_hi.astype(f32)).astype(bf16)
    logits = (jnp.dot(u_hi, w_hi, preferred_element_type=f32)
              + (jnp.dot(u_lo, w_hi, preferred_element_type=f32) + jnp.dot(u_hi, w_lo, preferred_element_type=f32))
              + br_ref[...])

    lane = lax.broadcasted_iota(jnp.int32, (TOK_TILE, N_EXPERTS), 1)
    work = logits
    vals, sels, idxs = [], [], []
    for _ in range(TOP_K):
        m = jnp.max(work, axis=-1, keepdims=True)
        idx = jnp.min(jnp.where(work == m, lane, N_EXPERTS), axis=-1, keepdims=True)
        sel = lane == idx
        vals.append(m)
        idxs.append(idx)
        sels.append(sel)
        work = jnp.where(sel, -jnp.inf, work)
    exps = [jnp.exp(v - vals[0]) for v in vals]
    den = exps[0]
    for e in exps[1:]:
        den = den + e
    onehot = sels[0].astype(f32)
    for sel in sels[1:]:
        onehot = onehot + sel.astype(f32)
    r = lax.broadcasted_iota(jnp.int32, (TOK_TILE, TOK_TILE), 0)
    c = lax.broadcasted_iota(jnp.int32, (TOK_TILE, TOK_TILE), 1)
    before = jnp.dot((r > c).astype(jnp.bfloat16), onehot.astype(jnp.bfloat16), preferred_element_type=f32)
    rank_all = before + run_ref[...]
    run = run_ref[...] + jnp.sum(onehot, axis=0, keepdims=True)
    run_ref[...] = run
    cnt_ref[...] = run.astype(jnp.int32)
    col = lax.broadcasted_iota(jnp.int32, (TOK_TILE, TOP_K), 1)
    idx_o = jnp.zeros((TOK_TILE, TOP_K), jnp.int32)
    gate_o = jnp.zeros((TOK_TILE, TOP_K), f32)
    rank_o = jnp.zeros((TOK_TILE, TOP_K), f32)
    for k in range(TOP_K):
        rank_k = jnp.sum(jnp.where(sels[k], rank_all, 0.0), axis=-1, keepdims=True)
        idx_o = jnp.where(col == k, idxs[k], idx_o)
        gate_o = jnp.where(col == k, exps[k] / den, gate_o)
        rank_o = jnp.where(col == k, rank_k, rank_o)
    idx_ref[...] = idx_o
    gate_ref[...] = gate_o
    rank_ref[...] = rank_o.astype(jnp.int32)


def _outproj(ys, ws, x, mod, ln, w_router, b_router):
    n_in = len(ys)
    xs = x if isinstance(x, tuple) else (x,)
    return pl.pallas_call(
        functools.partial(_outproj_kernel, n_in=n_in, n_x=len(xs)),
        grid=(N_TOK // TOK_TILE,),
        in_specs=[pl.BlockSpec((TOK_TILE, y.shape[1]), lambda i: (i, 0)) for y in ys]
        + [pl.BlockSpec(w.shape, lambda i: (0, 0)) for w in ws]
        + _stream_specs(x)
        + [
            pl.BlockSpec((None, MOD_ROWS, D_MODEL), lambda i: (_cond_row(i), 0, 0)),
            pl.BlockSpec((2, D_MODEL), lambda i: (0, 0)),
            pl.BlockSpec((D_MODEL, N_EXPERTS), lambda i: (0, 0)),
            pl.BlockSpec((1, N_EXPERTS), lambda i: (0, 0)),
        ],
        out_specs=[
            pl.BlockSpec((TOK_TILE, D_MODEL), lambda i: (i, 0)),
            pl.BlockSpec((TOK_TILE, D_MODEL), lambda i: (i, 0)),
            pl.BlockSpec((TOK_TILE, TOP_K), lambda i: (i, 0)),
            pl.BlockSpec((TOK_TILE, TOP_K), lambda i: (i, 0)),
            pl.BlockSpec((TOK_TILE, TOP_K), lambda i: (i, 0)),
            pl.BlockSpec((1, N_EXPERTS), lambda i: (0, 0)),
        ],
        out_shape=[
            jax.ShapeDtypeStruct((N_TOK, D_MODEL), jnp.float32),
            jax.ShapeDtypeStruct((N_TOK, D_MODEL), jnp.float32),
            jax.ShapeDtypeStruct((N_TOK, TOP_K), jnp.int32),
            jax.ShapeDtypeStruct((N_TOK, TOP_K), jnp.float32),
            jax.ShapeDtypeStruct((N_TOK, TOP_K), jnp.int32),
            jax.ShapeDtypeStruct((1, N_EXPERTS), jnp.int32),
        ],
        scratch_shapes=[pltpu.VMEM((1, N_EXPERTS), jnp.float32)],
        compiler_params=_cparams(("arbitrary",)),
        name="outproj_ln_router",
    )(*ys, *ws, *xs, mod, ln, w_router, b_router.reshape(1, N_EXPERTS))


def _moe_kernel(te_ref, nv_ref, src0_ref, last_ref, wslot_ref, nexte_ref, stok_ref, x_hbm,
                wg_hbm, bg_ref, wl_hbm, bl_ref, wd_hbm, bd_ref, o_ref,
                w_bf, xbuf, sem, wst, wsem, *, layer):
    i = pl.program_id(0)
    nv = nv_ref[0]
    valid = i < nv
    prev = te_ref[jnp.maximum(i - 1, 0)]
    new_expert = jnp.logical_or(i == 0, te_ref[i] != prev)

    def start_gather(tile, buf):
        base, last = src0_ref[tile], last_ref[tile]
        for r in range(MOE_TILE):
            tok = stok_ref[jnp.minimum(base + r, last)]
            pltpu.make_async_copy(x_hbm.at[tok], xbuf.at[buf, r], sem.at[buf]).start()

    def wait_gather(buf):
        pltpu.make_async_copy(x_hbm.at[pl.ds(0, MOE_TILE)], xbuf.at[buf], sem.at[buf]).wait()

    def weight_copies(expert, slot):
        return [pltpu.make_async_copy(w.at[layer, expert], wst.at[slot, j], wsem.at[slot])
                for j, w in enumerate((wg_hbm, wl_hbm, wd_hbm))]

    @pl.when(i == 0)
    def _():
        start_gather(0, 0)
        for cp in weight_copies(te_ref[0], 0):
            cp.start(priority=WEIGHT_DMA_PRIORITY)

    @pl.when(jnp.logical_and(valid, new_expert))
    def _():
        slot = wslot_ref[i]
        nxt = nexte_ref[i]

        @pl.when(nxt >= 0)
        def _():
            for cp in weight_copies(nxt, 1 - slot):
                cp.start(priority=WEIGHT_DMA_PRIORITY)

        for cp in weight_copies(te_ref[i], slot):
            cp.wait()
        for j in range(3):
            w_bf[j] = wst[slot, j].astype(jnp.bfloat16)

    for buf in range(2):
        @pl.when(jnp.logical_and(valid, i % 2 == buf))
        def _():
            wait_gather(buf)
            start_gather(jnp.minimum(i + 1, nv - 1), 1 - buf)
            x = jnp.concatenate([xbuf[buf, :, s, :] for s in range(ROW_TILES)], axis=1).astype(jnp.bfloat16)
            hg = jnp.dot(x, w_bf[0], preferred_element_type=jnp.float32) + bg_ref[...]
            hl = jnp.dot(x, w_bf[1], preferred_element_type=jnp.float32) + bl_ref[...]
            hg = jnp.minimum(hg, SWIGLU_LIMIT)
            hl = jnp.clip(hl, -SWIGLU_LIMIT, SWIGLU_LIMIT)
            h = hg * jax.nn.sigmoid(SWIGLU_ALPHA * hg) * (hl + 1.0)
            o_ref[...] = (jnp.dot(h.astype(jnp.bfloat16), w_bf[2], preferred_element_type=jnp.float32)
                          + bd_ref[...])

        @pl.when(jnp.logical_and(i == nv - 1, i % 2 == buf))
        def _():
            wait_gather(1 - buf)

    @pl.when(jnp.logical_not(valid))
    def _():
        o_ref[...] = jnp.zeros_like(o_ref)


def _moe_grouped(layer, tables, sorted_tok, x, w_gate, b_gate, w_lin, b_lin, w_down, b_down):
    hbm = pl.BlockSpec(memory_space=pl.ANY)
    bspec = pl.BlockSpec((None, None, 1, D_EXPERT), lambda i, te, *_: (layer, te[i], 0, 0))
    grid_spec = pltpu.PrefetchScalarGridSpec(
        num_scalar_prefetch=len(tables) + 1,
        grid=(MOE_TILES,),
        in_specs=[hbm, hbm, bspec, hbm, bspec, hbm, bspec],
        out_specs=pl.BlockSpec((MOE_TILE, D_MODEL), lambda i, *_: (i, 0)),
        scratch_shapes=[
            pltpu.VMEM((3, D_MODEL, D_EXPERT), jnp.bfloat16),
            pltpu.VMEM((2, MOE_TILE, ROW_TILES, LANES), jnp.float32), pltpu.SemaphoreType.DMA((2,)),
            pltpu.VMEM((2, 3, D_MODEL, D_EXPERT), jnp.float32), pltpu.SemaphoreType.DMA((2,)),
        ],
    )
    bshape = (DEPTH, N_EXPERTS, 1, D_EXPERT)
    return pl.pallas_call(
        functools.partial(_moe_kernel, layer=layer),
        grid_spec=grid_spec,
        out_shape=jax.ShapeDtypeStruct((MOE_ROWS, D_MODEL), jnp.float32),
        compiler_params=_cparams(("arbitrary",)),
        name="moe_grouped",
    )(*tables, sorted_tok, x.reshape(N_TOK, ROW_TILES, LANES), w_gate, b_gate.reshape(bshape),
      w_lin, b_lin.reshape(bshape), w_down, b_down.reshape(bshape))


def _combine_kernel(y_ref, g_ref, x_ref, mod_ref, ln_ref, *o_refs):
    g = g_ref[...]
    ff = g[:, 0:1] * y_ref[0]
    for k in range(1, TOP_K):
        ff = ff + g[:, k:k + 1] * y_ref[k]
    out = _layer_norm(DEEPNORM_ALPHA * x_ref[...] + mod_ref[5:6, :] * ff, ln_ref[0:1, :], ln_ref[1:2, :])
    if len(o_refs) == 1:
        o_refs[0][...] = out
    else:
        is_prompt = pl.program_id(0) < PROMPT_TILES

        @pl.when(is_prompt)
        def _():
            o_refs[0][...] = out

        @pl.when(jnp.logical_not(is_prompt))
        def _():
            o_refs[1][...] = out


def _combine(yk, gates, x, mod, ln, split=False):
    if split:
        out_specs = [pl.BlockSpec((TOK_TILE, D_MODEL), lambda i: (jnp.minimum(i, PROMPT_TILES - 1), 0)),
                     pl.BlockSpec((TOK_TILE, D_MODEL), lambda i: (jnp.maximum(i - PROMPT_TILES, 0), 0))]
        out_shape = [jax.ShapeDtypeStruct((N_PROMPT, D_MODEL), jnp.float32),
                     jax.ShapeDtypeStruct((N_SAMPLE, D_MODEL), jnp.float32)]
    else:
        out_specs = pl.BlockSpec((TOK_TILE, D_MODEL), lambda i: (i, 0))
        out_shape = jax.ShapeDtypeStruct((N_TOK, D_MODEL), jnp.float32)
    return pl.pallas_call(
        _combine_kernel,
        grid=(N_TOK // TOK_TILE,),
        in_specs=[
            pl.BlockSpec((TOP_K, TOK_TILE, D_MODEL), lambda i: (0, i, 0)),
            pl.BlockSpec((TOK_TILE, TOP_K), lambda i: (i, 0)),
            pl.BlockSpec((TOK_TILE, D_MODEL), lambda i: (i, 0)),
            pl.BlockSpec((None, MOD_ROWS, D_MODEL), lambda i: (_cond_row(i), 0, 0)),
            pl.BlockSpec((2, D_MODEL), lambda i: (0, 0)),
        ],
        out_specs=out_specs,
        out_shape=out_shape,
        compiler_params=_cparams(("arbitrary",)),
        name="combine_ln",
    )(yk, gates, x, mod, ln)


def _route(top_idx, rank, counts):
    counts = counts.reshape(N_EXPERTS)
    padded = (counts + MOE_TILE - 1) // MOE_TILE * MOE_TILE
    pend = jnp.cumsum(padded)
    pstart = pend - padded
    start = jnp.cumsum(counts) - counts
    experts = jnp.arange(N_EXPERTS, dtype=jnp.int32)
    lookup = lambda table, idx: jnp.sum(jnp.where(idx[..., None] == experts, table, 0), axis=-1)
    slot = lookup(pstart, top_idx) + rank
    key = lookup(start, top_idx) + rank
    tok = jnp.broadcast_to(jnp.arange(N_TOK, dtype=jnp.int32)[:, None], (N_TOK, TOP_K))
    _, sorted_tok = lax.sort((key.reshape(-1), tok.reshape(-1)), num_keys=1)
    n_valid = (pend[-1] // MOE_TILE).astype(jnp.int32)
    tile_start = jnp.arange(MOE_TILES, dtype=jnp.int32) * MOE_TILE
    tile_expert = jnp.sum((tile_start[:, None] >= pend[None, :]).astype(jnp.int32), axis=1)
    tile_expert = jnp.where(tile_start < pend[-1], tile_expert, tile_expert[n_valid - 1])
    src0 = tile_start - lookup(pstart, tile_expert) + lookup(start, tile_expert)
    src_last = lookup(start, tile_expert) + lookup(counts, tile_expert) - 1
    used = (counts > 0).astype(jnp.int32)
    order = jnp.cumsum(used) - used
    later = jnp.where(jnp.logical_and(experts[None, :] > experts[:, None], used[None, :] > 0),
                      experts[None, :], N_EXPERTS)
    successor = jnp.min(later, axis=1)
    successor = jnp.where(successor < N_EXPERTS, successor, -1)
    wslot = lookup(order, tile_expert) % 2
    next_expert = lookup(successor, tile_expert)
    tables = (tile_expert, n_valid.reshape(1), src0, src_last, wslot, next_expert)
    return slot, sorted_tok, tables


def _moe_layer(i, x1, u2, routing, mod, ln, w_gate, b_gate, w_lin, b_lin, w_down, b_down, split=False):
    top_idx, gates, rank, counts = routing
    slot, sorted_tok, tables = _route(top_idx, rank, counts)
    y = _moe_grouped(i, tables, sorted_tok, u2, w_gate, b_gate, w_lin, b_lin, w_down, b_down)
    yk = jnp.take(y, slot.T, axis=0, mode="clip")
    return _combine(yk, gates, x1, mod, ln, split)


_F_ROW, _F_FIRST, _F_LAST, _F_PREV8, _F_HASPREV, _F_NEXT8, _F_HASNEXT, _F_H0, _F_USEH0, _F_HOUT, _F_WRITEH = range(11)
_N_FIELDS = 11
_SSD_STEPS = N_TOK // SSM_CHUNK


def _ssd_table(reverse):
    rows = []
    for prompt, nseq, slen, base in ((True, BATCH, SEQ, 0), (False, DEC_BATCH, DEC_SEQ, N_PROMPT)):
        nc = slen // SSM_CHUNK
        for s in range(nseq):
            for c in range(nc):
                row = (base + s * slen) // SSM_CHUNK + c
                first, last = (c == nc - 1, c == 0) if reverse else (c == 0, c == nc - 1)
                per = SSM_CHUNK // SUBLANES
                rows.append([
                    row, int(first), int(last),
                    max(row * per - 1, 0), int(c > 0),
                    min(row * per + per, N_TOK // SUBLANES - 1), int(c < nc - 1),
                    0 if prompt else s, int(not prompt),
                    s if prompt else BATCH - 1, int(prompt and last),
                ])
    if reverse:
        rows = rows[::-1]
    return np.asarray(rows, np.int32).T.reshape(-1)


def _fld(tbl, f, s):
    return tbl[f * _SSD_STEPS + s]


def _exact_expand(a, e_ref, passes):
    k = a.shape[1]
    pieces, rem = [], a
    for _ in range(passes):
        piece = rem.astype(jnp.bfloat16)
        rem = rem - piece.astype(jnp.float32)
        pieces.append(piece)
    out = None
    for i in range(0, passes, 2):
        pair = pieces[i:i + 2]
        lhs = pair[0] if len(pair) == 1 else jnp.concatenate(pair, axis=1)
        t = jnp.dot(lhs, e_ref[0:k * len(pair), :], preferred_element_type=jnp.float32)
        out = t if out is None else out + t
    return out


def _ssd_chunk(xs, bmat, cmat, dt_raw, dtb_row, alog_row, ex_ref, eb_ref, s_ref, reverse):
    f32, bf16 = jnp.float32, jnp.bfloat16
    base = SSM_HEADS if reverse else 0
    lane = lax.broadcasted_iota(jnp.int32, (1, DT_PAD), 1)
    head_cols = jnp.logical_and(lane >= base, lane < base + SSM_HEADS)
    xr = dt_raw + dtb_row
    dt = jnp.maximum(xr, 0.0) + jnp.log1p(jnp.exp(-jnp.abs(xr)))
    a_row = jnp.where(head_cols, -jnp.exp(alog_row), 0.0)
    d_a = dt * a_row
    r = lax.broadcasted_iota(jnp.int32, (SSM_CHUNK, SSM_CHUNK), 0)
    c = lax.broadcasted_iota(jnp.int32, (SSM_CHUNK, SSM_CHUNK), 1)
    tri = (r <= c) if reverse else (r >= c)
    cum = jnp.dot(tri.astype(f32), d_a, preferred_element_type=f32, precision=lax.Precision.HIGHEST)
    cum_t = cum.T
    total = cum[0:1, :] if reverse else cum[SSM_CHUNK - 1:SSM_CHUNK, :]
    stacked = jnp.concatenate([dt, jnp.exp(cum), dt * jnp.exp(total - cum),
                               jnp.broadcast_to(jnp.exp(total), (SUBLANES, DT_PAD))], axis=0)
    stacked_x = _exact_expand(stacked, ex_ref, 2)
    dt_x = stacked_x[0:SSM_CHUNK]
    e_in_x = stacked_x[SSM_CHUNK:2 * SSM_CHUNK]
    w_end_x = stacked_x[2 * SSM_CHUNK:3 * SSM_CHUNK]
    dec_x = stacked_x[3 * SSM_CHUNK:3 * SSM_CHUNK + 1]
    cum_b = _exact_expand(cum, eb_ref, 3)
    xdt = xs * dt_x
    xw = (xs * w_end_x).astype(bf16)
    s_in = s_ref[...]
    s_bf = s_in.astype(bf16)
    lo_half = lax.broadcasted_iota(jnp.int32, (1, 2 * SSM_HEAD_DIM), 1) < SSM_HEAD_DIM
    gw = SSM_HEADS_PER_GROUP * SSM_HEAD_DIM
    pieces = []
    for g in range(SSM_GROUPS):
        bg = bmat[:, g * D_STATE:(g + 1) * D_STATE]
        cg = cmat[:, g * D_STATE:(g + 1) * D_STATE]
        gcols = slice(g * gw, (g + 1) * gw)
        cb = lax.dot_general(cg, bg, _NT, preferred_element_type=f32)
        y_off = jnp.dot(cg, s_bf[:, gcols], preferred_element_type=f32) * e_in_x[:, gcols]
        upd = lax.dot_general(bg, xw[:, gcols], _TN, preferred_element_type=f32)
        s_ref[:, gcols] = s_in[:, gcols] * dec_x[:, gcols] + upd
        for q in range(SSM_HEADS_PER_GROUP // 2):
            h0 = g * SSM_HEADS_PER_GROUP + 2 * q
            pcols = slice(h0 * SSM_HEAD_DIM, (h0 + 2) * SSM_HEAD_DIM)
            xp = xdt[:, pcols]
            bd = jnp.concatenate([jnp.where(lo_half, xp, 0.0), jnp.where(lo_half, 0.0, xp)], axis=0).astype(bf16)
            ms = []
            for h in (h0, h0 + 1):
                ch = base + h
                seg = cum_b[:, h * SSM_CHUNK:(h + 1) * SSM_CHUNK] - cum_t[ch:ch + 1, :]
                decay = jnp.exp(jnp.where(tri, seg, NEG_BIG))
                ms.append((cb * decay).astype(bf16))
            y_diag = jnp.dot(jnp.concatenate(ms, axis=1), bd, preferred_element_type=f32)
            pieces.append(y_diag + y_off[:, 2 * q * SSM_HEAD_DIM:(2 * q + 2) * SSM_HEAD_DIM])
    return jnp.concatenate(pieces, axis=1)


def _ssd_init_state(tbl, step, h0_ref, s_ref):
    @pl.when(_fld(tbl, _F_FIRST, step) == 1)
    def _():
        use = _fld(tbl, _F_USEH0, step) == 1
        s_ref[...] = jnp.where(use, h0_ref[...], 0.0).T


def _ssd_write_state(tbl, step, hout_ref, s_ref):
    @pl.when(_fld(tbl, _F_WRITEH, step) == 1)
    def _():
        hout_ref[...] = s_ref[...].T


def _ssd_fwd_kernel(tbl, xc_ref, xp_ref, xn_ref, dt_ref, cw_ref, cbias_ref, dtb_ref, alog_ref, ex_ref, eb_ref,
                    h0_ref, xs_ref, bc_ref, yf_ref, hout_ref, s_ref):
    step = pl.program_id(0)
    _ssd_init_state(tbl, step, h0_ref, s_ref)
    hp = (_fld(tbl, _F_HASPREV, step) == 1).astype(jnp.float32)
    hn = (_fld(tbl, _F_HASNEXT, step) == 1).astype(jnp.float32)
    window = jnp.concatenate([xp_ref[...] * hp, xc_ref[...], xn_ref[...] * hn], axis=0)
    acc = cbias_ref[...] + cw_ref[0:1, :] * window[SUBLANES - 2:SUBLANES - 2 + SSM_CHUNK, :]
    for k in range(1, SSM_CONV):
        off = SUBLANES - SSM_CONV // 2 + k
        acc = acc + cw_ref[k:k + 1, :] * window[off:off + SSM_CHUNK, :]
    xbc = acc * jax.nn.sigmoid(acc)
    xs = xbc[:, :SSM_INNER]
    bc = xbc[:, SSM_INNER:].astype(jnp.bfloat16)
    xs_ref[...] = xs
    bc_ref[...] = bc
    yf_ref[...] = _ssd_chunk(xs, bc[:, :SSM_BC], bc[:, SSM_BC:], dt_ref[...], dtb_ref[...], alog_ref[...],
                             ex_ref, eb_ref, s_ref, reverse=False)
    _ssd_write_state(tbl, step, hout_ref, s_ref)


def _ssd_bwd_kernel(tbl, xs_ref, bc_ref, dt_ref, yf_ref, z_ref, dtb_ref, alog_ref, dskip_ref, nw_ref,
                    ex_ref, eb_ref, h0_ref, y_ref, hout_ref, s_ref):
    step = pl.program_id(0)
    _ssd_init_state(tbl, step, h0_ref, s_ref)
    xs = xs_ref[...]
    bc = bc_ref[...]
    yb = _ssd_chunk(xs, bc[:, :SSM_BC], bc[:, SSM_BC:], dt_ref[...], dtb_ref[...], alog_ref[...],
                    ex_ref, eb_ref, s_ref, reverse=True)
    z = z_ref[...]
    hg = (yf_ref[...] + yb + dskip_ref[...] * xs) * (z * jax.nn.sigmoid(z))
    gw = SSM_INNER // SSM_GROUPS
    outs = []
    for g in range(SSM_GROUPS):
        hgg = hg[:, g * gw:(g + 1) * gw]
        outs.append(hgg * lax.rsqrt(jnp.mean(hgg * hgg, axis=-1, keepdims=True) + RMS_EPS))
    y_ref[...] = (jnp.concatenate(outs, axis=1) * nw_ref[...]).astype(jnp.bfloat16)
    _ssd_write_state(tbl, step, hout_ref, s_ref)


def _ssd_mixer(z, xbc_raw, dt_raw, conv_w, conv_b, a_log, dt_bias, d_skip, norm_w, state_in):
    f32 = jnp.float32
    row = lambda f: (lambda s, tbl: (_fld(tbl, f, s), 0))
    chunk_spec = lambda w: pl.BlockSpec((SSM_CHUNK, w), row(_F_ROW))
    const_spec = lambda shape: pl.BlockSpec(shape, lambda s, tbl: (0,) * len(shape))
    h0_spec = lambda d: pl.BlockSpec((None, None, SSM_INNER, D_STATE), lambda s, tbl: (_fld(tbl, _F_H0, s), d, 0, 0))
    hout_spec = pl.BlockSpec((None, SSM_INNER, D_STATE), lambda s, tbl: (_fld(tbl, _F_HOUT, s), 0, 0))
    hout_shape = jax.ShapeDtypeStruct((BATCH, SSM_INNER, D_STATE), f32)
    state_scratch = [pltpu.VMEM((D_STATE, SSM_INNER), f32)]

    def expanders(reverse):
        base = SSM_HEADS if reverse else 0
        ex = np.zeros((DT_PAD, SSM_INNER), np.float32)
        eb = np.zeros((DT_PAD, SSM_HEADS * SSM_CHUNK), np.float32)
        for h in range(SSM_HEADS):
            ex[base + h, h * SSM_HEAD_DIM:(h + 1) * SSM_HEAD_DIM] = 1.0
            eb[base + h, h * SSM_CHUNK:(h + 1) * SSM_CHUNK] = 1.0
        twice = lambda e: jnp.asarray(np.concatenate([e, e], axis=0), jnp.bfloat16)
        return twice(ex), twice(eb)

    expander_specs = [const_spec((2 * DT_PAD, SSM_INNER)), const_spec((2 * DT_PAD, SSM_HEADS * SSM_CHUNK))]

    cw = jnp.pad(conv_w, ((0, SUBLANES - SSM_CONV), (0, 0)))
    cbias = conv_b.reshape(1, SSM_CONV_DIM)
    dtb = jnp.pad(dt_bias.reshape(1, 2 * SSM_HEADS), ((0, 0), (0, DT_PAD - 2 * SSM_HEADS)))
    alog = jnp.pad(a_log.reshape(1, 2 * SSM_HEADS), ((0, 0), (0, DT_PAD - 2 * SSM_HEADS)))
    dskip = jnp.repeat(d_skip, SSM_HEAD_DIM).reshape(1, SSM_INNER)
    nw = norm_w.reshape(1, SSM_INNER)

    xs, bc, yf, h_f = pl.pallas_call(
        _ssd_fwd_kernel,
        grid_spec=pltpu.PrefetchScalarGridSpec(
            num_scalar_prefetch=1,
            grid=(_SSD_STEPS,),
            in_specs=[
                chunk_spec(SSM_CONV_DIM),
                pl.BlockSpec((SUBLANES, SSM_CONV_DIM), row(_F_PREV8)),
                pl.BlockSpec((SUBLANES, SSM_CONV_DIM), row(_F_NEXT8)),
                chunk_spec(DT_PAD),
                const_spec((SUBLANES, SSM_CONV_DIM)), const_spec((1, SSM_CONV_DIM)),
                const_spec((1, DT_PAD)), const_spec((1, DT_PAD)), *expander_specs,
                h0_spec(0),
            ],
            out_specs=[chunk_spec(SSM_INNER), chunk_spec(2 * SSM_BC), chunk_spec(SSM_INNER), hout_spec],
            scratch_shapes=state_scratch,
        ),
        out_shape=[
            jax.ShapeDtypeStruct((N_TOK, SSM_INNER), f32),
            jax.ShapeDtypeStruct((N_TOK, 2 * SSM_BC), jnp.bfloat16),
            jax.ShapeDtypeStruct((N_TOK, SSM_INNER), f32),
            hout_shape,
        ],
        compiler_params=_cparams(("arbitrary",)),
        name="ssd_forward",
    )(jnp.asarray(_ssd_table(False)), xbc_raw, xbc_raw, xbc_raw, dt_raw, cw, cbias, dtb, alog,
      *expanders(False), state_in)

    y, h_b = pl.pallas_call(
        _ssd_bwd_kernel,
        grid_spec=pltpu.PrefetchScalarGridSpec(
            num_scalar_prefetch=1,
            grid=(_SSD_STEPS,),
            in_specs=[
                chunk_spec(SSM_INNER), chunk_spec(2 * SSM_BC), chunk_spec(DT_PAD), chunk_spec(SSM_INNER),
                chunk_spec(SSM_INNER),
                const_spec((1, DT_PAD)), const_spec((1, DT_PAD)),
                const_spec((1, SSM_INNER)), const_spec((1, SSM_INNER)), *expander_specs,
                h0_spec(1),
            ],
            out_specs=[chunk_spec(SSM_INNER), hout_spec],
            scratch_shapes=state_scratch,
        ),
        out_shape=[jax.ShapeDtypeStruct((N_TOK, SSM_INNER), jnp.bfloat16), hout_shape],
        compiler_params=_cparams(("arbitrary",)),
        name="ssd_backward",
    )(jnp.asarray(_ssd_table(True)), xs, bc, dt_raw, yf, z, dtb, alog, dskip, nw, *expanders(True), state_in)
    return y, h_f, h_b


def _sink_attention(q_heads, keys, vals, sink_ref, kv, masks):
    outs = []
    for g, qh in enumerate(q_heads):
        sk = sink_ref[kv * ATT_GROUP + g]
        scores = []
        for kk, mask in zip(keys, masks):
            s = lax.dot_general(qh, kk, _NT, preferred_element_type=jnp.float32) * ATT_SCALE
            scores.append(s if mask is None else jnp.where(mask, s, NEG_BIG))
        m = sk
        for s in scores:
            m = jnp.maximum(m, jnp.max(s, axis=-1, keepdims=True))
        den = jnp.exp(sk - m)
        acc = None
        for s, vv in zip(scores, vals):
            p = jnp.exp(s - m)
            den = den + jnp.sum(p, axis=-1, keepdims=True)
            pv = jnp.dot(p.astype(jnp.bfloat16), vv, preferred_element_type=jnp.float32)
            acc = pv if acc is None else acc + pv
        outs.append(acc / den)
    return outs


def _head(x, h):
    return x[:, h * ATT_HEAD_DIM:(h + 1) * ATT_HEAD_DIM]


def _attn_ctx_kernel(sink_ref, q_ref, k_ref, v_ref, o_ref):
    bf16 = jnp.bfloat16
    q = q_ref[...].astype(bf16)
    k = k_ref[...].astype(bf16)
    v = v_ref[...].astype(bf16)
    outs = []
    for kv in range(ATT_KV_HEADS):
        qs = [_head(q, kv * ATT_GROUP + g) for g in range(ATT_GROUP)]
        outs += _sink_attention(qs, [_head(k, kv)], [_head(v, kv)], sink_ref, kv, [None])
    o_ref[...] = jnp.concatenate(outs, axis=1).astype(bf16)


def _attend_context(q, k, v, sink):
    return pl.pallas_call(
        _attn_ctx_kernel,
        grid=(BATCH,),
        in_specs=[
            pl.BlockSpec(memory_space=pltpu.SMEM),
            pl.BlockSpec((SEQ, D_MODEL), lambda b: (b, 0)),
            pl.BlockSpec((SEQ, KV_DIM), lambda b: (b, 0)),
            pl.BlockSpec((SEQ, KV_DIM), lambda b: (b, 0)),
        ],
        out_specs=pl.BlockSpec((SEQ, D_MODEL), lambda b: (b, 0)),
        out_shape=jax.ShapeDtypeStruct((N_PROMPT, D_MODEL), jnp.bfloat16),
        compiler_params=_cparams(("arbitrary",)),
        name="attend_context",
    )(sink, q, k, v)


def _rope_tables(width):
    t = np.arange(DEC_SEQ)
    d = np.arange(width) % ATT_HEAD_DIM
    pos = np.where(d[None, :] < ROPE_HALF, (t // GRID_W)[:, None], (t % GRID_W)[:, None]).astype(np.float32)
    inv = (ROPE_BASE ** (-np.arange(ROPE_QUARTER, dtype=np.float32) / ROPE_QUARTER)).astype(np.float32)
    ang = pos * inv[d % ROPE_QUARTER][None, :]
    sign = np.where((d % ROPE_HALF) < ROPE_QUARTER, -1.0, 1.0).astype(np.float32)
    return jnp.asarray(np.cos(ang), jnp.float32), jnp.asarray(np.sin(ang) * sign[None, :], jnp.float32)


def _rope(x, cos, sin_signed):
    width = x.shape[1]
    lane = lax.broadcasted_iota(jnp.int32, (1, width), 1)
    first = (lane % ROPE_HALF) < ROPE_QUARTER
    partner = jnp.where(first, pltpu.roll(x, width - ROPE_QUARTER, 1), pltpu.roll(x, ROPE_QUARTER, 1))
    return x * cos + partner * sin_signed


def _rope_kernel(q_ref, k_ref, v_ref, cos_ref, sin_ref, qo_ref, ko_ref, vo_ref):
    bf16 = jnp.bfloat16
    cos, sin = cos_ref[...], sin_ref[...]
    qo_ref[...] = _rope(q_ref[...], cos, sin).astype(bf16)
    ko_ref[...] = _rope(k_ref[...], cos[:, :KV_DIM], sin[:, :KV_DIM]).astype(bf16)
    vo_ref[...] = v_ref[...].astype(bf16)


def _rope_latent(q, k, v):
    nb = DEC_SEQ // ATT_BLOCK
    off = N_PROMPT // ATT_BLOCK
    cos, sin = _rope_tables(D_MODEL)
    tok = lambda b, i: (off + b * nb + i, 0)
    out = lambda b, i: (b * nb + i, 0)
    return pl.pallas_call(
        _rope_kernel,
        grid=(DEC_BATCH, nb),
        in_specs=[
            pl.BlockSpec((ATT_BLOCK, D_MODEL), tok),
            pl.BlockSpec((ATT_BLOCK, KV_DIM), tok),
            pl.BlockSpec((ATT_BLOCK, KV_DIM), tok),
            pl.BlockSpec((ATT_BLOCK, D_MODEL), lambda b, i: (i, 0)),
            pl.BlockSpec((ATT_BLOCK, D_MODEL), lambda b, i: (i, 0)),
        ],
        out_specs=[
            pl.BlockSpec((ATT_BLOCK, D_MODEL), out),
            pl.BlockSpec((ATT_BLOCK, KV_DIM), out),
            pl.BlockSpec((ATT_BLOCK, KV_DIM), out),
        ],
        out_shape=[
            jax.ShapeDtypeStruct((N_SAMPLE, D_MODEL), jnp.bfloat16),
            jax.ShapeDtypeStruct((N_SAMPLE, KV_DIM), jnp.bfloat16),
            jax.ShapeDtypeStruct((N_SAMPLE, KV_DIM), jnp.bfloat16),
        ],
        compiler_params=_cparams(("arbitrary", "arbitrary")),
        name="rope_latent",
    )(q, k, v, cos, sin)


def _attn_lat_kernel(sink_ref, q_ref, kp_ref, kc_ref, kn_ref, vp_ref, vc_ref, vn_ref, kx_ref, vx_ref, o_ref):
    i = pl.program_id(1)
    nb = pl.num_programs(1)
    q = q_ref[...]
    k_loc = jnp.concatenate([kp_ref[...], kc_ref[...], kn_ref[...]], axis=0)
    v_loc = jnp.concatenate([vp_ref[...], vc_ref[...], vn_ref[...]], axis=0)
    kx, vx = kx_ref[...], vx_ref[...]
    r = lax.broadcasted_iota(jnp.int32, (ATT_BLOCK, 3 * ATT_BLOCK), 0)
    c = lax.broadcasted_iota(jnp.int32, (ATT_BLOCK, 3 * ATT_BLOCK), 1)
    rel = c - ATT_BLOCK - r
    in_window = jnp.logical_and(rel >= -WINDOW, rel <= WINDOW)
    in_seq = jnp.logical_and(jnp.logical_or(c >= ATT_BLOCK, i > 0),
                             jnp.logical_or(c < 2 * ATT_BLOCK, i < nb - 1))
    valid = jnp.logical_and(in_window, in_seq)
    outs = []
    for kv in range(ATT_KV_HEADS):
        qs = [_head(q, kv * ATT_GROUP + g) for g in range(ATT_GROUP)]
        outs += _sink_attention(qs, [_head(k_loc, kv), _head(kx, kv)], [_head(v_loc, kv), _head(vx, kv)],
                                sink_ref, kv, [valid, None])
    o_ref[...] = jnp.concatenate(outs, axis=1).astype(jnp.bfloat16)


def _attend_latent(qr, kr, vb, k_ctx, v_ctx, sink):
    nb = DEC_SEQ // ATT_BLOCK
    cur = lambda b, i: (b * nb + i, 0)
    prv = lambda b, i: (b * nb + jnp.maximum(i - 1, 0), 0)
    nxt = lambda b, i: (b * nb + jnp.minimum(i + 1, nb - 1), 0)
    kvs = lambda f: pl.BlockSpec((ATT_BLOCK, KV_DIM), f)
    ctx = pl.BlockSpec((None, PAST_LEN, KV_DIM), lambda b, i: (b, 0, 0))
    return pl.pallas_call(
        _attn_lat_kernel,
        grid=(DEC_BATCH, nb),
        in_specs=[
            pl.BlockSpec(memory_space=pltpu.SMEM),
            pl.BlockSpec((ATT_BLOCK, D_MODEL), cur),
            kvs(prv), kvs(cur), kvs(nxt), kvs(prv), kvs(cur), kvs(nxt), ctx, ctx,
        ],
        out_specs=pl.BlockSpec((ATT_BLOCK, D_MODEL), cur),
        out_shape=jax.ShapeDtypeStruct((N_SAMPLE, D_MODEL), jnp.bfloat16),
        compiler_params=_cparams(("arbitrary", "arbitrary")),
        name="attend_latent",
    )(sink, qr, kr, kr, kr, vb, vb, vb, k_ctx, v_ctx)


def _gconv_kernel(bg_ref, g_ref, gp_ref, gn_ref, w_ref, o_ref):
    i = pl.program_id(0)
    first = N_PROMPT // CONV_TILE
    per = DEC_SEQ // CONV_TILE
    t = (i - first) % per
    latent = i >= first
    hp = jnp.logical_and(latent, t > 0).astype(jnp.float32)
    hn = jnp.logical_and(latent, t < per - 1).astype(jnp.float32)
    window = jnp.concatenate([gp_ref[...] * hp, g_ref[...], gn_ref[...] * hn], axis=0)
    acc = None
    for k in range(SHORT_CONV):
        off = SUBLANES - SHORT_CONV // 2 + k
        term = w_ref[k:k + 1, :] * window[off:off + CONV_TILE, :]
        acc = term if acc is None else acc + term
    o_ref[...] = (bg_ref[...] * acc).astype(jnp.bfloat16)


def _gated_conv(bg, g, conv_w):
    per = CONV_TILE // SUBLANES
    last = N_TOK // SUBLANES - 1
    cw = jnp.pad(conv_w, ((0, SUBLANES - SHORT_CONV), (0, 0)))
    return pl.pallas_call(
        _gconv_kernel,
        grid=(N_TOK // CONV_TILE,),
        in_specs=[
            pl.BlockSpec((CONV_TILE, D_MODEL), lambda i: (i, 0)),
            pl.BlockSpec((CONV_TILE, D_MODEL), lambda i: (i, 0)),
            pl.BlockSpec((SUBLANES, D_MODEL), lambda i: (jnp.maximum(i * per - 1, 0), 0)),
            pl.BlockSpec((SUBLANES, D_MODEL), lambda i: (jnp.minimum(i * per + per, last), 0)),
            pl.BlockSpec((SUBLANES, D_MODEL), lambda i: (0, 0)),
        ],
        out_specs=pl.BlockSpec((CONV_TILE, D_MODEL), lambda i: (i, 0)),
        out_shape=jax.ShapeDtypeStruct((N_TOK, D_MODEL), jnp.bfloat16),
        compiler_params=_cparams(("arbitrary",)),
        name="gated_conv",
    )(bg, g, g, g, cw)


def kernel(x_prompt, x_sample, cache_k, cache_v, state_ssm, c, c_ctx,
           w_mod, b_mod, ln_w, ln_b,
           w_in_a, conv_w_a, conv_b_a, a_log, dt_bias, d_skip, ssm_norm_w, attn_sink, w_out_a,
           w_in_c, conv_w_c, w_out_c,
           w_router, b_router, w_gate, b_gate, w_lin, b_lin, w_down, b_down):
    bf16 = jnp.bfloat16
    x = (x_prompt.reshape(N_PROMPT, D_MODEL), x_sample.reshape(N_SAMPLE, D_MODEL))

    cond = jnp.concatenate([c_ctx[None, :], c, jnp.zeros((COND_PAD - N_COND, D_MODEL), jnp.float32)], axis=0)
    mod_all = _modulation(cond, w_mod, b_mod).reshape(DEPTH, COND_PAD, 6, D_MODEL)
    mod_all = jnp.pad(mod_all, ((0, 0), (0, 0), (0, MOD_ROWS - 6), (0, 0)))

    mod = mod_all[0]
    wa = w_in_a[0].astype(bf16)
    e = np.cumsum((0, SSM_INNER, SSM_CONV_DIM, 2 * SSM_HEADS, D_MODEL, KV_DIM, KV_DIM))
    w_dt = jnp.pad(wa[:, e[2]:e[3]], ((0, 0), (0, DT_PAD - 2 * SSM_HEADS)))
    w_parts = [wa[:, e[0]:e[1]], wa[:, e[1]:e[2]], w_dt, wa[:, e[3]:e[4]], wa[:, e[4]:e[5]], wa[:, e[5]:e[6]]]
    z, xbc_raw, dt_raw, q, k, v = _inproj(x, mod, w_parts)

    state_in = state_ssm[:, 0].reshape(DEC_BATCH, 2, SSM_INNER, D_STATE)
    y_ssm, h_f, h_b = _ssd_mixer(z, xbc_raw, dt_raw, conv_w_a[0], conv_b_a[0], a_log[0], dt_bias[0],
                                 d_skip[0], ssm_norm_w[0], state_in)

    sink = attn_sink[0]
    att_p = _attend_context(q, k, v, sink)
    qr, kr, vb = _rope_latent(q, k, v)
    k_ctx = cache_k[:, 0].reshape(DEC_BATCH, PAST_LEN, KV_DIM).astype(bf16)
    v_ctx = cache_v[:, 0].reshape(DEC_BATCH, PAST_LEN, KV_DIM).astype(bf16)
    att_s = _attend_latent(qr, kr, vb, k_ctx, v_ctx, sink)
    y_att = jnp.concatenate([att_p, att_s], axis=0)

    wo = w_out_a[0].astype(bf16)
    ln = jnp.stack([ln_w[0, 0], ln_b[0, 0]])
    x1, u2, *routing = _outproj([y_ssm, y_att], [wo[:SSM_INNER], wo[SSM_INNER:]], x, mod, ln,
                                w_router[0], b_router[0])
    x = _moe_layer(0, x1, u2, routing, mod, jnp.stack([ln_w[0, 1], ln_b[0, 1]]),
                   w_gate, b_gate, w_lin, b_lin, w_down, b_down)

    mod = mod_all[1]
    wc = w_in_c[0].astype(bf16)
    bg, g = _inproj(x, mod, [wc[:, :D_MODEL], wc[:, D_MODEL:2 * D_MODEL], wc[:, 2 * D_MODEL:]], gate_product=True)
    y_c = _gated_conv(bg, g, conv_w_c[0])
    ln = jnp.stack([ln_w[1, 0], ln_b[1, 0]])
    x1, u2, *routing = _outproj([y_c], [w_out_c[0].astype(bf16)], x, mod, ln, w_router[1], b_router[1])
    y_p, y_s = _moe_layer(1, x1, u2, routing, mod, jnp.stack([ln_w[1, 1], ln_b[1, 1]]),
                          w_gate, b_gate, w_lin, b_lin, w_down, b_down, split=True)

    y_prompt = y_p.reshape(BATCH, SEQ, D_MODEL)
    y_sample = y_s.reshape(DEC_BATCH, DEC_SEQ, D_MODEL)
    new_k = k[:N_PROMPT].reshape(BATCH, 1, SEQ, ATT_KV_HEADS, ATT_HEAD_DIM)
    new_v = v[:N_PROMPT].reshape(BATCH, 1, SEQ, ATT_KV_HEADS, ATT_HEAD_DIM)
    new_state = jnp.stack([h_f, h_b], axis=1).reshape(BATCH, 1, 2, SSM_HEADS, SSM_HEAD_DIM, D_STATE)
    return (y_prompt, y_sample, new_k, new_v, new_state)
```

```python
import functools

import numpy as np
import jax
import jax.numpy as jnp
from jax import lax
from jax.experimental import pallas as pl
from jax.experimental.pallas import tpu as pltpu

D_MODEL = 1024
BATCH = 32
SEQ = 256
DEPTH = 2
DEC_BATCH = 4
DEC_SEQ = 1024
PAST_LEN = 512
GRID_W = 64
SSM_HEAD_DIM = 64
SSM_INNER = D_MODEL
SSM_HEADS = SSM_INNER // SSM_HEAD_DIM
SSM_GROUPS = 2
SSM_HEADS_PER_GROUP = SSM_HEADS // SSM_GROUPS
D_STATE = 128
SSM_CONV = 5
SSM_CHUNK = 128
SSM_BC = SSM_GROUPS * D_STATE
SSM_CONV_DIM = SSM_INNER + 2 * SSM_BC
ATT_HEAD_DIM = 64
ATT_HEADS = D_MODEL // ATT_HEAD_DIM
ATT_KV_HEADS = 4
ATT_GROUP = ATT_HEADS // ATT_KV_HEADS
WINDOW = 128
ATT_BLOCK = 128
ATT_SCALE = ATT_HEAD_DIM ** -0.5
ROPE_BASE = 10000.0
ROPE_HALF = ATT_HEAD_DIM // 2
ROPE_QUARTER = ATT_HEAD_DIM // 4
SHORT_CONV = 3
N_EXPERTS = 32
TOP_K = 4
D_EXPERT = D_MODEL
SWIGLU_ALPHA = 1.702
SWIGLU_LIMIT = 7.0
N_EVEN = (DEPTH + 1) // 2
DEEPNORM_ALPHA = (2 * DEPTH) ** 0.25
LN_EPS = 1e-5
RMS_EPS = 1e-5
KV_DIM = ATT_KV_HEADS * ATT_HEAD_DIM

N_PROMPT = BATCH * SEQ
N_SAMPLE = DEC_BATCH * DEC_SEQ
N_TOK = N_PROMPT + N_SAMPLE
N_COND = 1 + DEC_BATCH
SUBLANES = 8
LANES = 128
COND_PAD = SUBLANES
MOD_ROWS = SUBLANES
DT_PAD = LANES

TOK_TILE = 512
CONV_TILE = 256
MOE_TILE = 256
MOE_ROWS = N_TOK * TOP_K + N_EXPERTS * MOE_TILE
MOE_TILES = MOE_ROWS // MOE_TILE
VMEM_LIMIT = 56 * 1024 * 1024
NEG_BIG = -1e30
WEIGHT_DMA_PRIORITY = 1
ROW_TILES = D_MODEL // LANES

assert N_PROMPT % TOK_TILE == 0 and DEC_SEQ % TOK_TILE == 0
PROMPT_TILES = N_PROMPT // TOK_TILE
assert SEQ % CONV_TILE == 0 and DEC_SEQ % CONV_TILE == 0

_NT = (((1,), (1,)), ((), ()))
_TN = (((0,), (0,)), ((), ()))


def _cparams(sem):
    return pltpu.CompilerParams(dimension_semantics=sem, vmem_limit_bytes=VMEM_LIMIT)


def _cond_row(i):
    first = N_PROMPT // TOK_TILE
    per = DEC_SEQ // TOK_TILE
    return jnp.where(i < first, 0, 1 + (i - first) // per)


def _bf16_dot(a, b):
    return jnp.dot(a.astype(jnp.bfloat16), b.astype(jnp.bfloat16), preferred_element_type=jnp.float32)


def _mod_kernel(c_ref, w_ref, b_ref, o_ref):
    c = c_ref[...]
    s = c * jax.nn.sigmoid(c)
    o_ref[...] = _bf16_dot(s, w_ref[...]) + b_ref[...]


def _modulation(cond, w_mod, b_mod):
    tn = 1536
    return pl.pallas_call(
        _mod_kernel,
        grid=(DEPTH, 6 * D_MODEL // tn),
        in_specs=[
            pl.BlockSpec((COND_PAD, D_MODEL), lambda l, n: (0, 0)),
            pl.BlockSpec((None, D_MODEL, tn), lambda l, n: (l, 0, n)),
            pl.BlockSpec((None, 1, tn), lambda l, n: (l, 0, n)),
        ],
        out_specs=pl.BlockSpec((None, COND_PAD, tn), lambda l, n: (l, 0, n)),
        out_shape=jax.ShapeDtypeStruct((DEPTH, COND_PAD, 6 * D_MODEL), jnp.float32),
        compiler_params=_cparams(("arbitrary", "arbitrary")),
        name="modulation",
    )(cond, w_mod, b_mod.reshape(DEPTH, 1, 6 * D_MODEL))


def _stream_specs(x):
    if not isinstance(x, tuple):
        return [pl.BlockSpec((TOK_TILE, D_MODEL), lambda i: (i, 0))]
    return [pl.BlockSpec((TOK_TILE, D_MODEL), lambda i: (jnp.minimum(i, PROMPT_TILES - 1), 0)),
            pl.BlockSpec((TOK_TILE, D_MODEL), lambda i: (jnp.maximum(i - PROMPT_TILES, 0), 0))]


def _stream_tile(x_refs):
    if len(x_refs) == 1:
        return x_refs[0][...]
    return jnp.where(pl.program_id(0) < PROMPT_TILES, x_refs[0][...], x_refs[1][...])


def _inproj_kernel(*refs, n_x, gate_product):
    x_refs, mod_ref, refs = refs[:n_x], refs[n_x], refs[n_x + 1:]
    n = len(refs) // 2 + (1 if gate_product else 0)
    w_refs, o_refs = refs[:n], refs[n:]
    u = _stream_tile(x_refs) * (1.0 + mod_ref[1:2, :]) + mod_ref[0:1, :]
    ub = u.astype(jnp.bfloat16)
    outs = [jnp.dot(ub, w_ref[...], preferred_element_type=jnp.float32) for w_ref in w_refs]
    if gate_product:
        outs = outs[:-2] + [outs[-2] * outs[-1]]
    for o_ref, o in zip(o_refs, outs):
        o_ref[...] = o


def _inproj(x, mod, weights, gate_product=False):
    out_w = [w.shape[1] for w in weights]
    if gate_product:
        out_w = out_w[:-1]
    xs = x if isinstance(x, tuple) else (x,)
    return pl.pallas_call(
        functools.partial(_inproj_kernel, n_x=len(xs), gate_product=gate_product),
        grid=(N_TOK // TOK_TILE,),
        in_specs=_stream_specs(x)
        + [pl.BlockSpec((None, MOD_ROWS, D_MODEL), lambda i: (_cond_row(i), 0, 0))]
        + [pl.BlockSpec(w.shape, lambda i: (0, 0)) for w in weights],
        out_specs=[pl.BlockSpec((TOK_TILE, n), lambda i: (i, 0)) for n in out_w],
        out_shape=[jax.ShapeDtypeStruct((N_TOK, n), jnp.float32) for n in out_w],
        compiler_params=_cparams(("arbitrary",)),
        name="inproj",
    )(*xs, mod, *weights)


def _layer_norm(v, w, b):
    mu = jnp.mean(v, axis=-1, keepdims=True)
    d = v - mu
    var = jnp.mean(d * d, axis=-1, keepdims=True)
    return d * lax.rsqrt(var + LN_EPS) * w + b


def _outproj_kernel(*refs, y_counts, n_x):
    n_in, n_y = len(y_counts), sum(y_counts)
    y_refs = refs[:n_y]
    w_refs = refs[n_y:n_y + n_in]
    x_refs = refs[n_y + n_in:n_y + n_in + n_x]
    (mod_ref, ln_ref, wr_ref, br_ref,
     x1_ref, u2_ref, idx_ref, gate_ref, rank_ref, cnt_ref, run_ref) = refs[n_y + n_in + n_x:]
    f32, bf16 = jnp.float32, jnp.bfloat16

    @pl.when(pl.program_id(0) == 0)
    def _():
        run_ref[...] = jnp.zeros_like(run_ref)

    mix, first = None, 0
    for count, w_ref in zip(y_counts, w_refs):
        part = _bf16_dot(_stream_tile(y_refs[first:first + count]), w_ref[...])
        mix = part if mix is None else mix + part
        first += count
    x1 = _layer_norm(DEEPNORM_ALPHA * _stream_tile(x_refs) + mod_ref[2:3, :] * mix, ln_ref[0:1, :], ln_ref[1:2, :])
    x1_ref[...] = x1
    u2 = x1 * (1.0 + mod_ref[4:5, :]) + mod_ref[3:4, :]
    u2_ref[...] = u2
    wr = wr_ref[...]
    u_hi, w_hi = u2.astype(bf16), wr.astype(bf16)
    u_lo, w_lo = (u2 - u_hi.astype(f32)).astype(bf16), (wr - w_hi.astype(f32)).astype(bf16)
    logits = (jnp.dot(u_hi, w_hi, preferred_element_type=f32)
              + (jnp.dot(u_lo, w_hi, preferred_element_type=f32) + jnp.dot(u_hi, w_lo, preferred_element_type=f32))
              + br_ref[...])

    lane = lax.broadcasted_iota(jnp.int32, (TOK_TILE, N_EXPERTS), 1)
    work = logits
    vals, sels, idxs = [], [], []
    for _ in range(TOP_K):
        m = jnp.max(work, axis=-1, keepdims=True)
        idx = jnp.min(jnp.where(work == m, lane, N_EXPERTS), axis=-1, keepdims=True)
        sel = lane == idx
        vals.append(m)
        idxs.append(idx)
        sels.append(sel)
        work = jnp.where(sel, -jnp.inf, work)
    exps = [jnp.exp(v - vals[0]) for v in vals]
    den = exps[0]
    for e in exps[1:]:
        den = den + e
    onehot = sels[0].astype(f32)
    for sel in sels[1:]:
        onehot = onehot + sel.astype(f32)
    r = lax.broadcasted_iota(jnp.int32, (TOK_TILE, TOK_TILE), 0)
    c = lax.broadcasted_iota(jnp.int32, (TOK_TILE, TOK_TILE), 1)
    before = jnp.dot((r > c).astype(jnp.bfloat16), onehot.astype(jnp.bfloat16), preferred_element_type=f32)
    rank_all = before + run_ref[...]
    run = run_ref[...] + jnp.sum(onehot, axis=0, keepdims=True)
    run_ref[...] = run
    cnt_ref[...] = run.astype(jnp.int32)
    col = lax.broadcasted_iota(jnp.int32, (TOK_TILE, TOP_K), 1)
    idx_o = jnp.zeros((TOK_TILE, TOP_K), jnp.int32)
    gate_o = jnp.zeros((TOK_TILE, TOP_K), f32)
    rank_o = jnp.zeros((TOK_TILE, TOP_K), f32)
    for k in range(TOP_K):
        rank_k = jnp.sum(jnp.where(sels[k], rank_all, 0.0), axis=-1, keepdims=True)
        idx_o = jnp.where(col == k, idxs[k], idx_o)
        gate_o = jnp.where(col == k, exps[k] / den, gate_o)
        rank_o = jnp.where(col == k, rank_k, rank_o)
    idx_ref[...] = idx_o
    gate_ref[...] = gate_o
    rank_ref[...] = rank_o.astype(jnp.int32)


def _outproj(ys, ws, x, mod, ln, w_router, b_router):
    xs = x if isinstance(x, tuple) else (x,)
    y_parts = [y if isinstance(y, tuple) else (y,) for y in ys]
    y_flat = [part for parts in y_parts for part in parts]
    return pl.pallas_call(
        functools.partial(_outproj_kernel, y_counts=tuple(len(parts) for parts in y_parts), n_x=len(xs)),
        grid=(N_TOK // TOK_TILE,),
        in_specs=[spec for y in ys for spec in _stream_specs(y)]
        + [pl.BlockSpec(w.shape, lambda i: (0, 0)) for w in ws]
        + _stream_specs(x)
        + [
            pl.BlockSpec((None, MOD_ROWS, D_MODEL), lambda i: (_cond_row(i), 0, 0)),
            pl.BlockSpec((2, D_MODEL), lambda i: (0, 0)),
            pl.BlockSpec((D_MODEL, N_EXPERTS), lambda i: (0, 0)),
            pl.BlockSpec((1, N_EXPERTS), lambda i: (0, 0)),
        ],
        out_specs=[
            pl.BlockSpec((TOK_TILE, D_MODEL), lambda i: (i, 0)),
            pl.BlockSpec((TOK_TILE, D_MODEL), lambda i: (i, 0)),
            pl.BlockSpec((TOK_TILE, TOP_K), lambda i: (i, 0)),
            pl.BlockSpec((TOK_TILE, TOP_K), lambda i: (i, 0)),
            pl.BlockSpec((TOK_TILE, TOP_K), lambda i: (i, 0)),
            pl.BlockSpec((1, N_EXPERTS), lambda i: (0, 0)),
        ],
        out_shape=[
            jax.ShapeDtypeStruct((N_TOK, D_MODEL), jnp.float32),
            jax.ShapeDtypeStruct((N_TOK, D_MODEL), jnp.float32),
            jax.ShapeDtypeStruct((N_TOK, TOP_K), jnp.int32),
            jax.ShapeDtypeStruct((N_TOK, TOP_K), jnp.float32),
            jax.ShapeDtypeStruct((N_TOK, TOP_K), jnp.int32),
            jax.ShapeDtypeStruct((1, N_EXPERTS), jnp.int32),
        ],
        scratch_shapes=[pltpu.VMEM((1, N_EXPERTS), jnp.float32)],
        compiler_params=_cparams(("arbitrary",)),
        name="outproj_ln_router",
    )(*y_flat, *ws, *xs, mod, ln, w_router, b_router.reshape(1, N_EXPERTS))


def _moe_kernel(te_ref, nv_ref, src0_ref, last_ref, wslot_ref, nexte_ref, stok_ref, x_hbm,
                wg_hbm, bg_ref, wl_hbm, bl_ref, wd_hbm, bd_ref, o_ref,
                w_bf, xbuf, sem, wst, wsem, *, layer):
    i = pl.program_id(0)
    nv = nv_ref[0]
    valid = i < nv
    prev = te_ref[jnp.maximum(i - 1, 0)]
    new_expert = jnp.logical_or(i == 0, te_ref[i] != prev)

    def start_gather(tile, buf):
        base, last = src0_ref[tile], last_ref[tile]
        for r in range(MOE_TILE):
            tok = stok_ref[jnp.minimum(base + r, last)]
            pltpu.make_async_copy(x_hbm.at[tok], xbuf.at[buf, r], sem.at[buf]).start()

    def wait_gather(buf):
        pltpu.make_async_copy(x_hbm.at[pl.ds(0, MOE_TILE)], xbuf.at[buf], sem.at[buf]).wait()

    def weight_copies(expert, slot):
        return [pltpu.make_async_copy(w.at[layer, expert], wst.at[slot, j], wsem.at[slot])
                for j, w in enumerate((wg_hbm, wl_hbm, wd_hbm))]

    @pl.when(i == 0)
    def _():
        start_gather(0, 0)
        for cp in weight_copies(te_ref[0], 0):
            cp.start(priority=WEIGHT_DMA_PRIORITY)

    @pl.when(jnp.logical_and(valid, new_expert))
    def _():
        slot = wslot_ref[i]
        nxt = nexte_ref[i]

        @pl.when(nxt >= 0)
        def _():
            for cp in weight_copies(nxt, 1 - slot):
                cp.start(priority=WEIGHT_DMA_PRIORITY)

        for cp in weight_copies(te_ref[i], slot):
            cp.wait()
        for j in range(3):
            w_bf[j] = wst[slot, j].astype(jnp.bfloat16)

    for buf in range(2):
        @pl.when(jnp.logical_and(valid, i % 2 == buf))
        def _():
            wait_gather(buf)
            start_gather(jnp.minimum(i + 1, nv - 1), 1 - buf)
            x = jnp.concatenate([xbuf[buf, :, s, :] for s in range(ROW_TILES)], axis=1).astype(jnp.bfloat16)
            hg = jnp.dot(x, w_bf[0], preferred_element_type=jnp.float32) + bg_ref[...]
            hl = jnp.dot(x, w_bf[1], preferred_element_type=jnp.float32) + bl_ref[...]
            hg = jnp.minimum(hg, SWIGLU_LIMIT)
            hl = jnp.clip(hl, -SWIGLU_LIMIT, SWIGLU_LIMIT)
            h = hg * jax.nn.sigmoid(SWIGLU_ALPHA * hg) * (hl + 1.0)
            o_ref[...] = (jnp.dot(h.astype(jnp.bfloat16), w_bf[2], preferred_element_type=jnp.float32)
                          + bd_ref[...])

        @pl.when(jnp.logical_and(i == nv - 1, i % 2 == buf))
        def _():
            wait_gather(1 - buf)

    @pl.when(jnp.logical_not(valid))
    def _():
        o_ref[...] = jnp.zeros_like(o_ref)


def _moe_grouped(layer, tables, sorted_tok, x, w_gate, b_gate, w_lin, b_lin, w_down, b_down):
    hbm = pl.BlockSpec(memory_space=pl.ANY)
    bspec = pl.BlockSpec((None, None, 1, D_EXPERT), lambda i, te, *_: (layer, te[i], 0, 0))
    grid_spec = pltpu.PrefetchScalarGridSpec(
        num_scalar_prefetch=len(tables) + 1,
        grid=(MOE_TILES,),
        in_specs=[hbm, hbm, bspec, hbm, bspec, hbm, bspec],
        out_specs=pl.BlockSpec((MOE_TILE, D_MODEL), lambda i, *_: (i, 0)),
        scratch_shapes=[
            pltpu.VMEM((3, D_MODEL, D_EXPERT), jnp.bfloat16),
            pltpu.VMEM((2, MOE_TILE, ROW_TILES, LANES), jnp.float32), pltpu.SemaphoreType.DMA((2,)),
            pltpu.VMEM((2, 3, D_MODEL, D_EXPERT), jnp.float32), pltpu.SemaphoreType.DMA((2,)),
        ],
    )
    bshape = (DEPTH, N_EXPERTS, 1, D_EXPERT)
    return pl.pallas_call(
        functools.partial(_moe_kernel, layer=layer),
        grid_spec=grid_spec,
        out_shape=jax.ShapeDtypeStruct((MOE_ROWS, D_MODEL), jnp.float32),
        compiler_params=_cparams(("arbitrary",)),
        name="moe_grouped",
    )(*tables, sorted_tok, x.reshape(N_TOK, ROW_TILES, LANES), w_gate, b_gate.reshape(bshape),
      w_lin, b_lin.reshape(bshape), w_down, b_down.reshape(bshape))


def _combine_kernel(y_ref, g_ref, x_ref, mod_ref, ln_ref, *o_refs):
    g = g_ref[...]
    ff = g[:, 0:1] * y_ref[0]
    for k in range(1, TOP_K):
        ff = ff + g[:, k:k + 1] * y_ref[k]
    out = _layer_norm(DEEPNORM_ALPHA * x_ref[...] + mod_ref[5:6, :] * ff, ln_ref[0:1, :], ln_ref[1:2, :])
    if len(o_refs) == 1:
        o_refs[0][...] = out
    else:
        is_prompt = pl.program_id(0) < PROMPT_TILES

        @pl.when(is_prompt)
        def _():
            o_refs[0][...] = out

        @pl.when(jnp.logical_not(is_prompt))
        def _():
            o_refs[1][...] = out


def _combine(yk, gates, x, mod, ln, split=False):
    if split:
        out_specs = [pl.BlockSpec((TOK_TILE, D_MODEL), lambda i: (jnp.minimum(i, PROMPT_TILES - 1), 0)),
                     pl.BlockSpec((TOK_TILE, D_MODEL), lambda i: (jnp.maximum(i - PROMPT_TILES, 0), 0))]
        out_shape = [jax.ShapeDtypeStruct((N_PROMPT, D_MODEL), jnp.float32),
                     jax.ShapeDtypeStruct((N_SAMPLE, D_MODEL), jnp.float32)]
    else:
        out_specs = pl.BlockSpec((TOK_TILE, D_MODEL), lambda i: (i, 0))
        out_shape = jax.ShapeDtypeStruct((N_TOK, D_MODEL), jnp.float32)
    return pl.pallas_call(
        _combine_kernel,
        grid=(N_TOK // TOK_TILE,),
        in_specs=[
            pl.BlockSpec((TOP_K, TOK_TILE, D_MODEL), lambda i: (0, i, 0)),
            pl.BlockSpec((TOK_TILE, TOP_K), lambda i: (i, 0)),
            pl.BlockSpec((TOK_TILE, D_MODEL), lambda i: (i, 0)),
            pl.BlockSpec((None, MOD_ROWS, D_MODEL), lambda i: (_cond_row(i), 0, 0)),
            pl.BlockSpec((2, D_MODEL), lambda i: (0, 0)),
        ],
        out_specs=out_specs,
        out_shape=out_shape,
        compiler_params=_cparams(("arbitrary",)),
        name="combine_ln",
    )(yk, gates, x, mod, ln)


def _route(top_idx, rank, counts):
    counts = counts.reshape(N_EXPERTS)
    padded = (counts + MOE_TILE - 1) // MOE_TILE * MOE_TILE
    pend = jnp.cumsum(padded)
    pstart = pend - padded
    start = jnp.cumsum(counts) - counts
    experts = jnp.arange(N_EXPERTS, dtype=jnp.int32)
    lookup = lambda table, idx: jnp.sum(jnp.where(idx[..., None] == experts, table, 0), axis=-1)
    slot = lookup(pstart, top_idx) + rank
    key = lookup(start, top_idx) + rank
    tok = jnp.broadcast_to(jnp.arange(N_TOK, dtype=jnp.int32)[:, None], (N_TOK, TOP_K))
    _, sorted_tok = lax.sort((key.reshape(-1), tok.reshape(-1)), num_keys=1)
    n_valid = (pend[-1] // MOE_TILE).astype(jnp.int32)
    tile_start = jnp.arange(MOE_TILES, dtype=jnp.int32) * MOE_TILE
    tile_expert = jnp.sum((tile_start[:, None] >= pend[None, :]).astype(jnp.int32), axis=1)
    tile_expert = jnp.where(tile_start < pend[-1], tile_expert, tile_expert[n_valid - 1])
    src0 = tile_start - lookup(pstart, tile_expert) + lookup(start, tile_expert)
    src_last = lookup(start, tile_expert) + lookup(counts, tile_expert) - 1
    used = (counts > 0).astype(jnp.int32)
    order = jnp.cumsum(used) - used
    later = jnp.where(jnp.logical_and(experts[None, :] > experts[:, None], used[None, :] > 0),
                      experts[None, :], N_EXPERTS)
    successor = jnp.min(later, axis=1)
    successor = jnp.where(successor < N_EXPERTS, successor, -1)
    wslot = lookup(order, tile_expert) % 2
    next_expert = lookup(successor, tile_expert)
    tables = (tile_expert, n_valid.reshape(1), src0, src_last, wslot, next_expert)
    return slot, sorted_tok, tables


def _moe_layer(i, x1, u2, routing, mod, ln, w_gate, b_gate, w_lin, b_lin, w_down, b_down, split=False):
    top_idx, gates, rank, counts = routing
    slot, sorted_tok, tables = _route(top_idx, rank, counts)
    y = _moe_grouped(i, tables, sorted_tok, u2, w_gate, b_gate, w_lin, b_lin, w_down, b_down)
    yk = jnp.take(y, slot.T, axis=0, mode="clip")
    return _combine(yk, gates, x1, mod, ln, split)


_F_ROW, _F_FIRST, _F_LAST, _F_PREV8, _F_HASPREV, _F_NEXT8, _F_HASNEXT, _F_H0, _F_USEH0, _F_HOUT, _F_WRITEH = range(11)
_N_FIELDS = 11
_SSD_STEPS = N_TOK // SSM_CHUNK


def _ssd_table(reverse):
    rows = []
    for prompt, nseq, slen, base in ((True, BATCH, SEQ, 0), (False, DEC_BATCH, DEC_SEQ, N_PROMPT)):
        nc = slen // SSM_CHUNK
        for s in range(nseq):
            for c in range(nc):
                row = (base + s * slen) // SSM_CHUNK + c
                first, last = (c == nc - 1, c == 0) if reverse else (c == 0, c == nc - 1)
                per = SSM_CHUNK // SUBLANES
                rows.append([
                    row, int(first), int(last),
                    max(row * per - 1, 0), int(c > 0),
                    min(row * per + per, N_TOK // SUBLANES - 1), int(c < nc - 1),
                    0 if prompt else s, int(not prompt),
                    s if prompt else BATCH - 1, int(prompt and last),
                ])
    if reverse:
        rows = rows[::-1]
    return np.asarray(rows, np.int32).T.reshape(-1)


def _fld(tbl, f, s):
    return tbl[f * _SSD_STEPS + s]


def _exact_expand(a, e_ref, passes):
    k = a.shape[1]
    pieces, rem = [], a
    for _ in range(passes):
        piece = rem.astype(jnp.bfloat16)
        rem = rem - piece.astype(jnp.float32)
        pieces.append(piece)
    out = None
    for i in range(0, passes, 2):
        pair = pieces[i:i + 2]
        lhs = pair[0] if len(pair) == 1 else jnp.concatenate(pair, axis=1)
        t = jnp.dot(lhs, e_ref[0:k * len(pair), :], preferred_element_type=jnp.float32)
        out = t if out is None else out + t
    return out


def _ssd_chunk(xs, bmat, cmat, dt_raw, dtb_row, alog_row, ex_ref, eb_ref, s_ref, reverse):
    f32, bf16 = jnp.float32, jnp.bfloat16
    base = SSM_HEADS if reverse else 0
    lane = lax.broadcasted_iota(jnp.int32, (1, DT_PAD), 1)
    head_cols = jnp.logical_and(lane >= base, lane < base + SSM_HEADS)
    xr = dt_raw + dtb_row
    dt = jnp.maximum(xr, 0.0) + jnp.log1p(jnp.exp(-jnp.abs(xr)))
    a_row = jnp.where(head_cols, -jnp.exp(alog_row), 0.0)
    d_a = dt * a_row
    r = lax.broadcasted_iota(jnp.int32, (SSM_CHUNK, SSM_CHUNK), 0)
    c = lax.broadcasted_iota(jnp.int32, (SSM_CHUNK, SSM_CHUNK), 1)
    tri = (r <= c) if reverse else (r >= c)
    cum = jnp.dot(tri.astype(f32), d_a, preferred_element_type=f32, precision=lax.Precision.HIGHEST)
    cum_t = cum.T
    total = cum[0:1, :] if reverse else cum[SSM_CHUNK - 1:SSM_CHUNK, :]
    stacked = jnp.concatenate([dt, jnp.exp(cum), dt * jnp.exp(total - cum),
                               jnp.broadcast_to(jnp.exp(total), (SUBLANES, DT_PAD))], axis=0)
    stacked_x = _exact_expand(stacked, ex_ref, 2)
    dt_x = stacked_x[0:SSM_CHUNK]
    e_in_x = stacked_x[SSM_CHUNK:2 * SSM_CHUNK]
    w_end_x = stacked_x[2 * SSM_CHUNK:3 * SSM_CHUNK]
    dec_x = stacked_x[3 * SSM_CHUNK:3 * SSM_CHUNK + 1]
    cum_b = _exact_expand(cum, eb_ref, 3)
    xdt = xs * dt_x
    xw = (xs * w_end_x).astype(bf16)
    s_in = s_ref[...]
    s_bf = s_in.astype(bf16)
    lo_half = lax.broadcasted_iota(jnp.int32, (1, 2 * SSM_HEAD_DIM), 1) < SSM_HEAD_DIM
    gw = SSM_HEADS_PER_GROUP * SSM_HEAD_DIM
    pieces = []
    for g in range(SSM_GROUPS):
        bg = bmat[:, g * D_STATE:(g + 1) * D_STATE]
        cg = cmat[:, g * D_STATE:(g + 1) * D_STATE]
        gcols = slice(g * gw, (g + 1) * gw)
        cb = lax.dot_general(cg, bg, _NT, preferred_element_type=f32)
        y_off = jnp.dot(cg, s_bf[:, gcols], preferred_element_type=f32) * e_in_x[:, gcols]
        upd = lax.dot_general(bg, xw[:, gcols], _TN, preferred_element_type=f32)
        s_ref[:, gcols] = s_in[:, gcols] * dec_x[:, gcols] + upd
        for q in range(SSM_HEADS_PER_GROUP // 2):
            h0 = g * SSM_HEADS_PER_GROUP + 2 * q
            pcols = slice(h0 * SSM_HEAD_DIM, (h0 + 2) * SSM_HEAD_DIM)
            xp = xdt[:, pcols]
            bd = jnp.concatenate([jnp.where(lo_half, xp, 0.0), jnp.where(lo_half, 0.0, xp)], axis=0).astype(bf16)
            ms = []
            for h in (h0, h0 + 1):
                ch = base + h
                seg = cum_b[:, h * SSM_CHUNK:(h + 1) * SSM_CHUNK] - cum_t[ch:ch + 1, :]
                decay = jnp.exp(jnp.where(tri, seg, NEG_BIG))
                ms.append((cb * decay).astype(bf16))
            y_diag = jnp.dot(jnp.concatenate(ms, axis=1), bd, preferred_element_type=f32)
            pieces.append(y_diag + y_off[:, 2 * q * SSM_HEAD_DIM:(2 * q + 2) * SSM_HEAD_DIM])
    return jnp.concatenate(pieces, axis=1)


def _ssd_init_state(tbl, step, h0_ref, s_ref):
    @pl.when(_fld(tbl, _F_FIRST, step) == 1)
    def _():
        use = _fld(tbl, _F_USEH0, step) == 1
        s_ref[...] = jnp.where(use, h0_ref[...], 0.0).T


def _ssd_write_state(tbl, step, hout_ref, s_ref):
    @pl.when(_fld(tbl, _F_WRITEH, step) == 1)
    def _():
        hout_ref[...] = s_ref[...].T


def _ssd_fwd_kernel(tbl, xc_ref, xp_ref, xn_ref, dt_ref, cw_ref, cbias_ref, dtb_ref, alog_ref, ex_ref, eb_ref,
                    h0_ref, xs_ref, bc_ref, yf_ref, hout_ref, s_ref):
    step = pl.program_id(0)
    _ssd_init_state(tbl, step, h0_ref, s_ref)
    hp = (_fld(tbl, _F_HASPREV, step) == 1).astype(jnp.float32)
    hn = (_fld(tbl, _F_HASNEXT, step) == 1).astype(jnp.float32)
    window = jnp.concatenate([xp_ref[...] * hp, xc_ref[...], xn_ref[...] * hn], axis=0)
    acc = cbias_ref[...] + cw_ref[0:1, :] * window[SUBLANES - 2:SUBLANES - 2 + SSM_CHUNK, :]
    for k in range(1, SSM_CONV):
        off = SUBLANES - SSM_CONV // 2 + k
        acc = acc + cw_ref[k:k + 1, :] * window[off:off + SSM_CHUNK, :]
    xbc = acc * jax.nn.sigmoid(acc)
    xs = xbc[:, :SSM_INNER]
    bc = xbc[:, SSM_INNER:].astype(jnp.bfloat16)
    xs_ref[...] = xs
    bc_ref[...] = bc
    yf_ref[...] = _ssd_chunk(xs, bc[:, :SSM_BC], bc[:, SSM_BC:], dt_ref[...], dtb_ref[...], alog_ref[...],
                             ex_ref, eb_ref, s_ref, reverse=False)
    _ssd_write_state(tbl, step, hout_ref, s_ref)


def _ssd_bwd_kernel(tbl, xs_ref, bc_ref, dt_ref, yf_ref, z_ref, dtb_ref, alog_ref, dskip_ref, nw_ref,
                    ex_ref, eb_ref, h0_ref, hf_ref, y_ref, hout_ref, s_ref):
    step = pl.program_id(0)
    _ssd_init_state(tbl, step, h0_ref, s_ref)
    xs = xs_ref[...]
    bc = bc_ref[...]
    yb = _ssd_chunk(xs, bc[:, :SSM_BC], bc[:, SSM_BC:], dt_ref[...], dtb_ref[...], alog_ref[...],
                    ex_ref, eb_ref, s_ref, reverse=True)
    z = z_ref[...]
    hg = (yf_ref[...] + yb + dskip_ref[...] * xs) * (z * jax.nn.sigmoid(z))
    gw = SSM_INNER // SSM_GROUPS
    outs = []
    for g in range(SSM_GROUPS):
        hgg = hg[:, g * gw:(g + 1) * gw]
        outs.append(hgg * lax.rsqrt(jnp.mean(hgg * hgg, axis=-1, keepdims=True) + RMS_EPS))
    y_ref[...] = (jnp.concatenate(outs, axis=1) * nw_ref[...]).astype(jnp.bfloat16)

    @pl.when(_fld(tbl, _F_WRITEH, step) == 1)
    def _():
        hout_ref[0] = hf_ref[...]
        hout_ref[1] = s_ref[...].T


def _ssd_mixer(z, xbc_raw, dt_raw, conv_w, conv_b, a_log, dt_bias, d_skip, norm_w, state_in):
    f32 = jnp.float32
    row = lambda f: (lambda s, tbl: (_fld(tbl, f, s), 0))
    chunk_spec = lambda w: pl.BlockSpec((SSM_CHUNK, w), row(_F_ROW))
    const_spec = lambda shape: pl.BlockSpec(shape, lambda s, tbl: (0,) * len(shape))
    h0_spec = lambda d: pl.BlockSpec((None, None, SSM_INNER, D_STATE), lambda s, tbl: (_fld(tbl, _F_H0, s), d, 0, 0))
    hout_spec = pl.BlockSpec((None, SSM_INNER, D_STATE), lambda s, tbl: (_fld(tbl, _F_HOUT, s), 0, 0))
    hout_shape = jax.ShapeDtypeStruct((BATCH, SSM_INNER, D_STATE), f32)
    state_scratch = [pltpu.VMEM((D_STATE, SSM_INNER), f32)]

    def expanders(reverse):
        base = SSM_HEADS if reverse else 0
        ex = np.zeros((DT_PAD, SSM_INNER), np.float32)
        eb = np.zeros((DT_PAD, SSM_HEADS * SSM_CHUNK), np.float32)
        for h in range(SSM_HEADS):
            ex[base + h, h * SSM_HEAD_DIM:(h + 1) * SSM_HEAD_DIM] = 1.0
            eb[base + h, h * SSM_CHUNK:(h + 1) * SSM_CHUNK] = 1.0
        twice = lambda e: jnp.asarray(np.concatenate([e, e], axis=0), jnp.bfloat16)
        return twice(ex), twice(eb)

    expander_specs = [const_spec((2 * DT_PAD, SSM_INNER)), const_spec((2 * DT_PAD, SSM_HEADS * SSM_CHUNK))]

    cw = jnp.pad(conv_w, ((0, SUBLANES - SSM_CONV), (0, 0)))
    cbias = conv_b.reshape(1, SSM_CONV_DIM)
    dtb = jnp.pad(dt_bias.reshape(1, 2 * SSM_HEADS), ((0, 0), (0, DT_PAD - 2 * SSM_HEADS)))
    alog = jnp.pad(a_log.reshape(1, 2 * SSM_HEADS), ((0, 0), (0, DT_PAD - 2 * SSM_HEADS)))
    dskip = jnp.repeat(d_skip, SSM_HEAD_DIM).reshape(1, SSM_INNER)
    nw = norm_w.reshape(1, SSM_INNER)

    xs, bc, yf, h_f = pl.pallas_call(
        _ssd_fwd_kernel,
        grid_spec=pltpu.PrefetchScalarGridSpec(
            num_scalar_prefetch=1,
            grid=(_SSD_STEPS,),
            in_specs=[
                chunk_spec(SSM_CONV_DIM),
                pl.BlockSpec((SUBLANES, SSM_CONV_DIM), row(_F_PREV8)),
                pl.BlockSpec((SUBLANES, SSM_CONV_DIM), row(_F_NEXT8)),
                chunk_spec(DT_PAD),
                const_spec((SUBLANES, SSM_CONV_DIM)), const_spec((1, SSM_CONV_DIM)),
                const_spec((1, DT_PAD)), const_spec((1, DT_PAD)), *expander_specs,
                h0_spec(0),
            ],
            out_specs=[chunk_spec(SSM_INNER), chunk_spec(2 * SSM_BC), chunk_spec(SSM_INNER), hout_spec],
            scratch_shapes=state_scratch,
        ),
        out_shape=[
            jax.ShapeDtypeStruct((N_TOK, SSM_INNER), f32),
            jax.ShapeDtypeStruct((N_TOK, 2 * SSM_BC), jnp.bfloat16),
            jax.ShapeDtypeStruct((N_TOK, SSM_INNER), f32),
            hout_shape,
        ],
        compiler_params=_cparams(("arbitrary",)),
        name="ssd_forward",
    )(jnp.asarray(_ssd_table(False)), xbc_raw, xbc_raw, xbc_raw, dt_raw, cw, cbias, dtb, alog,
      *expanders(False), state_in)

    y, h_pair = pl.pallas_call(
        _ssd_bwd_kernel,
        grid_spec=pltpu.PrefetchScalarGridSpec(
            num_scalar_prefetch=1,
            grid=(_SSD_STEPS,),
            in_specs=[
                chunk_spec(SSM_INNER), chunk_spec(2 * SSM_BC), chunk_spec(DT_PAD), chunk_spec(SSM_INNER),
                chunk_spec(SSM_INNER),
                const_spec((1, DT_PAD)), const_spec((1, DT_PAD)),
                const_spec((1, SSM_INNER)), const_spec((1, SSM_INNER)), *expander_specs,
                h0_spec(1), hout_spec,
            ],
            out_specs=[chunk_spec(SSM_INNER),
                       pl.BlockSpec((None, 2, SSM_INNER, D_STATE), lambda s, tbl: (_fld(tbl, _F_HOUT, s), 0, 0, 0))],
            scratch_shapes=state_scratch,
        ),
        out_shape=[jax.ShapeDtypeStruct((N_TOK, SSM_INNER), jnp.bfloat16),
                   jax.ShapeDtypeStruct((BATCH, 2, SSM_INNER, D_STATE), f32)],
        compiler_params=_cparams(("arbitrary",)),
        name="ssd_backward",
    )(jnp.asarray(_ssd_table(True)), xs, bc, dt_raw, yf, z, dtb, alog, dskip, nw, *expanders(True), state_in,
      h_f)
    return y, h_pair


def _sink_attention(q_heads, keys, vals, sink_ref, kv, masks):
    outs = []
    for g, qh in enumerate(q_heads):
        sk = sink_ref[kv * ATT_GROUP + g]
        scores = []
        for kk, mask in zip(keys, masks):
            s = lax.dot_general(qh, kk, _NT, preferred_element_type=jnp.float32) * ATT_SCALE
            scores.append(s if mask is None else jnp.where(mask, s, NEG_BIG))
        m = sk
        for s in scores:
            m = jnp.maximum(m, jnp.max(s, axis=-1, keepdims=True))
        den = jnp.exp(sk - m)
        acc = None
        for s, vv in zip(scores, vals):
            p = jnp.exp(s - m)
            den = den + jnp.sum(p, axis=-1, keepdims=True)
            pv = jnp.dot(p.astype(jnp.bfloat16), vv, preferred_element_type=jnp.float32)
            acc = pv if acc is None else acc + pv
        outs.append(acc / den)
    return outs


def _head(x, h):
    return x[:, h * ATT_HEAD_DIM:(h + 1) * ATT_HEAD_DIM]


def _attn_ctx_kernel(sink_ref, q_ref, k_ref, v_ref, o_ref):
    bf16 = jnp.bfloat16
    q = q_ref[...].astype(bf16)
    k = k_ref[...].astype(bf16)
    v = v_ref[...].astype(bf16)
    outs = []
    for kv in range(ATT_KV_HEADS):
        qs = [_head(q, kv * ATT_GROUP + g) for g in range(ATT_GROUP)]
        outs += _sink_attention(qs, [_head(k, kv)], [_head(v, kv)], sink_ref, kv, [None])
    o_ref[...] = jnp.concatenate(outs, axis=1).astype(bf16)


def _attend_context(q, k, v, sink):
    return pl.pallas_call(
        _attn_ctx_kernel,
        grid=(BATCH,),
        in_specs=[
            pl.BlockSpec(memory_space=pltpu.SMEM),
            pl.BlockSpec((SEQ, D_MODEL), lambda b: (b, 0)),
            pl.BlockSpec((SEQ, KV_DIM), lambda b: (b, 0)),
            pl.BlockSpec((SEQ, KV_DIM), lambda b: (b, 0)),
        ],
        out_specs=pl.BlockSpec((SEQ, D_MODEL), lambda b: (b, 0)),
        out_shape=jax.ShapeDtypeStruct((N_PROMPT, D_MODEL), jnp.bfloat16),
        compiler_params=_cparams(("arbitrary",)),
        name="attend_context",
    )(sink, q, k, v)


def _rope_tables(width):
    t = np.arange(DEC_SEQ)
    d = np.arange(width) % ATT_HEAD_DIM
    pos = np.where(d[None, :] < ROPE_HALF, (t // GRID_W)[:, None], (t % GRID_W)[:, None]).astype(np.float32)
    inv = (ROPE_BASE ** (-np.arange(ROPE_QUARTER, dtype=np.float32) / ROPE_QUARTER)).astype(np.float32)
    ang = pos * inv[d % ROPE_QUARTER][None, :]
    sign = np.where((d % ROPE_HALF) < ROPE_QUARTER, -1.0, 1.0).astype(np.float32)
    return jnp.asarray(np.cos(ang), jnp.float32), jnp.asarray(np.sin(ang) * sign[None, :], jnp.float32)


def _rope(x, cos, sin_signed):
    width = x.shape[1]
    lane = lax.broadcasted_iota(jnp.int32, (1, width), 1)
    first = (lane % ROPE_HALF) < ROPE_QUARTER
    partner = jnp.where(first, pltpu.roll(x, width - ROPE_QUARTER, 1), pltpu.roll(x, ROPE_QUARTER, 1))
    return x * cos + partner * sin_signed


def _rope_kernel(q_ref, k_ref, v_ref, cos_ref, sin_ref, qo_ref, ko_ref, vo_ref):
    bf16 = jnp.bfloat16
    cos, sin = cos_ref[...], sin_ref[...]
    qo_ref[...] = _rope(q_ref[...], cos, sin).astype(bf16)
    ko_ref[...] = _rope(k_ref[...], cos[:, :KV_DIM], sin[:, :KV_DIM]).astype(bf16)
    vo_ref[...] = v_ref[...].astype(bf16)


def _rope_latent(q, k, v):
    nb = DEC_SEQ // ATT_BLOCK
    off = N_PROMPT // ATT_BLOCK
    cos, sin = _rope_tables(D_MODEL)
    tok = lambda b, i: (off + b * nb + i, 0)
    out = lambda b, i: (b * nb + i, 0)
    return pl.pallas_call(
        _rope_kernel,
        grid=(DEC_BATCH, nb),
        in_specs=[
            pl.BlockSpec((ATT_BLOCK, D_MODEL), tok),
            pl.BlockSpec((ATT_BLOCK, KV_DIM), tok),
            pl.BlockSpec((ATT_BLOCK, KV_DIM), tok),
            pl.BlockSpec((ATT_BLOCK, D_MODEL), lambda b, i: (i, 0)),
            pl.BlockSpec((ATT_BLOCK, D_MODEL), lambda b, i: (i, 0)),
        ],
        out_specs=[
            pl.BlockSpec((ATT_BLOCK, D_MODEL), out),
            pl.BlockSpec((ATT_BLOCK, KV_DIM), out),
            pl.BlockSpec((ATT_BLOCK, KV_DIM), out),
        ],
        out_shape=[
            jax.ShapeDtypeStruct((N_SAMPLE, D_MODEL), jnp.bfloat16),
            jax.ShapeDtypeStruct((N_SAMPLE, KV_DIM), jnp.bfloat16),
            jax.ShapeDtypeStruct((N_SAMPLE, KV_DIM), jnp.bfloat16),
        ],
        compiler_params=_cparams(("arbitrary", "arbitrary")),
        name="rope_latent",
    )(q, k, v, cos, sin)


def _attn_lat_kernel(sink_ref, q_ref, kp_ref, kc_ref, kn_ref, vp_ref, vc_ref, vn_ref, kx_ref, vx_ref, o_ref):
    i = pl.program_id(1)
    nb = pl.num_programs(1)
    q = q_ref[...]
    k_loc = jnp.concatenate([kp_ref[...], kc_ref[...], kn_ref[...]], axis=0)
    v_loc = jnp.concatenate([vp_ref[...], vc_ref[...], vn_ref[...]], axis=0)
    kx, vx = kx_ref[...], vx_ref[...]
    r = lax.broadcasted_iota(jnp.int32, (ATT_BLOCK, 3 * ATT_BLOCK), 0)
    c = lax.broadcasted_iota(jnp.int32, (ATT_BLOCK, 3 * ATT_BLOCK), 1)
    rel = c - ATT_BLOCK - r
    in_window = jnp.logical_and(rel >= -WINDOW, rel <= WINDOW)
    in_seq = jnp.logical_and(jnp.logical_or(c >= ATT_BLOCK, i > 0),
                             jnp.logical_or(c < 2 * ATT_BLOCK, i < nb - 1))
    valid = jnp.logical_and(in_window, in_seq)
    outs = []
    for kv in range(ATT_KV_HEADS):
        qs = [_head(q, kv * ATT_GROUP + g) for g in range(ATT_GROUP)]
        outs += _sink_attention(qs, [_head(k_loc, kv), _head(kx, kv)], [_head(v_loc, kv), _head(vx, kv)],
                                sink_ref, kv, [valid, None])
    o_ref[...] = jnp.concatenate(outs, axis=1).astype(jnp.bfloat16)


def _attend_latent(qr, kr, vb, k_ctx, v_ctx, sink):
    nb = DEC_SEQ // ATT_BLOCK
    cur = lambda b, i: (b * nb + i, 0)
    prv = lambda b, i: (b * nb + jnp.maximum(i - 1, 0), 0)
    nxt = lambda b, i: (b * nb + jnp.minimum(i + 1, nb - 1), 0)
    kvs = lambda f: pl.BlockSpec((ATT_BLOCK, KV_DIM), f)
    ctx = pl.BlockSpec((None, PAST_LEN, KV_DIM), lambda b, i: (b, 0, 0))
    return pl.pallas_call(
        _attn_lat_kernel,
        grid=(DEC_BATCH, nb),
        in_specs=[
            pl.BlockSpec(memory_space=pltpu.SMEM),
            pl.BlockSpec((ATT_BLOCK, D_MODEL), cur),
            kvs(prv), kvs(cur), kvs(nxt), kvs(prv), kvs(cur), kvs(nxt), ctx, ctx,
        ],
        out_specs=pl.BlockSpec((ATT_BLOCK, D_MODEL), cur),
        out_shape=jax.ShapeDtypeStruct((N_SAMPLE, D_MODEL), jnp.bfloat16),
        compiler_params=_cparams(("arbitrary", "arbitrary")),
        name="attend_latent",
    )(sink, qr, kr, kr, kr, vb, vb, vb, k_ctx, v_ctx)


def _gconv_kernel(bg_ref, g_ref, gp_ref, gn_ref, w_ref, o_ref):
    i = pl.program_id(0)
    first = N_PROMPT // CONV_TILE
    per = DEC_SEQ // CONV_TILE
    t = (i - first) % per
    latent = i >= first
    hp = jnp.logical_and(latent, t > 0).astype(jnp.float32)
    hn = jnp.logical_and(latent, t < per - 1).astype(jnp.float32)
    window = jnp.concatenate([gp_ref[...] * hp, g_ref[...], gn_ref[...] * hn], axis=0)
    acc = None
    for k in range(SHORT_CONV):
        off = SUBLANES - SHORT_CONV // 2 + k
        term = w_ref[k:k + 1, :] * window[off:off + CONV_TILE, :]
        acc = term if acc is None else acc + term
    o_ref[...] = (bg_ref[...] * acc).astype(jnp.bfloat16)


def _gated_conv(bg, g, conv_w):
    per = CONV_TILE // SUBLANES
    last = N_TOK // SUBLANES - 1
    cw = jnp.pad(conv_w, ((0, SUBLANES - SHORT_CONV), (0, 0)))
    return pl.pallas_call(
        _gconv_kernel,
        grid=(N_TOK // CONV_TILE,),
        in_specs=[
            pl.BlockSpec((CONV_TILE, D_MODEL), lambda i: (i, 0)),
            pl.BlockSpec((CONV_TILE, D_MODEL), lambda i: (i, 0)),
            pl.BlockSpec((SUBLANES, D_MODEL), lambda i: (jnp.maximum(i * per - 1, 0), 0)),
            pl.BlockSpec((SUBLANES, D_MODEL), lambda i: (jnp.minimum(i * per + per, last), 0)),
            pl.BlockSpec((SUBLANES, D_MODEL), lambda i: (0, 0)),
        ],
        out_specs=pl.BlockSpec((CONV_TILE, D_MODEL), lambda i: (i, 0)),
        out_shape=jax.ShapeDtypeStruct((N_TOK, D_MODEL), jnp.bfloat16),
        compiler_params=_cparams(("arbitrary",)),
        name="gated_conv",
    )(bg, g, g, g, cw)


def kernel(x_prompt, x_sample, cache_k, cache_v, state_ssm, c, c_ctx,
           w_mod, b_mod, ln_w, ln_b,
           w_in_a, conv_w_a, conv_b_a, a_log, dt_bias, d_skip, ssm_norm_w, attn_sink, w_out_a,
           w_in_c, conv_w_c, w_out_c,
           w_router, b_router, w_gate, b_gate, w_lin, b_lin, w_down, b_down):
    bf16 = jnp.bfloat16
    x = (x_prompt.reshape(N_PROMPT, D_MODEL), x_sample.reshape(N_SAMPLE, D_MODEL))

    cond = jnp.concatenate([c_ctx[None, :], c, jnp.zeros((COND_PAD - N_COND, D_MODEL), jnp.float32)], axis=0)
    mod_all = _modulation(cond, w_mod, b_mod).reshape(DEPTH, COND_PAD, 6, D_MODEL)
    mod_all = jnp.pad(mod_all, ((0, 0), (0, 0), (0, MOD_ROWS - 6), (0, 0)))

    mod = mod_all[0]
    wa = w_in_a[0].astype(bf16)
    e = np.cumsum((0, SSM_INNER, SSM_CONV_DIM, 2 * SSM_HEADS, D_MODEL, KV_DIM, KV_DIM))
    w_dt = jnp.pad(wa[:, e[2]:e[3]], ((0, 0), (0, DT_PAD - 2 * SSM_HEADS)))
    w_parts = [wa[:, e[0]:e[1]], wa[:, e[1]:e[2]], w_dt, wa[:, e[3]:e[4]], wa[:, e[4]:e[5]], wa[:, e[5]:e[6]]]
    z, xbc_raw, dt_raw, q, k, v = _inproj(x, mod, w_parts)

    state_in = state_ssm[:, 0].reshape(DEC_BATCH, 2, SSM_INNER, D_STATE)
    y_ssm, h_pair = _ssd_mixer(z, xbc_raw, dt_raw, conv_w_a[0], conv_b_a[0], a_log[0], dt_bias[0],
                                 d_skip[0], ssm_norm_w[0], state_in)

    sink = attn_sink[0]
    att_p = _attend_context(q, k, v, sink)
    qr, kr, vb = _rope_latent(q, k, v)
    k_ctx = cache_k[:, 0].reshape(DEC_BATCH, PAST_LEN, KV_DIM).astype(bf16)
    v_ctx = cache_v[:, 0].reshape(DEC_BATCH, PAST_LEN, KV_DIM).astype(bf16)
    att_s = _attend_latent(qr, kr, vb, k_ctx, v_ctx, sink)
    y_att = (att_p, att_s)

    wo = w_out_a[0].astype(bf16)
    ln = jnp.stack([ln_w[0, 0], ln_b[0, 0]])
    x1, u2, *routing = _outproj([y_ssm, y_att], [wo[:SSM_INNER], wo[SSM_INNER:]], x, mod, ln,
                                w_router[0], b_router[0])
    x = _moe_layer(0, x1, u2, routing, mod, jnp.stack([ln_w[0, 1], ln_b[0, 1]]),
                   w_gate, b_gate, w_lin, b_lin, w_down, b_down)

    mod = mod_all[1]
    wc = w_in_c[0].astype(bf16)
    bg, g = _inproj(x, mod, [wc[:, :D_MODEL], wc[:, D_MODEL:2 * D_MODEL], wc[:, 2 * D_MODEL:]], gate_product=True)
    y_c = _gated_conv(bg, g, conv_w_c[0])
    ln = jnp.stack([ln_w[1, 0], ln_b[1, 0]])
    x1, u2, *routing = _outproj([y_c], [w_out_c[0].astype(bf16)], x, mod, ln, w_router[1], b_router[1])
    y_p, y_s = _moe_layer(1, x1, u2, routing, mod, jnp.stack([ln_w[1, 1], ln_b[1, 1]]),
                          w_gate, b_gate, w_lin, b_lin, w_down, b_down, split=True)

    y_prompt = y_p.reshape(BATCH, SEQ, D_MODEL)
    y_sample = y_s.reshape(DEC_BATCH, DEC_SEQ, D_MODEL)
    new_k = k[:N_PROMPT].reshape(BATCH, 1, SEQ, ATT_KV_HEADS, ATT_HEAD_DIM)
    new_v = v[:N_PROMPT].reshape(BATCH, 1, SEQ, ATT_KV_HEADS, ATT_HEAD_DIM)
    new_state = h_pair.reshape(BATCH, 1, 2, SSM_HEADS, SSM_HEAD_DIM, D_STATE)
    return (y_prompt, y_sample, new_k, new_v, new_state)
```

```python
import functools

import numpy as np
import jax
import jax.numpy as jnp
from jax import lax
from jax.experimental import pallas as pl
from jax.experimental.pallas import tpu as pltpu

D_MODEL = 1024
BATCH = 32
SEQ = 256
DEPTH = 2
DEC_BATCH = 4
DEC_SEQ = 1024
PAST_LEN = 512
GRID_W = 64
SSM_HEAD_DIM = 64
SSM_INNER = D_MODEL
SSM_HEADS = SSM_INNER // SSM_HEAD_DIM
SSM_GROUPS = 2
SSM_HEADS_PER_GROUP = SSM_HEADS // SSM_GROUPS
D_STATE = 128
SSM_CONV = 5
SSM_CHUNK = 128
SSM_BC = SSM_GROUPS * D_STATE
SSM_CONV_DIM = SSM_INNER + 2 * SSM_BC
ATT_HEAD_DIM = 64
ATT_HEADS = D_MODEL // ATT_HEAD_DIM
ATT_KV_HEADS = 4
ATT_GROUP = ATT_HEADS // ATT_KV_HEADS
WINDOW = 128
ATT_BLOCK = 128
ATT_SCALE = ATT_HEAD_DIM ** -0.5
ROPE_BASE = 10000.0
ROPE_HALF = ATT_HEAD_DIM // 2
ROPE_QUARTER = ATT_HEAD_DIM // 4
SHORT_CONV = 3
N_EXPERTS = 32
TOP_K = 4
D_EXPERT = D_MODEL
SWIGLU_ALPHA = 1.702
SWIGLU_LIMIT = 7.0
N_EVEN = (DEPTH + 1) // 2
DEEPNORM_ALPHA = (2 * DEPTH) ** 0.25
LN_EPS = 1e-5
RMS_EPS = 1e-5
KV_DIM = ATT_KV_HEADS * ATT_HEAD_DIM

N_PROMPT = BATCH * SEQ
N_SAMPLE = DEC_BATCH * DEC_SEQ
N_TOK = N_PROMPT + N_SAMPLE
N_COND = 1 + DEC_BATCH
SUBLANES = 8
LANES = 128
COND_PAD = SUBLANES
MOD_ROWS = SUBLANES
DT_PAD = LANES

TOK_TILE = 512
CONV_TILE = 256
MOE_TILE = 256
MOE_ROWS = N_TOK * TOP_K + N_EXPERTS * MOE_TILE
MOE_TILES = MOE_ROWS // MOE_TILE
VMEM_LIMIT = 56 * 1024 * 1024
NEG_BIG = -1e30
WEIGHT_DMA_PRIORITY = 1
ROW_TILES = D_MODEL // LANES

assert N_PROMPT % TOK_TILE == 0 and DEC_SEQ % TOK_TILE == 0
PROMPT_TILES = N_PROMPT // TOK_TILE
assert SEQ % CONV_TILE == 0 and DEC_SEQ % CONV_TILE == 0

_NT = (((1,), (1,)), ((), ()))
_TN = (((0,), (0,)), ((), ()))


def _cparams(sem):
    return pltpu.CompilerParams(dimension_semantics=sem, vmem_limit_bytes=VMEM_LIMIT)


def _cond_row(i):
    first = N_PROMPT // TOK_TILE
    per = DEC_SEQ // TOK_TILE
    return jnp.where(i < first, 0, 1 + (i - first) // per)


def _bf16_dot(a, b):
    return jnp.dot(a.astype(jnp.bfloat16), b.astype(jnp.bfloat16), preferred_element_type=jnp.float32)


def _mod_kernel(c_ref, w_ref, b_ref, o_ref):
    c = c_ref[...]
    s = c * jax.nn.sigmoid(c)
    o_ref[...] = _bf16_dot(s, w_ref[...]) + b_ref[...]


def _modulation(cond, w_mod, b_mod):
    tn = 1536
    return pl.pallas_call(
        _mod_kernel,
        grid=(DEPTH, 6 * D_MODEL // tn),
        in_specs=[
            pl.BlockSpec((COND_PAD, D_MODEL), lambda l, n: (0, 0)),
            pl.BlockSpec((None, D_MODEL, tn), lambda l, n: (l, 0, n)),
            pl.BlockSpec((None, 1, tn), lambda l, n: (l, 0, n)),
        ],
        out_specs=pl.BlockSpec((None, COND_PAD, tn), lambda l, n: (l, 0, n)),
        out_shape=jax.ShapeDtypeStruct((DEPTH, COND_PAD, 6 * D_MODEL), jnp.float32),
        compiler_params=_cparams(("arbitrary", "arbitrary")),
        name="modulation",
    )(cond, w_mod, b_mod.reshape(DEPTH, 1, 6 * D_MODEL))


def _stream_specs(x):
    if not isinstance(x, tuple):
        return [pl.BlockSpec((TOK_TILE, D_MODEL), lambda i: (i, 0))]
    return [pl.BlockSpec((TOK_TILE, D_MODEL), lambda i: (jnp.minimum(i, PROMPT_TILES - 1), 0)),
            pl.BlockSpec((TOK_TILE, D_MODEL), lambda i: (jnp.maximum(i - PROMPT_TILES, 0), 0))]


def _stream_tile(x_refs):
    if len(x_refs) == 1:
        return x_refs[0][...]
    return jnp.where(pl.program_id(0) < PROMPT_TILES, x_refs[0][...], x_refs[1][...])


def _inproj_kernel(*refs, n_x, gate_product):
    x_refs, mod_ref, refs = refs[:n_x], refs[n_x], refs[n_x + 1:]
    n = len(refs) // 2 + (1 if gate_product else 0)
    w_refs, o_refs = refs[:n], refs[n:]
    u = _stream_tile(x_refs) * (1.0 + mod_ref[1:2, :]) + mod_ref[0:1, :]
    ub = u.astype(jnp.bfloat16)
    outs = [jnp.dot(ub, w_ref[...], preferred_element_type=jnp.float32) for w_ref in w_refs]
    if gate_product:
        outs = outs[:-2] + [outs[-2] * outs[-1]]
    for o_ref, o in zip(o_refs, outs):
        o_ref[...] = o


def _inproj(x, mod, weights, gate_product=False):
    out_w = [w.shape[1] for w in weights]
    if gate_product:
        out_w = out_w[:-1]
    xs = x if isinstance(x, tuple) else (x,)
    return pl.pallas_call(
        functools.partial(_inproj_kernel, n_x=len(xs), gate_product=gate_product),
        grid=(N_TOK // TOK_TILE,),
        in_specs=_stream_specs(x)
        + [pl.BlockSpec((None, MOD_ROWS, D_MODEL), lambda i: (_cond_row(i), 0, 0))]
        + [pl.BlockSpec(w.shape, lambda i: (0, 0)) for w in weights],
        out_specs=[pl.BlockSpec((TOK_TILE, n), lambda i: (i, 0)) for n in out_w],
        out_shape=[jax.ShapeDtypeStruct((N_TOK, n), jnp.float32) for n in out_w],
        compiler_params=_cparams(("arbitrary",)),
        name="inproj",
    )(*xs, mod, *weights)


def _layer_norm(v, w, b):
    mu = jnp.mean(v, axis=-1, keepdims=True)
    d = v - mu
    var = jnp.mean(d * d, axis=-1, keepdims=True)
    return d * lax.rsqrt(var + LN_EPS) * w + b


def _outproj_kernel(*refs, y_counts, n_x):
    n_in, n_y = len(y_counts), sum(y_counts)
    y_refs = refs[:n_y]
    w_refs = refs[n_y:n_y + n_in]
    x_refs = refs[n_y + n_in:n_y + n_in + n_x]
    (mod_ref, ln_ref, wr_ref, br_ref,
     x1_ref, u2_ref, idx_ref, gate_ref, rank_ref, cnt_ref, run_ref) = refs[n_y + n_in + n_x:]
    f32, bf16 = jnp.float32, jnp.bfloat16

    @pl.when(pl.program_id(0) == 0)
    def _():
        run_ref[...] = jnp.zeros_like(run_ref)

    mix, first = None, 0
    for count, w_ref in zip(y_counts, w_refs):
        part = _bf16_dot(_stream_tile(y_refs[first:first + count]), w_ref[...])
        mix = part if mix is None else mix + part
        first += count
    x1 = _layer_norm(DEEPNORM_ALPHA * _stream_tile(x_refs) + mod_ref[2:3, :] * mix, ln_ref[0:1, :], ln_ref[1:2, :])
    x1_ref[...] = x1
    u2 = x1 * (1.0 + mod_ref[4:5, :]) + mod_ref[3:4, :]
    u2_ref[...] = u2
    wr = wr_ref[...]
    u_hi, w_hi = u2.astype(bf16), wr.astype(bf16)
    u_lo, w_lo = (u2 - u_hi.astype(f32)).astype(bf16), (wr - w_hi.astype(f32)).astype(bf16)
    logits = (jnp.dot(u_hi, w_hi, preferred_element_type=f32)
              + (jnp.dot(u_lo, w_hi, preferred_element_type=f32) + jnp.dot(u_hi, w_lo, preferred_element_type=f32))
              + br_ref[...])

    lane = lax.broadcasted_iota(jnp.int32, (TOK_TILE, N_EXPERTS), 1)
    work = logits
    vals, sels, idxs = [], [], []
    for _ in range(TOP_K):
        m = jnp.max(work, axis=-1, keepdims=True)
        idx = jnp.min(jnp.where(work == m, lane, N_EXPERTS), axis=-1, keepdims=True)
        sel = lane == idx
        vals.append(m)
        idxs.append(idx)
        sels.append(sel)
        work = jnp.where(sel, -jnp.inf, work)
    exps = [jnp.exp(v - vals[0]) for v in vals]
    den = exps[0]
    for e in exps[1:]:
        den = den + e
    onehot = sels[0].astype(f32)
    for sel in sels[1:]:
        onehot = onehot + sel.astype(f32)
    r = lax.broadcasted_iota(jnp.int32, (TOK_TILE, TOK_TILE), 0)
    c = lax.broadcasted_iota(jnp.int32, (TOK_TILE, TOK_TILE), 1)
    before = jnp.dot((r > c).astype(jnp.bfloat16), onehot.astype(jnp.bfloat16), preferred_element_type=f32)
    rank_all = before + run_ref[...]
    run = run_ref[...] + jnp.sum(onehot, axis=0, keepdims=True)
    run_ref[...] = run
    cnt_ref[...] = run.astype(jnp.int32)
    col = lax.broadcasted_iota(jnp.int32, (TOK_TILE, TOP_K), 1)
    idx_o = jnp.zeros((TOK_TILE, TOP_K), jnp.int32)
    gate_o = jnp.zeros((TOK_TILE, TOP_K), f32)
    rank_o = jnp.zeros((TOK_TILE, TOP_K), f32)
    for k in range(TOP_K):
        rank_k = jnp.sum(jnp.where(sels[k], rank_all, 0.0), axis=-1, keepdims=True)
        idx_o = jnp.where(col == k, idxs[k], idx_o)
        gate_o = jnp.where(col == k, exps[k] / den, gate_o)
        rank_o = jnp.where(col == k, rank_k, rank_o)
    idx_ref[...] = idx_o
    gate_ref[...] = gate_o
    rank_ref[...] = rank_o.astype(jnp.int32)


def _outproj(ys, ws, x, mod, ln, w_router, b_router):
    xs = x if isinstance(x, tuple) else (x,)
    y_parts = [y if isinstance(y, tuple) else (y,) for y in ys]
    y_flat = [part for parts in y_parts for part in parts]
    return pl.pallas_call(
        functools.partial(_outproj_kernel, y_counts=tuple(len(parts) for parts in y_parts), n_x=len(xs)),
        grid=(N_TOK // TOK_TILE,),
        in_specs=[spec for y in ys for spec in _stream_specs(y)]
        + [pl.BlockSpec(w.shape, lambda i: (0, 0)) for w in ws]
        + _stream_specs(x)
        + [
            pl.BlockSpec((None, MOD_ROWS, D_MODEL), lambda i: (_cond_row(i), 0, 0)),
            pl.BlockSpec((2, D_MODEL), lambda i: (0, 0)),
            pl.BlockSpec((D_MODEL, N_EXPERTS), lambda i: (0, 0)),
            pl.BlockSpec((1, N_EXPERTS), lambda i: (0, 0)),
        ],
        out_specs=[
            pl.BlockSpec((TOK_TILE, D_MODEL), lambda i: (i, 0)),
            pl.BlockSpec((TOK_TILE, D_MODEL), lambda i: (i, 0)),
            pl.BlockSpec((TOK_TILE, TOP_K), lambda i: (i, 0)),
            pl.BlockSpec((TOK_TILE, TOP_K), lambda i: (i, 0)),
            pl.BlockSpec((TOK_TILE, TOP_K), lambda i: (i, 0)),
            pl.BlockSpec((1, N_EXPERTS), lambda i: (0, 0)),
        ],
        out_shape=[
            jax.ShapeDtypeStruct((N_TOK, D_MODEL), jnp.float32),
            jax.ShapeDtypeStruct((N_TOK, D_MODEL), jnp.float32),
            jax.ShapeDtypeStruct((N_TOK, TOP_K), jnp.int32),
            jax.ShapeDtypeStruct((N_TOK, TOP_K), jnp.float32),
            jax.ShapeDtypeStruct((N_TOK, TOP_K), jnp.int32),
            jax.ShapeDtypeStruct((1, N_EXPERTS), jnp.int32),
        ],
        scratch_shapes=[pltpu.VMEM((1, N_EXPERTS), jnp.float32)],
        compiler_params=_cparams(("arbitrary",)),
        name="outproj_ln_router",
    )(*y_flat, *ws, *xs, mod, ln, w_router, b_router.reshape(1, N_EXPERTS))


def _moe_kernel(te_ref, nv_ref, src0_ref, last_ref, wslot_ref, nexte_ref, stok_ref, x_hbm,
                wg_hbm, bg_ref, wl_hbm, bl_ref, wd_hbm, bd_ref, o_ref,
                w_bf, xbuf, sem, wst, wsem, *, layer):
    i = pl.program_id(0)
    nv = nv_ref[0]
    valid = i < nv
    prev = te_ref[jnp.maximum(i - 1, 0)]
    new_expert = jnp.logical_or(i == 0, te_ref[i] != prev)

    def start_gather(tile, buf):
        base, last = src0_ref[tile], last_ref[tile]
        for r in range(MOE_TILE):
            tok = stok_ref[jnp.minimum(base + r, last)]
            pltpu.make_async_copy(x_hbm.at[tok], xbuf.at[buf, r], sem.at[buf]).start()

    def wait_gather(buf):
        pltpu.make_async_copy(x_hbm.at[pl.ds(0, MOE_TILE)], xbuf.at[buf], sem.at[buf]).wait()

    def weight_copies(expert, slot):
        return [pltpu.make_async_copy(w.at[layer, expert], wst.at[slot, j], wsem.at[slot])
                for j, w in enumerate((wg_hbm, wl_hbm, wd_hbm))]

    @pl.when(i == 0)
    def _():
        start_gather(0, 0)
        for cp in weight_copies(te_ref[0], 0):
            cp.start(priority=WEIGHT_DMA_PRIORITY)

    @pl.when(jnp.logical_and(valid, new_expert))
    def _():
        slot = wslot_ref[i]
        nxt = nexte_ref[i]

        @pl.when(nxt >= 0)
        def _():
            for cp in weight_copies(nxt, 1 - slot):
                cp.start(priority=WEIGHT_DMA_PRIORITY)

        for cp in weight_copies(te_ref[i], slot):
            cp.wait()
        for j in range(3):
            w_bf[j] = wst[slot, j].astype(jnp.bfloat16)

    for buf in range(2):
        @pl.when(jnp.logical_and(valid, i % 2 == buf))
        def _():
            wait_gather(buf)
            start_gather(jnp.minimum(i + 1, nv - 1), 1 - buf)
            x = jnp.concatenate([xbuf[buf, :, s, :] for s in range(ROW_TILES)], axis=1).astype(jnp.bfloat16)
            hg = jnp.dot(x, w_bf[0], preferred_element_type=jnp.float32) + bg_ref[...]
            hl = jnp.dot(x, w_bf[1], preferred_element_type=jnp.float32) + bl_ref[...]
            hg = jnp.minimum(hg, SWIGLU_LIMIT)
            hl = jnp.clip(hl, -SWIGLU_LIMIT, SWIGLU_LIMIT)
            h = hg * jax.nn.sigmoid(SWIGLU_ALPHA * hg) * (hl + 1.0)
            o_ref[...] = (jnp.dot(h.astype(jnp.bfloat16), w_bf[2], preferred_element_type=jnp.float32)
                          + bd_ref[...])

        @pl.when(jnp.logical_and(i == nv - 1, i % 2 == buf))
        def _():
            wait_gather(1 - buf)

    @pl.when(jnp.logical_not(valid))
    def _():
        o_ref[...] = jnp.zeros_like(o_ref)


def _moe_grouped(layer, tables, sorted_tok, x, w_gate, b_gate, w_lin, b_lin, w_down, b_down):
    hbm = pl.BlockSpec(memory_space=pl.ANY)
    bspec = pl.BlockSpec((None, None, 1, D_EXPERT), lambda i, te, *_: (layer, te[i], 0, 0))
    grid_spec = pltpu.PrefetchScalarGridSpec(
        num_scalar_prefetch=len(tables) + 1,
        grid=(MOE_TILES,),
        in_specs=[hbm, hbm, bspec, hbm, bspec, hbm, bspec],
        out_specs=pl.BlockSpec((MOE_TILE, D_MODEL), lambda i, *_: (i, 0)),
        scratch_shapes=[
            pltpu.VMEM((3, D_MODEL, D_EXPERT), jnp.bfloat16),
            pltpu.VMEM((2, MOE_TILE, ROW_TILES, LANES), jnp.float32), pltpu.SemaphoreType.DMA((2,)),
            pltpu.VMEM((2, 3, D_MODEL, D_EXPERT), jnp.float32), pltpu.SemaphoreType.DMA((2,)),
        ],
    )
    bshape = (DEPTH, N_EXPERTS, 1, D_EXPERT)
    return pl.pallas_call(
        functools.partial(_moe_kernel, layer=layer),
        grid_spec=grid_spec,
        out_shape=jax.ShapeDtypeStruct((MOE_ROWS, D_MODEL), jnp.float32),
        compiler_params=_cparams(("arbitrary",)),
        name="moe_grouped",
    )(*tables, sorted_tok, x.reshape(N_TOK, ROW_TILES, LANES), w_gate, b_gate.reshape(bshape),
      w_lin, b_lin.reshape(bshape), w_down, b_down.reshape(bshape))


def _combine_kernel(y_ref, g_ref, x_ref, mod_ref, ln_ref, *o_refs):
    g = g_ref[...]
    ff = g[:, 0:1] * y_ref[0]
    for k in range(1, TOP_K):
        ff = ff + g[:, k:k + 1] * y_ref[k]
    out = _layer_norm(DEEPNORM_ALPHA * x_ref[...] + mod_ref[5:6, :] * ff, ln_ref[0:1, :], ln_ref[1:2, :])
    if len(o_refs) == 1:
        o_refs[0][...] = out
    else:
        is_prompt = pl.program_id(0) < PROMPT_TILES

        @pl.when(is_prompt)
        def _():
            o_refs[0][...] = out

        @pl.when(jnp.logical_not(is_prompt))
        def _():
            o_refs[1][...] = out


def _combine(yk, gates, x, mod, ln, split=False):
    if split:
        out_specs = [pl.BlockSpec((TOK_TILE, D_MODEL), lambda i: (jnp.minimum(i, PROMPT_TILES - 1), 0)),
                     pl.BlockSpec((TOK_TILE, D_MODEL), lambda i: (jnp.maximum(i - PROMPT_TILES, 0), 0))]
        out_shape = [jax.ShapeDtypeStruct((N_PROMPT, D_MODEL), jnp.float32),
                     jax.ShapeDtypeStruct((N_SAMPLE, D_MODEL), jnp.float32)]
    else:
        out_specs = pl.BlockSpec((TOK_TILE, D_MODEL), lambda i: (i, 0))
        out_shape = jax.ShapeDtypeStruct((N_TOK, D_MODEL), jnp.float32)
    return pl.pallas_call(
        _combine_kernel,
        grid=(N_TOK // TOK_TILE,),
        in_specs=[
            pl.BlockSpec((TOP_K, TOK_TILE, D_MODEL), lambda i: (0, i, 0)),
            pl.BlockSpec((TOK_TILE, TOP_K), lambda i: (i, 0)),
            pl.BlockSpec((TOK_TILE, D_MODEL), lambda i: (i, 0)),
            pl.BlockSpec((None, MOD_ROWS, D_MODEL), lambda i: (_cond_row(i), 0, 0)),
            pl.BlockSpec((2, D_MODEL), lambda i: (0, 0)),
        ],
        out_specs=out_specs,
        out_shape=out_shape,
        compiler_params=_cparams(("arbitrary",)),
        name="combine_ln",
    )(yk, gates, x, mod, ln)


def _route(top_idx, rank, counts):
    counts = counts.reshape(N_EXPERTS)
    padded = (counts + MOE_TILE - 1) // MOE_TILE * MOE_TILE
    pend = jnp.cumsum(padded)
    pstart = pend - padded
    start = jnp.cumsum(counts) - counts
    experts = jnp.arange(N_EXPERTS, dtype=jnp.int32)
    lookup = lambda table, idx: jnp.sum(jnp.where(idx[..., None] == experts, table, 0), axis=-1)
    slot = lookup(pstart, top_idx) + rank
    key = lookup(start, top_idx) + rank
    tok = jnp.broadcast_to(jnp.arange(N_TOK, dtype=jnp.int32)[:, None], (N_TOK, TOP_K))
    _, sorted_tok = lax.sort((key.reshape(-1), tok.reshape(-1)), num_keys=1)
    n_valid = (pend[-1] // MOE_TILE).astype(jnp.int32)
    tile_start = jnp.arange(MOE_TILES, dtype=jnp.int32) * MOE_TILE
    tile_expert = jnp.sum((tile_start[:, None] >= pend[None, :]).astype(jnp.int32), axis=1)
    tile_expert = jnp.where(tile_start < pend[-1], tile_expert, tile_expert[n_valid - 1])
    src0 = tile_start - lookup(pstart, tile_expert) + lookup(start, tile_expert)
    src_last = lookup(start, tile_expert) + lookup(counts, tile_expert) - 1
    used = (counts > 0).astype(jnp.int32)
    order = jnp.cumsum(used) - used
    later = jnp.where(jnp.logical_and(experts[None, :] > experts[:, None], used[None, :] > 0),
                      experts[None, :], N_EXPERTS)
    successor = jnp.min(later, axis=1)
    successor = jnp.where(successor < N_EXPERTS, successor, -1)
    wslot = lookup(order, tile_expert) % 2
    next_expert = lookup(successor, tile_expert)
    tables = (tile_expert, n_valid.reshape(1), src0, src_last, wslot, next_expert)
    return slot, sorted_tok, tables


def _moe_layer(i, x1, u2, routing, mod, ln, w_gate, b_gate, w_lin, b_lin, w_down, b_down, split=False):
    top_idx, gates, rank, counts = routing
    slot, sorted_tok, tables = _route(top_idx, rank, counts)
    y = _moe_grouped(i, tables, sorted_tok, u2, w_gate, b_gate, w_lin, b_lin, w_down, b_down)
    yk = jnp.take(y, slot.T, axis=0, mode="clip")
    return _combine(yk, gates, x1, mod, ln, split)


_F_ROW, _F_FIRST, _F_LAST, _F_PREV8, _F_HASPREV, _F_NEXT8, _F_HASNEXT, _F_H0, _F_USEH0, _F_HOUT, _F_WRITEH = range(11)
_N_FIELDS = 11
_SSD_STEPS = N_TOK // SSM_CHUNK


def _ssd_table(reverse):
    rows = []
    for prompt, nseq, slen, base in ((True, BATCH, SEQ, 0), (False, DEC_BATCH, DEC_SEQ, N_PROMPT)):
        nc = slen // SSM_CHUNK
        for s in range(nseq):
            for c in range(nc):
                row = (base + s * slen) // SSM_CHUNK + c
                first, last = (c == nc - 1, c == 0) if reverse else (c == 0, c == nc - 1)
                per = SSM_CHUNK // SUBLANES
                rows.append([
                    row, int(first), int(last),
                    max(row * per - 1, 0), int(c > 0),
                    min(row * per + per, N_TOK // SUBLANES - 1), int(c < nc - 1),
                    0 if prompt else s, int(not prompt),
                    s if prompt else BATCH - 1, int(prompt and last),
                ])
    if reverse:
        rows = rows[::-1]
    return np.asarray(rows, np.int32).T.reshape(-1)


def _fld(tbl, f, s):
    return tbl[f * _SSD_STEPS + s]


def _exact_expand(a, e_ref, passes):
    k = a.shape[1]
    pieces, rem = [], a
    for _ in range(passes):
        piece = rem.astype(jnp.bfloat16)
        rem = rem - piece.astype(jnp.float32)
        pieces.append(piece)
    out = None
    for i in range(0, passes, 2):
        pair = pieces[i:i + 2]
        lhs = pair[0] if len(pair) == 1 else jnp.concatenate(pair, axis=1)
        t = jnp.dot(lhs, e_ref[0:k * len(pair), :], preferred_element_type=jnp.float32)
        out = t if out is None else out + t
    return out


def _ssd_chunk(xs, bmat, cmat, dt_raw, dtb_row, alog_row, ex_ref, eb_ref, s_ref, reverse):
    f32, bf16 = jnp.float32, jnp.bfloat16
    base = SSM_HEADS if reverse else 0
    lane = lax.broadcasted_iota(jnp.int32, (1, DT_PAD), 1)
    head_cols = jnp.logical_and(lane >= base, lane < base + SSM_HEADS)
    xr = dt_raw + dtb_row
    dt = jnp.maximum(xr, 0.0) + jnp.log1p(jnp.exp(-jnp.abs(xr)))
    a_row = jnp.where(head_cols, -jnp.exp(alog_row), 0.0)
    d_a = dt * a_row
    r = lax.broadcasted_iota(jnp.int32, (SSM_CHUNK, SSM_CHUNK), 0)
    c = lax.broadcasted_iota(jnp.int32, (SSM_CHUNK, SSM_CHUNK), 1)
    tri = (r <= c) if reverse else (r >= c)
    cum = jnp.dot(tri.astype(f32), d_a, preferred_element_type=f32, precision=lax.Precision.HIGHEST)
    cum_t = cum.T
    total = cum[0:1, :] if reverse else cum[SSM_CHUNK - 1:SSM_CHUNK, :]
    stacked = jnp.concatenate([dt, jnp.exp(cum), dt * jnp.exp(total - cum),
                               jnp.broadcast_to(jnp.exp(total), (SUBLANES, DT_PAD))], axis=0)
    stacked_x = _exact_expand(stacked, ex_ref, 2)
    dt_x = stacked_x[0:SSM_CHUNK]
    e_in_x = stacked_x[SSM_CHUNK:2 * SSM_CHUNK]
    w_end_x = stacked_x[2 * SSM_CHUNK:3 * SSM_CHUNK]
    dec_x = stacked_x[3 * SSM_CHUNK:3 * SSM_CHUNK + 1]
    cum_b = _exact_expand(cum, eb_ref, 3)
    xdt = xs * dt_x
    xw = (xs * w_end_x).astype(bf16)
    s_in = s_ref[...]
    s_bf = s_in.astype(bf16)
    lo_half = lax.broadcasted_iota(jnp.int32, (1, 2 * SSM_HEAD_DIM), 1) < SSM_HEAD_DIM
    gw = SSM_HEADS_PER_GROUP * SSM_HEAD_DIM
    pieces = []
    for g in range(SSM_GROUPS):
        bg = bmat[:, g * D_STATE:(g + 1) * D_STATE]
        cg = cmat[:, g * D_STATE:(g + 1) * D_STATE]
        gcols = slice(g * gw, (g + 1) * gw)
        cb = lax.dot_general(cg, bg, _NT, preferred_element_type=f32)
        y_off = jnp.dot(cg, s_bf[:, gcols], preferred_element_type=f32) * e_in_x[:, gcols]
        upd = lax.dot_general(bg, xw[:, gcols], _TN, preferred_element_type=f32)
        s_ref[:, gcols] = s_in[:, gcols] * dec_x[:, gcols] + upd
        for q in range(SSM_HEADS_PER_GROUP // 2):
            h0 = g * SSM_HEADS_PER_GROUP + 2 * q
            pcols = slice(h0 * SSM_HEAD_DIM, (h0 + 2) * SSM_HEAD_DIM)
            xp = xdt[:, pcols]
            bd = jnp.concatenate([jnp.where(lo_half, xp, 0.0), jnp.where(lo_half, 0.0, xp)], axis=0).astype(bf16)
            ms = []
            for h in (h0, h0 + 1):
                ch = base + h
                seg = cum_b[:, h * SSM_CHUNK:(h + 1) * SSM_CHUNK] - cum_t[ch:ch + 1, :]
                decay = jnp.exp(jnp.where(tri, seg, NEG_BIG))
                ms.append((cb * decay).astype(bf16))
            y_diag = jnp.dot(jnp.concatenate(ms, axis=1), bd, preferred_element_type=f32)
            pieces.append(y_diag + y_off[:, 2 * q * SSM_HEAD_DIM:(2 * q + 2) * SSM_HEAD_DIM])
    return jnp.concatenate(pieces, axis=1)


def _ssd_init_state(tbl, step, h0_ref, s_ref):
    @pl.when(_fld(tbl, _F_FIRST, step) == 1)
    def _():
        use = _fld(tbl, _F_USEH0, step) == 1
        s_ref[...] = jnp.where(use, h0_ref[...], 0.0).T


def _ssd_write_state(tbl, step, hout_ref, s_ref):
    @pl.when(_fld(tbl, _F_WRITEH, step) == 1)
    def _():
        hout_ref[...] = s_ref[...].T


def _ssd_fwd_kernel(tbl, xc_ref, xp_ref, xn_ref, dt_ref, cw_ref, cbias_ref, dtb_ref, alog_ref, ex_ref, eb_ref,
                    h0_ref, xs_ref, bc_ref, yf_ref, hout_ref, s_ref):
    step = pl.program_id(0)
    _ssd_init_state(tbl, step, h0_ref, s_ref)
    hp = (_fld(tbl, _F_HASPREV, step) == 1).astype(jnp.float32)
    hn = (_fld(tbl, _F_HASNEXT, step) == 1).astype(jnp.float32)
    window = jnp.concatenate([xp_ref[...] * hp, xc_ref[...], xn_ref[...] * hn], axis=0)
    acc = cbias_ref[...] + cw_ref[0:1, :] * window[SUBLANES - 2:SUBLANES - 2 + SSM_CHUNK, :]
    for k in range(1, SSM_CONV):
        off = SUBLANES - SSM_CONV // 2 + k
        acc = acc + cw_ref[k:k + 1, :] * window[off:off + SSM_CHUNK, :]
    xbc = acc * jax.nn.sigmoid(acc)
    xs = xbc[:, :SSM_INNER]
    bc = xbc[:, SSM_INNER:].astype(jnp.bfloat16)
    xs_ref[...] = xs
    bc_ref[...] = bc
    yf_ref[...] = _ssd_chunk(xs, bc[:, :SSM_BC], bc[:, SSM_BC:], dt_ref[...], dtb_ref[...], alog_ref[...],
                             ex_ref, eb_ref, s_ref, reverse=False)
    _ssd_write_state(tbl, step, hout_ref, s_ref)


def _ssd_bwd_kernel(tbl, xs_ref, bc_ref, dt_ref, yf_ref, z_ref, dtb_ref, alog_ref, dskip_ref, nw_ref,
                    ex_ref, eb_ref, h0_ref, hf_ref, y_ref, hout_ref, s_ref):
    step = pl.program_id(0)
    _ssd_init_state(tbl, step, h0_ref, s_ref)
    xs = xs_ref[...]
    bc = bc_ref[...]
    yb = _ssd_chunk(xs, bc[:, :SSM_BC], bc[:, SSM_BC:], dt_ref[...], dtb_ref[...], alog_ref[...],
                    ex_ref, eb_ref, s_ref, reverse=True)
    z = z_ref[...]
    hg = (yf_ref[...] + yb + dskip_ref[...] * xs) * (z * jax.nn.sigmoid(z))
    gw = SSM_INNER // SSM_GROUPS
    outs = []
    for g in range(SSM_GROUPS):
        hgg = hg[:, g * gw:(g + 1) * gw]
        outs.append(hgg * lax.rsqrt(jnp.mean(hgg * hgg, axis=-1, keepdims=True) + RMS_EPS))
    y_ref[...] = (jnp.concatenate(outs, axis=1) * nw_ref[...]).astype(jnp.bfloat16)

    @pl.when(_fld(tbl, _F_WRITEH, step) == 1)
    def _():
        hout_ref[0] = hf_ref[...]
        hout_ref[1] = s_ref[...].T


def _ssd_mixer(z, xbc_raw, dt_raw, conv_w, conv_b, a_log, dt_bias, d_skip, norm_w, state_in):
    f32 = jnp.float32
    row = lambda f: (lambda s, tbl: (_fld(tbl, f, s), 0))
    chunk_spec = lambda w: pl.BlockSpec((SSM_CHUNK, w), row(_F_ROW))
    const_spec = lambda shape: pl.BlockSpec(shape, lambda s, tbl: (0,) * len(shape))
    h0_spec = lambda d: pl.BlockSpec((None, None, SSM_INNER, D_STATE), lambda s, tbl: (_fld(tbl, _F_H0, s), d, 0, 0))
    hout_spec = pl.BlockSpec((None, SSM_INNER, D_STATE), lambda s, tbl: (_fld(tbl, _F_HOUT, s), 0, 0))
    hout_shape = jax.ShapeDtypeStruct((BATCH, SSM_INNER, D_STATE), f32)
    state_scratch = [pltpu.VMEM((D_STATE, SSM_INNER), f32)]

    def expanders(reverse):
        base = SSM_HEADS if reverse else 0
        ex = np.zeros((DT_PAD, SSM_INNER), np.float32)
        eb = np.zeros((DT_PAD, SSM_HEADS * SSM_CHUNK), np.float32)
        for h in range(SSM_HEADS):
            ex[base + h, h * SSM_HEAD_DIM:(h + 1) * SSM_HEAD_DIM] = 1.0
            eb[base + h, h * SSM_CHUNK:(h + 1) * SSM_CHUNK] = 1.0
        twice = lambda e: jnp.asarray(np.concatenate([e, e], axis=0), jnp.bfloat16)
        return twice(ex), twice(eb)

    expander_specs = [const_spec((2 * DT_PAD, SSM_INNER)), const_spec((2 * DT_PAD, SSM_HEADS * SSM_CHUNK))]

    cw = jnp.pad(conv_w, ((0, SUBLANES - SSM_CONV), (0, 0)))
    cbias = conv_b.reshape(1, SSM_CONV_DIM)
    dtb = jnp.pad(dt_bias.reshape(1, 2 * SSM_HEADS), ((0, 0), (0, DT_PAD - 2 * SSM_HEADS)))
    alog = jnp.pad(a_log.reshape(1, 2 * SSM_HEADS), ((0, 0), (0, DT_PAD - 2 * SSM_HEADS)))
    dskip = jnp.repeat(d_skip, SSM_HEAD_DIM).reshape(1, SSM_INNER)
    nw = norm_w.reshape(1, SSM_INNER)

    xs, bc, yf, h_f = pl.pallas_call(
        _ssd_fwd_kernel,
        grid_spec=pltpu.PrefetchScalarGridSpec(
            num_scalar_prefetch=1,
            grid=(_SSD_STEPS,),
            in_specs=[
                chunk_spec(SSM_CONV_DIM),
                pl.BlockSpec((SUBLANES, SSM_CONV_DIM), row(_F_PREV8)),
                pl.BlockSpec((SUBLANES, SSM_CONV_DIM), row(_F_NEXT8)),
                chunk_spec(DT_PAD),
                const_spec((SUBLANES, SSM_CONV_DIM)), const_spec((1, SSM_CONV_DIM)),
                const_spec((1, DT_PAD)), const_spec((1, DT_PAD)), *expander_specs,
                h0_spec(0),
            ],
            out_specs=[chunk_spec(SSM_INNER), chunk_spec(2 * SSM_BC), chunk_spec(SSM_INNER), hout_spec],
            scratch_shapes=state_scratch,
        ),
        out_shape=[
            jax.ShapeDtypeStruct((N_TOK, SSM_INNER), f32),
            jax.ShapeDtypeStruct((N_TOK, 2 * SSM_BC), jnp.bfloat16),
            jax.ShapeDtypeStruct((N_TOK, SSM_INNER), f32),
            hout_shape,
        ],
        compiler_params=_cparams(("arbitrary",)),
        name="ssd_forward",
    )(jnp.asarray(_ssd_table(False)), xbc_raw, xbc_raw, xbc_raw, dt_raw, cw, cbias, dtb, alog,
      *expanders(False), state_in)

    y, h_pair = pl.pallas_call(
        _ssd_bwd_kernel,
        grid_spec=pltpu.PrefetchScalarGridSpec(
            num_scalar_prefetch=1,
            grid=(_SSD_STEPS,),
            in_specs=[
                chunk_spec(SSM_INNER), chunk_spec(2 * SSM_BC), chunk_spec(DT_PAD), chunk_spec(SSM_INNER),
                chunk_spec(SSM_INNER),
                const_spec((1, DT_PAD)), const_spec((1, DT_PAD)),
                const_spec((1, SSM_INNER)), const_spec((1, SSM_INNER)), *expander_specs,
                h0_spec(1), hout_spec,
            ],
            out_specs=[chunk_spec(SSM_INNER),
                       pl.BlockSpec((None, 2, SSM_INNER, D_STATE), lambda s, tbl: (_fld(tbl, _F_HOUT, s), 0, 0, 0))],
            scratch_shapes=state_scratch,
        ),
        out_shape=[jax.ShapeDtypeStruct((N_TOK, SSM_INNER), jnp.bfloat16),
                   jax.ShapeDtypeStruct((BATCH, 2, SSM_INNER, D_STATE), f32)],
        compiler_params=_cparams(("arbitrary",)),
        name="ssd_backward",
    )(jnp.asarray(_ssd_table(True)), xs, bc, dt_raw, yf, z, dtb, alog, dskip, nw, *expanders(True), state_in,
      h_f)
    return y, h_pair


def _sink_attention(q_heads, keys, vals, sink_ref, kv, masks):
    nq = q_heads[0].shape[0]
    qs = jnp.concatenate(q_heads, axis=0)
    sk = jnp.concatenate([jnp.full((nq, 1), sink_ref[kv * ATT_GROUP + g], jnp.float32)
                          for g in range(ATT_GROUP)], axis=0)
    scores = []
    for kk, mask in zip(keys, masks):
        s = lax.dot_general(qs, kk, _NT, preferred_element_type=jnp.float32) * ATT_SCALE
        if mask is not None:
            s = s + jnp.concatenate([jnp.where(mask, 0.0, NEG_BIG)] * ATT_GROUP, axis=0)
        scores.append(s)
    m = sk
    for s in scores:
        m = jnp.maximum(m, jnp.max(s, axis=-1, keepdims=True))
    den = jnp.exp(sk - m)
    acc = None
    for s, vv in zip(scores, vals):
        p = jnp.exp(s - m)
        den = den + jnp.sum(p, axis=-1, keepdims=True)
        pv = jnp.dot(p.astype(jnp.bfloat16), vv, preferred_element_type=jnp.float32)
        acc = pv if acc is None else acc + pv
    out = acc / den
    return [out[g * nq:(g + 1) * nq] for g in range(ATT_GROUP)]


def _head(x, h):
    return x[:, h * ATT_HEAD_DIM:(h + 1) * ATT_HEAD_DIM]


def _attn_ctx_kernel(sink_ref, q_ref, k_ref, v_ref, o_ref):
    bf16 = jnp.bfloat16
    q = q_ref[...].astype(bf16)
    k = k_ref[...].astype(bf16)
    v = v_ref[...].astype(bf16)
    outs = []
    for kv in range(ATT_KV_HEADS):
        qs = [_head(q, kv * ATT_GROUP + g) for g in range(ATT_GROUP)]
        outs += _sink_attention(qs, [_head(k, kv)], [_head(v, kv)], sink_ref, kv, [None])
    o_ref[...] = jnp.concatenate(outs, axis=1).astype(bf16)


def _attend_context(q, k, v, sink):
    return pl.pallas_call(
        _attn_ctx_kernel,
        grid=(BATCH,),
        in_specs=[
            pl.BlockSpec(memory_space=pltpu.SMEM),
            pl.BlockSpec((SEQ, D_MODEL), lambda b: (b, 0)),
            pl.BlockSpec((SEQ, KV_DIM), lambda b: (b, 0)),
            pl.BlockSpec((SEQ, KV_DIM), lambda b: (b, 0)),
        ],
        out_specs=pl.BlockSpec((SEQ, D_MODEL), lambda b: (b, 0)),
        out_shape=jax.ShapeDtypeStruct((N_PROMPT, D_MODEL), jnp.bfloat16),
        compiler_params=_cparams(("arbitrary",)),
        name="attend_context",
    )(sink, q, k, v)


def _rope_tables(width):
    t = np.arange(DEC_SEQ)
    d = np.arange(width) % ATT_HEAD_DIM
    pos = np.where(d[None, :] < ROPE_HALF, (t // GRID_W)[:, None], (t % GRID_W)[:, None]).astype(np.float32)
    inv = (ROPE_BASE ** (-np.arange(ROPE_QUARTER, dtype=np.float32) / ROPE_QUARTER)).astype(np.float32)
    ang = pos * inv[d % ROPE_QUARTER][None, :]
    sign = np.where((d % ROPE_HALF) < ROPE_QUARTER, -1.0, 1.0).astype(np.float32)
    return jnp.asarray(np.cos(ang), jnp.float32), jnp.asarray(np.sin(ang) * sign[None, :], jnp.float32)


def _rope(x, cos, sin_signed):
    width = x.shape[1]
    lane = lax.broadcasted_iota(jnp.int32, (1, width), 1)
    first = (lane % ROPE_HALF) < ROPE_QUARTER
    partner = jnp.where(first, pltpu.roll(x, width - ROPE_QUARTER, 1), pltpu.roll(x, ROPE_QUARTER, 1))
    return x * cos + partner * sin_signed


def _rope_kernel(q_ref, k_ref, v_ref, cos_ref, sin_ref, qo_ref, ko_ref, vo_ref):
    bf16 = jnp.bfloat16
    cos, sin = cos_ref[...], sin_ref[...]
    qo_ref[...] = _rope(q_ref[...], cos, sin).astype(bf16)
    ko_ref[...] = _rope(k_ref[...], cos[:, :KV_DIM], sin[:, :KV_DIM]).astype(bf16)
    vo_ref[...] = v_ref[...].astype(bf16)


def _rope_latent(q, k, v):
    nb = DEC_SEQ // ATT_BLOCK
    off = N_PROMPT // ATT_BLOCK
    cos, sin = _rope_tables(D_MODEL)
    tok = lambda b, i: (off + b * nb + i, 0)
    out = lambda b, i: (b * nb + i, 0)
    return pl.pallas_call(
        _rope_kernel,
        grid=(DEC_BATCH, nb),
        in_specs=[
            pl.BlockSpec((ATT_BLOCK, D_MODEL), tok),
            pl.BlockSpec((ATT_BLOCK, KV_DIM), tok),
            pl.BlockSpec((ATT_BLOCK, KV_DIM), tok),
            pl.BlockSpec((ATT_BLOCK, D_MODEL), lambda b, i: (i, 0)),
            pl.BlockSpec((ATT_BLOCK, D_MODEL), lambda b, i: (i, 0)),
        ],
        out_specs=[
            pl.BlockSpec((ATT_BLOCK, D_MODEL), out),
            pl.BlockSpec((ATT_BLOCK, KV_DIM), out),
            pl.BlockSpec((ATT_BLOCK, KV_DIM), out),
        ],
        out_shape=[
            jax.ShapeDtypeStruct((N_SAMPLE, D_MODEL), jnp.bfloat16),
            jax.ShapeDtypeStruct((N_SAMPLE, KV_DIM), jnp.bfloat16),
            jax.ShapeDtypeStruct((N_SAMPLE, KV_DIM), jnp.bfloat16),
        ],
        compiler_params=_cparams(("arbitrary", "arbitrary")),
        name="rope_latent",
    )(q, k, v, cos, sin)


def _attn_lat_kernel(sink_ref, q_ref, kp_ref, kc_ref, kn_ref, vp_ref, vc_ref, vn_ref, kx_ref, vx_ref, o_ref):
    i = pl.program_id(1)
    nb = pl.num_programs(1)
    q = q_ref[...]
    k_loc = jnp.concatenate([kp_ref[...], kc_ref[...], kn_ref[...]], axis=0)
    v_loc = jnp.concatenate([vp_ref[...], vc_ref[...], vn_ref[...]], axis=0)
    kx, vx = kx_ref[...], vx_ref[...]
    r = lax.broadcasted_iota(jnp.int32, (ATT_BLOCK, 3 * ATT_BLOCK), 0)
    c = lax.broadcasted_iota(jnp.int32, (ATT_BLOCK, 3 * ATT_BLOCK), 1)
    rel = c - ATT_BLOCK - r
    in_window = jnp.logical_and(rel >= -WINDOW, rel <= WINDOW)
    in_seq = jnp.logical_and(jnp.logical_or(c >= ATT_BLOCK, i > 0),
                             jnp.logical_or(c < 2 * ATT_BLOCK, i < nb - 1))
    valid = jnp.logical_and(in_window, in_seq)
    outs = []
    for kv in range(ATT_KV_HEADS):
        qs = [_head(q, kv * ATT_GROUP + g) for g in range(ATT_GROUP)]
        outs += _sink_attention(qs, [_head(k_loc, kv), _head(kx, kv)], [_head(v_loc, kv), _head(vx, kv)],
                                sink_ref, kv, [valid, None])
    o_ref[...] = jnp.concatenate(outs, axis=1).astype(jnp.bfloat16)


def _attend_latent(qr, kr, vb, k_ctx, v_ctx, sink):
    nb = DEC_SEQ // ATT_BLOCK
    cur = lambda b, i: (b * nb + i, 0)
    prv = lambda b, i: (b * nb + jnp.maximum(i - 1, 0), 0)
    nxt = lambda b, i: (b * nb + jnp.minimum(i + 1, nb - 1), 0)
    kvs = lambda f: pl.BlockSpec((ATT_BLOCK, KV_DIM), f)
    ctx = pl.BlockSpec((None, PAST_LEN, KV_DIM), lambda b, i: (b, 0, 0))
    return pl.pallas_call(
        _attn_lat_kernel,
        grid=(DEC_BATCH, nb),
        in_specs=[
            pl.BlockSpec(memory_space=pltpu.SMEM),
            pl.BlockSpec((ATT_BLOCK, D_MODEL), cur),
            kvs(prv), kvs(cur), kvs(nxt), kvs(prv), kvs(cur), kvs(nxt), ctx, ctx,
        ],
        out_specs=pl.BlockSpec((ATT_BLOCK, D_MODEL), cur),
        out_shape=jax.ShapeDtypeStruct((N_SAMPLE, D_MODEL), jnp.bfloat16),
        compiler_params=_cparams(("arbitrary", "arbitrary")),
        name="attend_latent",
    )(sink, qr, kr, kr, kr, vb, vb, vb, k_ctx, v_ctx)


def _gconv_kernel(bg_ref, g_ref, gp_ref, gn_ref, w_ref, o_ref):
    i = pl.program_id(0)
    first = N_PROMPT // CONV_TILE
    per = DEC_SEQ // CONV_TILE
    t = (i - first) % per
    latent = i >= first
    hp = jnp.logical_and(latent, t > 0).astype(jnp.float32)
    hn = jnp.logical_and(latent, t < per - 1).astype(jnp.float32)
    window = jnp.concatenate([gp_ref[...] * hp, g_ref[...], gn_ref[...] * hn], axis=0)
    acc = None
    for k in range(SHORT_CONV):
        off = SUBLANES - SHORT_CONV // 2 + k
        term = w_ref[k:k + 1, :] * window[off:off + CONV_TILE, :]
        acc = term if acc is None else acc + term
    o_ref[...] = (bg_ref[...] * acc).astype(jnp.bfloat16)


def _gated_conv(bg, g, conv_w):
    per = CONV_TILE // SUBLANES
    last = N_TOK // SUBLANES - 1
    cw = jnp.pad(conv_w, ((0, SUBLANES - SHORT_CONV), (0, 0)))
    return pl.pallas_call(
        _gconv_kernel,
        grid=(N_TOK // CONV_TILE,),
        in_specs=[
            pl.BlockSpec((CONV_TILE, D_MODEL), lambda i: (i, 0)),
            pl.BlockSpec((CONV_TILE, D_MODEL), lambda i: (i, 0)),
            pl.BlockSpec((SUBLANES, D_MODEL), lambda i: (jnp.maximum(i * per - 1, 0), 0)),
            pl.BlockSpec((SUBLANES, D_MODEL), lambda i: (jnp.minimum(i * per + per, last), 0)),
            pl.BlockSpec((SUBLANES, D_MODEL), lambda i: (0, 0)),
        ],
        out_specs=pl.BlockSpec((CONV_TILE, D_MODEL), lambda i: (i, 0)),
        out_shape=jax.ShapeDtypeStruct((N_TOK, D_MODEL), jnp.bfloat16),
        compiler_params=_cparams(("arbitrary",)),
        name="gated_conv",
    )(bg, g, g, g, cw)


def kernel(x_prompt, x_sample, cache_k, cache_v, state_ssm, c, c_ctx,
           w_mod, b_mod, ln_w, ln_b,
           w_in_a, conv_w_a, conv_b_a, a_log, dt_bias, d_skip, ssm_norm_w, attn_sink, w_out_a,
           w_in_c, conv_w_c, w_out_c,
           w_router, b_router, w_gate, b_gate, w_lin, b_lin, w_down, b_down):
    bf16 = jnp.bfloat16
    x = (x_prompt.reshape(N_PROMPT, D_MODEL), x_sample.reshape(N_SAMPLE, D_MODEL))

    cond = jnp.concatenate([c_ctx[None, :], c, jnp.zeros((COND_PAD - N_COND, D_MODEL), jnp.float32)], axis=0)
    mod_all = _modulation(cond, w_mod, b_mod).reshape(DEPTH, COND_PAD, 6, D_MODEL)
    mod_all = jnp.pad(mod_all, ((0, 0), (0, 0), (0, MOD_ROWS - 6), (0, 0)))

    mod = mod_all[0]
    wa = w_in_a[0].astype(bf16)
    e = np.cumsum((0, SSM_INNER, SSM_CONV_DIM, 2 * SSM_HEADS, D_MODEL, KV_DIM, KV_DIM))
    w_dt = jnp.pad(wa[:, e[2]:e[3]], ((0, 0), (0, DT_PAD - 2 * SSM_HEADS)))
    w_parts = [wa[:, e[0]:e[1]], wa[:, e[1]:e[2]], w_dt, wa[:, e[3]:e[4]], wa[:, e[4]:e[5]], wa[:, e[5]:e[6]]]
    z, xbc_raw, dt_raw, q, k, v = _inproj(x, mod, w_parts)

    state_in = state_ssm[:, 0].reshape(DEC_BATCH, 2, SSM_INNER, D_STATE)
    y_ssm, h_pair = _ssd_mixer(z, xbc_raw, dt_raw, conv_w_a[0], conv_b_a[0], a_log[0], dt_bias[0],
                                 d_skip[0], ssm_norm_w[0], state_in)

    sink = attn_sink[0]
    att_p = _attend_context(q, k, v, sink)
    qr, kr, vb = _rope_latent(q, k, v)
    k_ctx = cache_k[:, 0].reshape(DEC_BATCH, PAST_LEN, KV_DIM).astype(bf16)
    v_ctx = cache_v[:, 0].reshape(DEC_BATCH, PAST_LEN, KV_DIM).astype(bf16)
    att_s = _attend_latent(qr, kr, vb, k_ctx, v_ctx, sink)
    y_att = (att_p, att_s)

    wo = w_out_a[0].astype(bf16)
    ln = jnp.stack([ln_w[0, 0], ln_b[0, 0]])
    x1, u2, *routing = _outproj([y_ssm, y_att], [wo[:SSM_INNER], wo[SSM_INNER:]], x, mod, ln,
                                w_router[0], b_router[0])
    x = _moe_layer(0, x1, u2, routing, mod, jnp.stack([ln_w[0, 1], ln_b[0, 1]]),
                   w_gate, b_gate, w_lin, b_lin, w_down, b_down)

    mod = mod_all[1]
    wc = w_in_c[0].astype(bf16)
    bg, g = _inproj(x, mod, [wc[:, :D_MODEL], wc[:, D_MODEL:2 * D_MODEL], wc[:, 2 * D_MODEL:]], gate_product=True)
    y_c = _gated_conv(bg, g, conv_w_c[0])
    ln = jnp.stack([ln_w[1, 0], ln_b[1, 0]])
    x1, u2, *routing = _outproj([y_c], [w_out_c[0].astype(bf16)], x, mod, ln, w_router[1], b_router[1])
    y_p, y_s = _moe_layer(1, x1, u2, routing, mod, jnp.stack([ln_w[1, 1], ln_b[1, 1]]),
                          w_gate, b_gate, w_lin, b_lin, w_down, b_down, split=True)

    y_prompt = y_p.reshape(BATCH, SEQ, D_MODEL)
    y_sample = y_s.reshape(DEC_BATCH, DEC_SEQ, D_MODEL)
    new_k = k[:N_PROMPT].reshape(BATCH, 1, SEQ, ATT_KV_HEADS, ATT_HEAD_DIM)
    new_v = v[:N_PROMPT].reshape(BATCH, 1, SEQ, ATT_KV_HEADS, ATT_HEAD_DIM)
    new_state = h_pair.reshape(BATCH, 1, 2, SSM_HEADS, SSM_HEAD_DIM, D_STATE)
    return (y_prompt, y_sample, new_k, new_v, new_state)
```
